```python
import jax, jax.numpy as jnp
from jax import lax
import numpy as np

D_MODEL = 1024
BATCH = 32
SEQ = 2048
DEPTH = 4

N_MIXERS = 3
N_MOD = 9
D_FF = 2816
EPS = 1e-6
RG_WIDTH = D_MODEL
RG_HEADS = 8
RG_BLOCK = RG_WIDTH // RG_HEADS
CONV_WIDTH = 4
RG_C = 8.0
HG_EXPAND = 128
HG_HEADS = D_MODEL // HG_EXPAND
GLA_HEADS = 4
GLA_KEY_DIM = D_MODEL // 2
GLA_VAL_DIM = D_MODEL
GLA_DK = GLA_KEY_DIM // GLA_HEADS
GLA_DV = GLA_VAL_DIM // GLA_HEADS
GLA_RANK = 16
GLA_LOGIT_NORM = 16.0
GLA_IN = 2 * GLA_KEY_DIM + 2 * GLA_VAL_DIM + GLA_RANK
CHUNK = 16

N_RG = len(range(0, DEPTH, N_MIXERS))
N_HG = len(range(1, DEPTH, N_MIXERS))
N_GLA = len(range(2, DEPTH, N_MIXERS))

kernel_name = 'hybrid_rglru_hgrn2_gla_macaron_adaln'


def rmsnorm(x, w):
    x32 = x.astype(jnp.float32)
    y = x32 * lax.rsqrt(jnp.mean(x32 * x32, axis=-1, keepdims=True) + EPS)
    return y.astype(x.dtype) * w


def modulate(h, shift, scale):
    return h * (1.0 + scale[:, None, :]) + shift[:, None, :]


def swiglu(h, w13, w2):
    gate, up = jnp.split(h @ w13, 2, axis=-1)
    return (jax.nn.silu(gate) * up) @ w2


def chunk_gla(q, k, v, log_f, scale):
    B, S, H, DK = q.shape
    DV = v.shape[-1]
    n = S // CHUNK

    def blocks(t):
        return t.astype(jnp.float32).reshape(B, n, CHUNK, H, t.shape[-1]).transpose(1, 0, 3, 2, 4)

    qb = blocks(q) * scale
    kb, vb = blocks(k), blocks(v)
    bcum = jnp.cumsum(blocks(log_f), axis=3)
    causal = jnp.tril(jnp.ones((CHUNK, CHUNK), bool))[:, :, None]

    def step(state, inp):
        q_c, k_c, v_c, b_c = inp
        o_inter = jnp.einsum('bhik,bhkv->bhiv', q_c * jnp.exp(b_c), state)
        rel = jnp.where(causal, b_c[:, :, :, None, :] - b_c[:, :, None, :, :], -jnp.inf)
        scores = jnp.einsum('bhik,bhjk,bhijk->bhij', q_c, k_c, jnp.exp(rel))
        o = o_inter + jnp.einsum('bhij,bhjv->bhiv', scores, v_c)
        b_last = b_c[:, :, -1:, :]
        state = (jnp.exp(b_last[:, :, 0, :, None]) * state
                 + jnp.einsum('bhjk,bhjv->bhkv', k_c * jnp.exp(b_last - b_c), v_c))
        return state, o

    state0 = jnp.zeros((B, H, DK, DV), jnp.float32)
    _, ob = lax.scan(step, state0, (qb, kb, vb, bcum))
    return ob.transpose(1, 0, 3, 2, 4).reshape(B, S, H, DV).astype(v.dtype)


def rglru_mixer(h, w_in, conv_w, conv_b, gate_w, gate_b, lam, w_out):
    B, S, _ = h.shape
    y_br, x_br = jnp.split(h @ w_in, 2, axis=-1)
    y_br = jax.nn.gelu(y_br)
    xp = jnp.pad(x_br, ((0, 0), (CONV_WIDTH - 1, 0), (0, 0)))
    xc = sum((xp[:, j:j + S] * conv_w[j] for j in range(CONV_WIDTH)), conv_b)
    xh = xc.reshape(B, S, RG_HEADS, RG_BLOCK)
    gates = jnp.einsum('bshi,hij->bshj', xh, gate_w) + gate_b
    gate_x, gate_a = jnp.split(gates.astype(jnp.float32), 2, axis=-1)
    gate_x = jax.nn.sigmoid(gate_x).reshape(B, S, RG_WIDTH)
    gate_a = jax.nn.sigmoid(gate_a).reshape(B, S, RG_WIDTH)
    log_a = -RG_C * gate_a * jax.nn.softplus(-lam.astype(jnp.float32))
    a = jnp.exp(log_a)
    mult = jnp.sqrt(-jnp.expm1(2.0 * log_a))
    mult = jnp.where(jnp.arange(S)[None, :, None] == 0, 1.0, mult)
    u = gate_x * xc.astype(jnp.float32) * mult

    def step(hc, inp):
        a_t, u_t = inp
        hn = a_t * hc + u_t
        return hn, hn

    _, hs = lax.scan(step, jnp.zeros((B, RG_WIDTH), jnp.float32),
                     (a.transpose(1, 0, 2), u.transpose(1, 0, 2)))
    hs = hs.transpose(1, 0, 2).astype(h.dtype)
    return (hs * y_br) @ w_out


def hgrn2_mixer(h, w_in, lb, norm_w, w_out):
    B, S, _ = h.shape
    q, fz, i_in, g = jnp.split(h @ w_in, 4, axis=-1)
    fz32 = fz.astype(jnp.float32)
    lb = lb.astype(jnp.float32)
    log_f = jnp.logaddexp(jnp.log(lb), jnp.log1p(-lb) + jax.nn.log_sigmoid(fz32))
    k = (1.0 - lb) * jax.nn.sigmoid(-fz32)
    heads = lambda t: t.reshape(B, S, HG_HEADS, HG_EXPAND)
    o = chunk_gla(heads(q), heads(k), heads(i_in), heads(log_f), 1.0)
    o = rmsnorm(o.reshape(B, S, D_MODEL), norm_w) * jax.nn.sigmoid(g)
    return o @ w_out


def gla_mixer(h, w_in, gate_w2, gate_b, norm_w, w_out):
    B, S, _ = h.shape
    q, k, v, g, a_low = jnp.split(
        h @ w_in,
        [GLA_KEY_DIM, 2 * GLA_KEY_DIM, 2 * GLA_KEY_DIM + GLA_VAL_DIM, 2 * GLA_KEY_DIM + 2 * GLA_VAL_DIM],
        axis=-1)
    log_a = jax.nn.log_sigmoid((a_low @ gate_w2 + gate_b).astype(jnp.float32)) / GLA_LOGIT_NORM
    kh = lambda t: t.reshape(B, S, GLA_HEADS, GLA_DK)
    vh = lambda t: t.reshape(B, S, GLA_HEADS, GLA_DV)
    o = chunk_gla(kh(q), kh(k), vh(v), kh(log_a), GLA_DK ** -0.5)
    o = rmsnorm(o, norm_w) * jax.nn.silu(vh(g))
    return o.reshape(B, S, GLA_VAL_DIM) @ w_out


def setup_inputs(seed: int = 0) -> dict:
    key = jax.random.key(seed)
    ks = jax.random.split(key, 24)
    nrm = lambda k, shape, s: jax.random.normal(k, shape, jnp.float32) * s
    u = jax.random.uniform(ks[13], (N_RG, RG_WIDTH), jnp.float32)
    a8 = 0.9 + 0.099 * u
    s = a8 ** (1.0 / RG_C)
    rg_lambda = jnp.log(s) - jnp.log1p(-s)
    return {
        'x': nrm(ks[0], (BATCH, SEQ, D_MODEL), 1.0),
        'c': nrm(ks[1], (BATCH, D_MODEL), 1.0),
        'ada_w': nrm(ks[2], (DEPTH, D_MODEL, N_MOD * D_MODEL), 0.5 * D_MODEL ** -0.5),
        'ada_b': nrm(ks[3], (DEPTH, N_MOD * D_MODEL), 0.01),
        'norm_w': 1.0 + nrm(ks[4], (DEPTH, 3, D_MODEL), 0.02),
        'final_norm_w': 1.0 + nrm(ks[5], (D_MODEL,), 0.02),
        'ffn_w13': nrm(ks[6], (DEPTH, 2, D_MODEL, 2 * D_FF), D_MODEL ** -0.5),
        'ffn_w2': nrm(ks[7], (DEPTH, 2, D_FF, D_MODEL), D_FF ** -0.5),
        'rg_w_in': nrm(ks[8], (N_RG, D_MODEL, 2 * RG_WIDTH), D_MODEL ** -0.5),
        'rg_conv_w': nrm(ks[9], (N_RG, CONV_WIDTH, RG_WIDTH), CONV_WIDTH ** -0.5),
        'rg_conv_b': nrm(ks[10], (N_RG, RG_WIDTH), 0.01),
        'rg_gate_w': nrm(ks[11], (N_RG, RG_HEADS, RG_BLOCK, 2 * RG_BLOCK), RG_BLOCK ** -0.5),
        'rg_gate_b': nrm(ks[12], (N_RG, RG_HEADS, 2 * RG_BLOCK), 0.01),
        'rg_lambda': rg_lambda,
        'rg_w_out': nrm(ks[14], (N_RG, RG_WIDTH, D_MODEL), RG_WIDTH ** -0.5),
        'hg_w_in': nrm(ks[15], (N_HG, D_MODEL, 4 * D_MODEL), D_MODEL ** -0.5),
        'hg_lb_logits': nrm(ks[16], (DEPTH, D_MODEL), 0.1),
        'hg_norm_w': 1.0 + nrm(ks[17], (N_HG, D_MODEL), 0.02),
        'hg_w_out': nrm(ks[18], (N_HG, D_MODEL, D_MODEL), D_MODEL ** -0.5),
        'gla_w_in': nrm(ks[19], (N_GLA, D_MODEL, GLA_IN), D_MODEL ** -0.5),
        'gla_gate_w2': nrm(ks[20], (N_GLA, GLA_RANK, GLA_KEY_DIM), GLA_RANK ** -0.5),
        'gla_gate_b': nrm(ks[21], (N_GLA, GLA_KEY_DIM), 0.01),
        'gla_norm_w': 1.0 + nrm(ks[22], (N_GLA, GLA_DV), 0.02),
        'gla_w_out': nrm(ks[23], (N_GLA, GLA_VAL_DIM, D_MODEL), GLA_VAL_DIM ** -0.5),
    }


def reference(x, c, ada_w, ada_b, norm_w, final_norm_w, ffn_w13, ffn_w2,
              rg_w_in, rg_conv_w, rg_conv_b, rg_gate_w, rg_gate_b, rg_lambda, rg_w_out,
              hg_w_in, hg_lb_logits, hg_norm_w, hg_w_out,
              gla_w_in, gla_gate_w2, gla_gate_b, gla_norm_w, gla_w_out):
    c_act = jax.nn.silu(c)
    p = jax.nn.softmax(hg_lb_logits.astype(jnp.float32), axis=0)
    lower_bounds = jnp.cumsum(p, axis=0) - p[0]
    i_rg = i_hg = i_gla = 0
    for l in range(DEPTH):
        mod = c_act @ ada_w[l] + ada_b[l]
        sh1, sc1, g1, sh2, sc2, g2, sh3, sc3, g3 = jnp.split(mod, N_MOD, axis=-1)
        h = modulate(rmsnorm(x, norm_w[l, 0]), sh1, sc1)
        x = x + 0.5 * g1[:, None, :] * swiglu(h, ffn_w13[l, 0], ffn_w2[l, 0])
        h = modulate(rmsnorm(x, norm_w[l, 1]), sh2, sc2)
        m = l % N_MIXERS
        if m == 0:
            y = rglru_mixer(h, rg_w_in[i_rg], rg_conv_w[i_rg], rg_conv_b[i_rg], rg_gate_w[i_rg],
                            rg_gate_b[i_rg], rg_lambda[i_rg], rg_w_out[i_rg])
            i_rg += 1
        elif m == 1:
            y = hgrn2_mixer(h, hg_w_in[i_hg], lower_bounds[l], hg_norm_w[i_hg], hg_w_out[i_hg])
            i_hg += 1
        else:
            y = gla_mixer(h, gla_w_in[i_gla], gla_gate_w2[i_gla], gla_gate_b[i_gla],
                          gla_norm_w[i_gla], gla_w_out[i_gla])
            i_gla += 1
        x = x + g2[:, None, :] * y
        h = modulate(rmsnorm(x, norm_w[l, 2]), sh3, sc3)
        x = x + 0.5 * g3[:, None, :] * swiglu(h, ffn_w13[l, 1], ffn_w2[l, 1])
    return rmsnorm(x, final_norm_w)
```

```python
import functools

import jax
import jax.numpy as jnp
from jax import lax
from jax.experimental import pallas as pl
from jax.experimental.pallas import tpu as pltpu

F32 = jnp.float32
BF16 = jnp.bfloat16

N_MIXERS = 3
N_MOD = 9
EPS = 1e-6
RG_HEADS = 8
CONV_WIDTH = 4
RG_C = 8.0
HG_EXPAND = 128
GLA_HEADS = 4
GLA_RANK = 16
GLA_LOGIT_NORM = 16.0

LANES = 128
SUBLANES = 8
VMEM_LIMIT_BYTES = 56 * 1024 * 1024

FFN_ROWS = 512
MIX_ROWS = 512
GLA_CHUNK = 64


def _params():
    return pltpu.CompilerParams(dimension_semantics=("arbitrary", "arbitrary"),
                                vmem_limit_bytes=VMEM_LIMIT_BYTES)


def _resident(shape):
    zeros = (0,) * len(shape)
    return pl.BlockSpec(shape, lambda *_: zeros, pipeline_mode=pl.Buffered(1))


def _dot(a, b):
    return jnp.dot(a, b, preferred_element_type=F32)


def _dot_nt(a, b):
    return lax.dot_general(a, b, (((1,), (1,)), ((), ())), preferred_element_type=F32)


def _dot_tn(a, b):
    return lax.dot_general(a, b, (((0,), (0,)), ((), ())), preferred_element_type=F32)


def _sigmoid(x):
    return 1.0 / (1.0 + jnp.exp(-x))


def _softplus(x):
    return jnp.maximum(x, 0.0) + jnp.log1p(jnp.exp(-jnp.abs(x)))


def _log_sigmoid(x):
    return jnp.minimum(x, 0.0) - jnp.log1p(jnp.exp(-jnp.abs(x)))


def _rmsnorm(x, w):
    ms = jnp.mean(x * x, axis=-1, keepdims=True)
    return (x * lax.rsqrt(ms + EPS)) * w


def _norm_mod(x, nw, shift, scale):
    return _rmsnorm(x, nw) * (1.0 + scale) + shift


def _mod_kernel(c_ref, w_ref, b_ref, o_ref):
    c = c_ref[...]
    ca = c * _sigmoid(c)
    w = w_ref[0]
    c_hi = ca.astype(BF16)
    c_lo = (ca - c_hi.astype(F32)).astype(BF16)
    w_hi = w.astype(BF16)
    w_lo = (w - w_hi.astype(F32)).astype(BF16)
    acc = _dot(c_hi, w_hi) + (_dot(c_hi, w_lo) + _dot(c_lo, w_hi))
    o_ref[0] = acc + b_ref[0]


def _mod_call(c, ada_w, ada_b):
    depth, d, _ = ada_w.shape
    b = c.shape[0]
    return pl.pallas_call(
        _mod_kernel,
        grid=(depth, N_MOD),
        in_specs=[
            pl.BlockSpec((b, d), lambda l, j: (0, 0)),
            pl.BlockSpec((1, d, d), lambda l, j: (l, 0, j)),
            pl.BlockSpec((1, 1, d), lambda l, j: (l, 0, j)),
        ],
        out_specs=pl.BlockSpec((1, b, d), lambda l, j: (l * N_MOD + j, 0, 0)),
        out_shape=jax.ShapeDtypeStruct((depth * N_MOD, b, d), F32),
        compiler_params=_params(),
        name="adaln_table",
    )(c, ada_w, ada_b.reshape(depth, 1, N_MOD * d))


def _ffn_kernel(x_ref, nw_ref, sh_ref, sc_ref, g_ref, w13_ref, w2_ref, *rest, d_ff, final):
    if final:
        fw_ref, o_ref = rest
    else:
        (o_ref,) = rest
    x = x_ref[0]
    h = _norm_mod(x, nw_ref[...], sh_ref[0], sc_ref[0]).astype(BF16)
    hh = _dot(h, w13_ref[...])
    gate = hh[:, :d_ff]
    up = hh[:, d_ff:]
    act = (gate * _sigmoid(gate) * up).astype(BF16)
    y = _dot(act, w2_ref[...])
    out = x + (0.5 * g_ref[0]) * y
    if final:
        out = _rmsnorm(out, fw_ref[...])
    o_ref[0] = out


def _ffn_call(x, nw, sh, sc, g, w13, w2, final_w=None):
    b, s, d = x.shape
    d_ff = w2.shape[0]
    tm = min(FFN_ROWS, s)
    tok = pl.BlockSpec((1, tm, d), lambda i, t: (i, t, 0))
    vec = pl.BlockSpec((1, 1, d), lambda i, t: (i, 0, 0))
    in_specs = [tok, _resident((1, d)), vec, vec, vec, _resident(w13.shape), _resident(w2.shape)]
    args = [x, nw.reshape(1, d), sh, sc, g, w13, w2]
    if final_w is not None:
        in_specs.append(_resident((1, d)))
        args.append(final_w.reshape(1, d))
    return pl.pallas_call(
        functools.partial(_ffn_kernel, d_ff=d_ff, final=final_w is not None),
        grid=(b, s // tm),
        in_specs=in_specs,
        out_specs=tok,
        out_shape=jax.ShapeDtypeStruct(x.shape, F32),
        compiler_params=_params(),
        name="swiglu_final" if final_w is not None else "swiglu",
    )(*args)


def _gelu_tanh(x):
    return x * (0.5 * (1.0 + jnp.tanh(0.7978845608028654 * (x + 0.044715 * (x * x * x)))))


def _rg_kernel(x_ref, nw_ref, sh_ref, sc_ref, g_ref, win_ref, cw_ref, cb_ref, gw_ref, gb_ref, lam_ref,
               wout_ref, o_ref, conv_scr, h_scr, a_scr, u_scr, *, tt, width):
    t = pl.program_id(1)

    @pl.when(t == 0)
    def _():
        conv_scr[...] = jnp.zeros_like(conv_scr)
        h_scr[...] = jnp.zeros_like(h_scr)

    x = x_ref[0]
    h = _norm_mod(x, nw_ref[...], sh_ref[0], sc_ref[0]).astype(BF16)
    yx = _dot(h, win_ref[...])
    y = _gelu_tanh(yx[:, :width])
    xb = yx[:, width:]

    ext = jnp.concatenate([conv_scr[...], xb], axis=0)
    cw = cw_ref[...]
    xc = cb_ref[...]
    for j in range(CONV_WIDTH - 1):
        off = SUBLANES - (CONV_WIDTH - 1) + j
        xc = xc + ext[off:off + tt] * cw[j:j + 1]
    xc = xc + xb * cw[CONV_WIDTH - 1:CONV_WIDTH]
    conv_scr[...] = xb[tt - SUBLANES:tt]

    blk = width // RG_HEADS
    xcb = xc.astype(BF16)
    gx, ga = [], []
    for hd in range(RG_HEADS):
        gts = _dot(xcb[:, hd * blk:(hd + 1) * blk], gw_ref[hd]) + gb_ref[hd]
        gx.append(gts[:, :blk])
        ga.append(gts[:, blk:])
    gate_x = _sigmoid(jnp.concatenate(gx, axis=1))
    gate_a = _sigmoid(jnp.concatenate(ga, axis=1))

    log_a = (-RG_C * gate_a) * _softplus(-lam_ref[...])
    a = jnp.exp(log_a)
    mult = jnp.sqrt(-jnp.tanh(log_a) * (a * a + 1.0))
    row = lax.broadcasted_iota(jnp.int32, (tt, 1), 0)
    mult = jnp.where(jnp.logical_and(row == 0, t == 0), 1.0, mult)
    a_scr[...] = a
    u_scr[...] = gate_x * xc * mult

    srow = lax.broadcasted_iota(jnp.int32, (SUBLANES, width), 0)

    def group(gi, carry):
        r0 = pl.multiple_of(gi * SUBLANES, SUBLANES)
        av = a_scr[pl.ds(r0, SUBLANES), :]
        uv = u_scr[pl.ds(r0, SUBLANES), :]
        for k in (1, 2, 4):
            keep = srow >= k
            ush = jnp.where(keep, pltpu.roll(uv, k, 0), 0.0)
            ash = jnp.where(keep, pltpu.roll(av, k, 0), 1.0)
            uv = av * ush + uv
            av = av * ash
        hv = uv + av * carry
        u_scr[pl.ds(r0, SUBLANES), :] = hv
        return hv[SUBLANES - 1:SUBLANES, :]

    h_scr[...] = lax.fori_loop(0, tt // SUBLANES, group, h_scr[...], unroll=2)
    hs = u_scr[...]
    out = _dot((hs * y).astype(BF16), wout_ref[...])
    o_ref[0] = x + g_ref[0] * out


def _rg_call(x, nw, sh, sc, g, w_in, conv_w, conv_b, gate_w, gate_b, lam, w_out):
    b, s, d = x.shape
    width = w_out.shape[0]
    blk = width // RG_HEADS
    tt = min(MIX_ROWS, s)
    tok = pl.BlockSpec((1, tt, d), lambda i, t: (i, t, 0))
    vec = pl.BlockSpec((1, 1, d), lambda i, t: (i, 0, 0))
    return pl.pallas_call(
        functools.partial(_rg_kernel, tt=tt, width=width),
        grid=(b, s // tt),
        in_specs=[tok, _resident((1, d)), vec, vec, vec, _resident(w_in.shape), _resident((CONV_WIDTH, width)),
                  _resident((1, width)), _resident(gate_w.shape), _resident((RG_HEADS, 1, 2 * blk)),
                  _resident((1, width)), _resident(w_out.shape)],
        out_specs=tok,
        out_shape=jax.ShapeDtypeStruct(x.shape, F32),
        scratch_shapes=[pltpu.VMEM((SUBLANES, width), F32), pltpu.VMEM((1, width), F32),
                        pltpu.VMEM((tt, width), F32), pltpu.VMEM((tt, width), F32)],
        compiler_params=_params(),
        name="rglru_mixer",
    )(x, nw.reshape(1, d), sh, sc, g, w_in, conv_w, conv_b.reshape(1, width), gate_w,
      gate_b.reshape(RG_HEADS, 1, 2 * blk), lam.reshape(1, width), w_out)


def _cumsum_rows(v, rows):
    n = v.shape[1]
    groups = rows // SUBLANES
    v3 = v.reshape(groups, SUBLANES, n)
    srow = lax.broadcasted_iota(jnp.int32, v3.shape, 1)
    for k in (1, 2, 4):
        v3 = v3 + jnp.where(srow >= k, pltpu.roll(v3, k, 1), 0.0)
    out, carry = [], None
    for gi in range(groups):
        blk = v3[gi] if carry is None else v3[gi] + carry
        out.append(blk)
        carry = blk[SUBLANES - 1:SUBLANES, :]
    return jnp.concatenate(out, axis=0)


def _anchor(b, half, rows):
    n = b.shape[1]
    if 2 * half >= SUBLANES:
        v = b.reshape(rows // (2 * half), 2 * half, n)
        return jnp.broadcast_to(v[:, half - 1:half, :], v.shape).reshape(rows, n)
    v = b.reshape(rows // SUBLANES, SUBLANES, n)
    srow = lax.broadcasted_iota(jnp.int32, v.shape, 1)
    if half == 2:
        lo = jnp.broadcast_to(v[:, 1:2, :], v.shape)
        hi = jnp.broadcast_to(v[:, 5:6, :], v.shape)
        return jnp.where(srow < 4, lo, hi).reshape(rows, n)
    assert half == 1
    return jnp.where((srow & 1) == 1, pltpu.roll(v, 1, 1), v).reshape(rows, n)


def _gla_masks(rows):
    qi = lax.broadcasted_iota(jnp.int32, (rows, rows), 0)
    kj = lax.broadcasted_iota(jnp.int32, (rows, rows), 1)
    masks = []
    half, lvl = 1, 0
    while half < rows:
        same = (qi >> (lvl + 1)) == (kj >> (lvl + 1))
        masks.append(same & ((qi & half) != 0) & ((kj & half) == 0))
        half, lvl = half * 2, lvl + 1
    return qi == kj, masks


def _gla_chunk(q, k, v, lf, st_ref, eye, masks, *, heads, dk, dv, rows):
    b = _cumsum_rows(lf, rows)
    b_last = b[rows - 1:rows, :]
    qe = (q * jnp.exp(b)).astype(BF16)
    kd = (k * jnp.exp(b_last - b)).astype(BF16)
    eb = jnp.exp(b_last)
    qb = q.astype(BF16)
    kb = k.astype(BF16)
    vb = v.astype(BF16)
    rown = lax.broadcasted_iota(jnp.int32, b.shape, 0)
    mixed = []
    half = 1
    while half < rows:
        e = jnp.exp(-jnp.abs(b - _anchor(b, half, rows)))
        mixed.append((jnp.where((rown & half) != 0, q, k) * e).astype(BF16))
        half *= 2
    outs = []
    for hd in range(heads):
        ks = slice(hd * dk, (hd + 1) * dk)
        vs = slice(hd * dv, (hd + 1) * dv)
        sc = jnp.where(eye, _dot_nt(qb[:, ks], kb[:, ks]), 0.0)
        for m, mk in zip(mixed, masks):
            sc = sc + jnp.where(mk, _dot_nt(m[:, ks], m[:, ks]), 0.0)
        st = st_ref[hd]
        outs.append(_dot(sc.astype(BF16), vb[:, vs]) + _dot_nt(qe[:, ks], st.astype(BF16)))
        st_ref[hd] = st * eb[:, ks] + _dot_tn(vb[:, vs], kd[:, ks])
    return jnp.concatenate(outs, axis=1)


def _hg_kernel(x_ref, nw_ref, sh_ref, sc_ref, g_ref, win_ref, lbl_ref, onw_ref, wout_ref, o_ref,
               p_scr, o_scr, st_scr, *, tt, d, layer, heads):
    t = pl.program_id(1)

    @pl.when(t == 0)
    def _():
        st_scr[...] = jnp.zeros_like(st_scr)

    x = x_ref[0]
    h = _norm_mod(x, nw_ref[...], sh_ref[0], sc_ref[0]).astype(BF16)
    p_scr[...] = _dot(h, win_ref[...])

    lg = lbl_ref[...]
    rows_l = [lg[i:i + 1, :] for i in range(lg.shape[0])]
    mx = functools.reduce(jnp.maximum, rows_l)
    ex = [jnp.exp(r - mx) for r in rows_l]
    den = functools.reduce(lambda u, w: u + w, ex)
    lb = jnp.zeros_like(den)
    for i in range(1, layer + 1):
        lb = lb + ex[i] / den
    log_lb = jnp.log(lb)
    log_1m = jnp.log1p(-lb)
    one_m = 1.0 - lb

    eye, masks = _gla_masks(GLA_CHUNK)

    def chunk(ci, carry):
        r0 = pl.multiple_of(ci * GLA_CHUNK, GLA_CHUNK)
        rs = pl.ds(r0, GLA_CHUNK)
        q = p_scr[rs, 0:d]
        fz = p_scr[rs, d:2 * d]
        v = p_scr[rs, 2 * d:3 * d]
        bv = log_1m + _log_sigmoid(fz)
        lf = jnp.maximum(log_lb, bv) + jnp.log1p(jnp.exp(-jnp.abs(log_lb - bv)))
        k = one_m * _sigmoid(-fz)
        o_scr[rs, :] = _gla_chunk(q, k, v, lf, st_scr, eye, masks, heads=heads, dk=d // heads, dv=d // heads,
                                  rows=GLA_CHUNK)
        return carry

    lax.fori_loop(0, tt // GLA_CHUNK, chunk, 0)

    o = _rmsnorm(o_scr[...], onw_ref[...]) * _sigmoid(p_scr[:, 3 * d:4 * d])
    out = _dot(o.astype(BF16), wout_ref[...])
    o_ref[0] = x + g_ref[0] * out


def _hg_call(x, nw, sh, sc, g, w_in, lb_logits, onw, w_out, layer):
    b, s, d = x.shape
    heads = d // HG_EXPAND
    tt = min(MIX_ROWS, s)
    tok = pl.BlockSpec((1, tt, d), lambda i, t: (i, t, 0))
    vec = pl.BlockSpec((1, 1, d), lambda i, t: (i, 0, 0))
    return pl.pallas_call(
        functools.partial(_hg_kernel, tt=tt, d=d, layer=layer, heads=heads),
        grid=(b, s // tt),
        in_specs=[tok, _resident((1, d)), vec, vec, vec, _resident(w_in.shape), _resident(lb_logits.shape),
                  _resident((1, d)), _resident(w_out.shape)],
        out_specs=tok,
        out_shape=jax.ShapeDtypeStruct(x.shape, F32),
        scratch_shapes=[pltpu.VMEM((tt, 4 * d), F32), pltpu.VMEM((tt, d), F32),
                        pltpu.VMEM((heads, HG_EXPAND, HG_EXPAND), F32)],
        compiler_params=_params(),
        name="hgrn2_mixer",
    )(x, nw.reshape(1, d), sh, sc, g, w_in, lb_logits, onw.reshape(1, d), w_out)


def _gla_kernel(x_ref, nw_ref, sh_ref, sc_ref, g_ref, win_ref, gw2_ref, gb_ref, onw_ref, wout_ref, o_ref,
                p_scr, o_scr, st_scr, *, tt, kd, vd, heads):
    t = pl.program_id(1)

    @pl.when(t == 0)
    def _():
        st_scr[...] = jnp.zeros_like(st_scr)

    x = x_ref[0]
    h = _norm_mod(x, nw_ref[...], sh_ref[0], sc_ref[0]).astype(BF16)
    p_scr[...] = _dot(h, win_ref[...])
    dk = kd // heads
    dv = vd // heads
    scale = dk ** -0.5
    low0 = 2 * kd + 2 * vd

    eye, masks = _gla_masks(GLA_CHUNK)

    def chunk(ci, carry):
        r0 = pl.multiple_of(ci * GLA_CHUNK, GLA_CHUNK)
        rs = pl.ds(r0, GLA_CHUNK)
        q = p_scr[rs, 0:kd] * scale
        k = p_scr[rs, kd:2 * kd]
        v = p_scr[rs, 2 * kd:2 * kd + vd]
        a_low = p_scr[rs, low0:low0 + LANES].astype(BF16)
        lf = _log_sigmoid(_dot(a_low, gw2_ref[...]) + gb_ref[...]) / GLA_LOGIT_NORM
        o_scr[rs, :] = _gla_chunk(q, k, v, lf, st_scr, eye, masks, heads=heads, dk=dk, dv=dv, rows=GLA_CHUNK)
        return carry

    lax.fori_loop(0, tt // GLA_CHUNK, chunk, 0)

    gate = p_scr[:, 2 * kd + vd:2 * kd + 2 * vd]
    onw = onw_ref[...]
    parts = []
    for hd in range(heads):
        vs = slice(hd * dv, (hd + 1) * dv)
        gh = gate[:, vs]
        parts.append(_rmsnorm(o_scr[:, vs], onw) * (gh * _sigmoid(gh)))
    out = _dot(jnp.concatenate(parts, axis=1).astype(BF16), wout_ref[...])
    o_ref[0] = x + g_ref[0] * out


def _gla_call(x, nw, sh, sc, g, w_in, gate_w2, gate_b, onw, w_out):
    b, s, d = x.shape
    kd = gate_w2.shape[1]
    vd = w_out.shape[0]
    heads = GLA_HEADS
    dv = vd // heads
    tt = min(MIX_ROWS, s)
    cols = 2 * kd + 2 * vd + LANES
    w_in_p = jnp.zeros((d, cols), BF16).at[:, :w_in.shape[1]].set(w_in)
    gw2_p = jnp.zeros((LANES, kd), BF16).at[:GLA_RANK, :].set(gate_w2)
    tok = pl.BlockSpec((1, tt, d), lambda i, t: (i, t, 0))
    vec = pl.BlockSpec((1, 1, d), lambda i, t: (i, 0, 0))
    return pl.pallas_call(
        functools.partial(_gla_kernel, tt=tt, kd=kd, vd=vd, heads=heads),
        grid=(b, s // tt),
        in_specs=[tok, _resident((1, d)), vec, vec, vec, _resident(w_in_p.shape), _resident(gw2_p.shape),
                  _resident((1, kd)), _resident((1, dv)), _resident(w_out.shape)],
        out_specs=tok,
        out_shape=jax.ShapeDtypeStruct(x.shape, F32),
        scratch_shapes=[pltpu.VMEM((tt, cols), F32), pltpu.VMEM((tt, vd), F32),
                        pltpu.VMEM((heads, dv, kd // heads), F32)],
        compiler_params=_params(),
        name="gla_mixer",
    )(x, nw.reshape(1, d), sh, sc, g, w_in_p, gw2_p, gate_b.reshape(1, kd), onw.reshape(1, dv), w_out)


def kernel(x, c, ada_w, ada_b, norm_w, final_norm_w, ffn_w13, ffn_w2, rg_w_in, rg_conv_w, rg_conv_b, rg_gate_w,
           rg_gate_b, rg_lambda, rg_w_out, hg_w_in, hg_lb_logits, hg_norm_w, hg_w_out, gla_w_in, gla_gate_w2,
           gla_gate_b, gla_norm_w, gla_w_out):
    depth = ada_w.shape[0]
    b, s, d = x.shape
    assert s % min(FFN_ROWS, s) == 0 and s % min(MIX_ROWS, s) == 0 and min(MIX_ROWS, s) % GLA_CHUNK == 0
    mods = _mod_call(c, ada_w, ada_b).reshape(depth * N_MOD, b, 1, d)
    bf = lambda w: w.astype(BF16)
    i_rg = i_hg = i_gla = 0
    for l in range(depth):
        sh1, sc1, g1, sh2, sc2, g2, sh3, sc3, g3 = [mods[l * N_MOD + j] for j in range(N_MOD)]
        x = _ffn_call(x, norm_w[l, 0], sh1, sc1, g1, bf(ffn_w13[l, 0]), bf(ffn_w2[l, 0]))
        m = l % N_MIXERS
        if m == 0:
            x = _rg_call(x, norm_w[l, 1], sh2, sc2, g2, bf(rg_w_in[i_rg]), rg_conv_w[i_rg], rg_conv_b[i_rg],
                         bf(rg_gate_w[i_rg]), rg_gate_b[i_rg], rg_lambda[i_rg], bf(rg_w_out[i_rg]))
            i_rg += 1
        elif m == 1:
            x = _hg_call(x, norm_w[l, 1], sh2, sc2, g2, bf(hg_w_in[i_hg]), hg_lb_logits, hg_norm_w[i_hg],
                         bf(hg_w_out[i_hg]), l)
            i_hg += 1
        else:
            x = _gla_call(x, norm_w[l, 1], sh2, sc2, g2, bf(gla_w_in[i_gla]), bf(gla_gate_w2[i_gla]),
                          gla_gate_b[i_gla], gla_norm_w[i_gla], bf(gla_w_out[i_gla]))
            i_gla += 1
        x = _ffn_call(x, norm_w[l, 2], sh3, sc3, g3, bf(ffn_w13[l, 1]), bf(ffn_w2[l, 1]),
                      final_w=final_norm_w if l == depth - 1 else None)
    return x
```

```python
import functools

import jax
import jax.numpy as jnp
from jax import lax
from jax.experimental import pallas as pl
from jax.experimental.pallas import tpu as pltpu

F32 = jnp.float32
BF16 = jnp.bfloat16

N_MIXERS = 3
N_MOD = 9
EPS = 1e-6
RG_HEADS = 8
CONV_WIDTH = 4
RG_C = 8.0
HG_EXPAND = 128
GLA_HEADS = 4
GLA_RANK = 16
GLA_LOGIT_NORM = 16.0
LOG2E = 1.4426950408889634

LANES = 128
SUBLANES = 8
VMEM_LIMIT_BYTES = 56 * 1024 * 1024

FFN_ROWS = 512
MIX_ROWS = 512
GLA_CHUNK = 64
GLA_UNROLL = 4


def _params():
    return pltpu.CompilerParams(dimension_semantics=("arbitrary", "arbitrary"),
                                vmem_limit_bytes=VMEM_LIMIT_BYTES)


def _resident(shape):
    zeros = (0,) * len(shape)
    return pl.BlockSpec(shape, lambda *_: zeros, pipeline_mode=pl.Buffered(1))


def _dot(a, b):
    return jnp.dot(a, b, preferred_element_type=F32)


def _dot_nt(a, b):
    return lax.dot_general(a, b, (((1,), (1,)), ((), ())), preferred_element_type=F32)


def _dot_tn(a, b):
    return lax.dot_general(a, b, (((0,), (0,)), ((), ())), preferred_element_type=F32)


def _neg_abs(x):
    bits = lax.bitcast_convert_type(x, jnp.uint32) | jnp.uint32(0x80000000)
    return lax.bitcast_convert_type(bits, F32)


def _sigmoid(x):
    return 1.0 / (1.0 + jnp.exp(-x))


def _softplus(x):
    return jnp.maximum(x, 0.0) + jnp.log1p(jnp.exp(-jnp.abs(x)))


def _log_sigmoid(x):
    return jnp.minimum(x, 0.0) - jnp.log(1.0 + jnp.exp(_neg_abs(x)))


def _rmsnorm(x, w):
    ms = jnp.mean(x * x, axis=-1, keepdims=True)
    return (x * lax.rsqrt(ms + EPS)) * w


def _norm_mod(x, nw, shift, scale):
    return _rmsnorm(x, nw) * (1.0 + scale) + shift


def _mod_kernel(c_ref, w_ref, b_ref, o_ref):
    c = c_ref[...]
    ca = c * _sigmoid(c)
    w = w_ref[0]
    c_hi = ca.astype(BF16)
    c_lo = (ca - c_hi.astype(F32)).astype(BF16)
    w_hi = w.astype(BF16)
    w_lo = (w - w_hi.astype(F32)).astype(BF16)
    acc = _dot(c_hi, w_hi) + (_dot(c_hi, w_lo) + _dot(c_lo, w_hi))
    o_ref[0] = acc + b_ref[0]


def _mod_call(c, ada_w, ada_b):
    depth, d, _ = ada_w.shape
    b = c.shape[0]
    return pl.pallas_call(
        _mod_kernel,
        grid=(depth, N_MOD),
        in_specs=[
            pl.BlockSpec((b, d), lambda l, j: (0, 0)),
            pl.BlockSpec((1, d, d), lambda l, j: (l, 0, j)),
            pl.BlockSpec((1, 1, d), lambda l, j: (l, 0, j)),
        ],
        out_specs=pl.BlockSpec((1, b, d), lambda l, j: (l * N_MOD + j, 0, 0)),
        out_shape=jax.ShapeDtypeStruct((depth * N_MOD, b, d), F32),
        compiler_params=_params(),
        name="adaln_table",
    )(c, ada_w, ada_b.reshape(depth, 1, N_MOD * d))


def _ffn_kernel(x_ref, nw_ref, sh_ref, sc_ref, g_ref, w13_ref, w2_ref, *rest, d_ff, final):
    if final:
        fw_ref, o_ref = rest
    else:
        (o_ref,) = rest
    x = x_ref[0]
    h = _norm_mod(x, nw_ref[...], sh_ref[0], sc_ref[0]).astype(BF16)
    hh = _dot(h, w13_ref[...])
    gate = hh[:, :d_ff]
    up = hh[:, d_ff:]
    act = (gate * _sigmoid(gate) * up).astype(BF16)
    y = _dot(act, w2_ref[...])
    out = x + (0.5 * g_ref[0]) * y
    if final:
        out = _rmsnorm(out, fw_ref[...])
    o_ref[0] = out


def _ffn_call(x, nw, sh, sc, g, w13, w2, final_w=None):
    b, s, d = x.shape
    d_ff = w2.shape[0]
    tm = min(FFN_ROWS, s)
    tok = pl.BlockSpec((1, tm, d), lambda i, t: (i, t, 0))
    vec = pl.BlockSpec((1, 1, d), lambda i, t: (i, 0, 0))
    in_specs = [tok, _resident((1, d)), vec, vec, vec, _resident(w13.shape), _resident(w2.shape)]
    args = [x, nw.reshape(1, d), sh, sc, g, w13, w2]
    if final_w is not None:
        in_specs.append(_resident((1, d)))
        args.append(final_w.reshape(1, d))
    return pl.pallas_call(
        functools.partial(_ffn_kernel, d_ff=d_ff, final=final_w is not None),
        grid=(b, s // tm),
        in_specs=in_specs,
        out_specs=tok,
        out_shape=jax.ShapeDtypeStruct(x.shape, F32),
        compiler_params=_params(),
        name="swiglu_final" if final_w is not None else "swiglu",
    )(*args)


def _gelu_tanh(x):
    return x * (0.5 * (1.0 + jnp.tanh(0.7978845608028654 * (x + 0.044715 * (x * x * x)))))


def _rg_kernel(x_ref, nw_ref, sh_ref, sc_ref, g_ref, win_ref, cw_ref, cb_ref, gw_ref, gb_ref, lam_ref,
               wout_ref, o_ref, conv_scr, h_scr, a_scr, u_scr, *, tt, width):
    t = pl.program_id(1)

    @pl.when(t == 0)
    def _():
        conv_scr[...] = jnp.zeros_like(conv_scr)
        h_scr[...] = jnp.zeros_like(h_scr)

    x = x_ref[0]
    h = _norm_mod(x, nw_ref[...], sh_ref[0], sc_ref[0]).astype(BF16)
    yx = _dot(h, win_ref[...])
    y = _gelu_tanh(yx[:, :width])
    xb = yx[:, width:]

    ext = jnp.concatenate([conv_scr[...], xb], axis=0)
    cw = cw_ref[...]
    xc = cb_ref[...]
    for j in range(CONV_WIDTH - 1):
        off = SUBLANES - (CONV_WIDTH - 1) + j
        xc = xc + ext[off:off + tt] * cw[j:j + 1]
    xc = xc + xb * cw[CONV_WIDTH - 1:CONV_WIDTH]
    conv_scr[...] = xb[tt - SUBLANES:tt]

    blk = width // RG_HEADS
    xcb = xc.astype(BF16)
    gx, ga = [], []
    for hd in range(RG_HEADS):
        gts = _dot(xcb[:, hd * blk:(hd + 1) * blk], gw_ref[hd]) + gb_ref[hd]
        gx.append(gts[:, :blk])
        ga.append(gts[:, blk:])
    gate_x = _sigmoid(jnp.concatenate(gx, axis=1))
    gate_a = _sigmoid(jnp.concatenate(ga, axis=1))

    log_a = (-RG_C * gate_a) * _softplus(-lam_ref[...])
    a = jnp.exp(log_a)
    m2 = -jnp.tanh(log_a) * (a * a + 1.0)
    mult = jnp.where(m2 > 0.0, m2 * lax.rsqrt(m2), 0.0)
    row = lax.broadcasted_iota(jnp.int32, (tt, 1), 0)
    mult = jnp.where(jnp.logical_and(row == 0, t == 0), 1.0, mult)
    a_scr[...] = a
    u_scr[...] = gate_x * xc * mult

    srow = lax.broadcasted_iota(jnp.int32, (SUBLANES, width), 0)

    def group(gi, carry):
        r0 = pl.multiple_of(gi * SUBLANES, SUBLANES)
        av = a_scr[pl.ds(r0, SUBLANES), :]
        uv = u_scr[pl.ds(r0, SUBLANES), :]
        for k in (1, 2, 4):
            keep = srow >= k
            ush = jnp.where(keep, pltpu.roll(uv, k, 0), 0.0)
            ash = jnp.where(keep, pltpu.roll(av, k, 0), 1.0)
            uv = av * ush + uv
            av = av * ash
        hv = uv + av * carry
        u_scr[pl.ds(r0, SUBLANES), :] = hv
        return hv[SUBLANES - 1:SUBLANES, :]

    h_scr[...] = lax.fori_loop(0, tt // SUBLANES, group, h_scr[...], unroll=2)
    hs = u_scr[...]
    out = _dot((hs * y).astype(BF16), wout_ref[...])
    o_ref[0] = x + g_ref[0] * out


def _rg_call(x, nw, sh, sc, g, w_in, conv_w, conv_b, gate_w, gate_b, lam, w_out):
    b, s, d = x.shape
    width = w_out.shape[0]
    blk = width // RG_HEADS
    tt = min(MIX_ROWS, s)
    tok = pl.BlockSpec((1, tt, d), lambda i, t: (i, t, 0))
    vec = pl.BlockSpec((1, 1, d), lambda i, t: (i, 0, 0))
    return pl.pallas_call(
        functools.partial(_rg_kernel, tt=tt, width=width),
        grid=(b, s // tt),
        in_specs=[tok, _resident((1, d)), vec, vec, vec, _resident(w_in.shape), _resident((CONV_WIDTH, width)),
                  _resident((1, width)), _resident(gate_w.shape), _resident((RG_HEADS, 1, 2 * blk)),
                  _resident((1, width)), _resident(w_out.shape)],
        out_specs=tok,
        out_shape=jax.ShapeDtypeStruct(x.shape, F32),
        scratch_shapes=[pltpu.VMEM((SUBLANES, width), F32), pltpu.VMEM((1, width), F32),
                        pltpu.VMEM((tt, width), F32), pltpu.VMEM((tt, width), F32)],
        compiler_params=_params(),
        name="rglru_mixer",
    )(x, nw.reshape(1, d), sh, sc, g, w_in, conv_w, conv_b.reshape(1, width), gate_w,
      gate_b.reshape(RG_HEADS, 1, 2 * blk), lam.reshape(1, width), w_out)


def _cumsum_rows(v, rows):
    n = v.shape[1]
    groups = rows // SUBLANES
    v3 = v.reshape(groups, SUBLANES, n)
    srow = lax.broadcasted_iota(jnp.int32, v3.shape, 1)
    for k in (1, 2, 4):
        v3 = v3 + jnp.where(srow >= k, pltpu.roll(v3, k, 1), 0.0)
    out, carry = [], None
    for gi in range(groups):
        blk = v3[gi] if carry is None else v3[gi] + carry
        out.append(blk)
        carry = blk[SUBLANES - 1:SUBLANES, :]
    return jnp.concatenate(out, axis=0)


def _anchor(b, half, rows):
    n = b.shape[1]
    if 2 * half >= SUBLANES:
        v = b.reshape(rows // (2 * half), 2 * half, n)
        return jnp.broadcast_to(v[:, half - 1:half, :], v.shape).reshape(rows, n)
    v = b.reshape(rows // SUBLANES, SUBLANES, n)
    srow = lax.broadcasted_iota(jnp.int32, v.shape, 1)
    if half == 2:
        lo = jnp.broadcast_to(v[:, 1:2, :], v.shape)
        hi = jnp.broadcast_to(v[:, 5:6, :], v.shape)
        return jnp.where(srow < 4, lo, hi).reshape(rows, n)
    assert half == 1
    return jnp.where((srow & 1) == 1, pltpu.roll(v, 1, 1), v).reshape(rows, n)


def _block_diag(a, b):
    top = jnp.concatenate([a, jnp.zeros_like(b)], axis=1)
    bot = jnp.concatenate([jnp.zeros_like(a), b], axis=1)
    return jnp.concatenate([top, bot], axis=0)


def _gla_masks(rows):
    qi = lax.broadcasted_iota(jnp.int32, (rows, 2 * rows), 0)
    kj = lax.broadcasted_iota(jnp.int32, (rows, 2 * rows), 1) & (rows - 1)
    masks = []
    half, lvl = 1, 0
    while half < rows:
        same = (qi >> (lvl + 1)) == (kj >> (lvl + 1))
        masks.append(same & ((qi & half) != 0) & ((kj & half) == 0))
        half, lvl = half * 2, lvl + 1
    return qi == kj, masks


def _gla_chunk(q, k, v, lf, st_ref, eye, masks, *, heads, dk, dv, rows):
    b = _cumsum_rows(lf * LOG2E, rows)
    b_last = b[rows - 1:rows, :]
    qe = (q * jnp.exp2(b)).astype(BF16)
    kd = (k * jnp.exp2(b_last - b)).astype(BF16)
    eb = jnp.exp2(b_last)
    qb = q.astype(BF16)
    kb = k.astype(BF16)
    vb = v.astype(BF16)
    rown = lax.broadcasted_iota(jnp.int32, b.shape, 0)
    mixed = []
    half = 1
    while half < rows:
        e = jnp.exp2(_neg_abs(b - _anchor(b, half, rows)))
        if half >= SUBLANES:
            qk = jnp.concatenate([(q if (r0 & half) else k)[r0:r0 + half] for r0 in range(0, rows, half)], axis=0)
        else:
            qk = jnp.where((rown & half) != 0, q, k)
        mixed.append((qk * e).astype(BF16))
        half *= 2
    ks = [slice(hd * dk, (hd + 1) * dk) for hd in range(heads)]
    vs = [slice(hd * dv, (hd + 1) * dv) for hd in range(heads)]
    inter = []
    for hd in range(heads):
        st = st_ref[hd]
        inter.append(_dot_nt(qe[:, ks[hd]], st.astype(BF16)))
        st_ref[hd] = st * eb[:, ks[hd]] + _dot_tn(vb[:, vs[hd]], kd[:, ks[hd]])
    scores = []
    for h0 in range(0, heads, 2):
        kc = slice(h0 * dk, (h0 + 2) * dk)
        sc = jnp.where(eye, _dot_nt(qb[:, kc], _block_diag(kb[:, ks[h0]], kb[:, ks[h0 + 1]])), 0.0)
        for m, mk in zip(mixed, masks):
            sc = sc + jnp.where(mk, _dot_nt(m[:, kc], _block_diag(m[:, ks[h0]], m[:, ks[h0 + 1]])), 0.0)
        scores.append(sc.astype(BF16))
    outs = []
    for h0 in range(0, heads, 2):
        o2 = _dot(scores[h0 // 2], _block_diag(vb[:, vs[h0]], vb[:, vs[h0 + 1]]))
        outs.append(o2 + jnp.concatenate([inter[h0], inter[h0 + 1]], axis=1))
    return jnp.concatenate(outs, axis=1)


def _hg_kernel(x_ref, nw_ref, sh_ref, sc_ref, g_ref, win_ref, lbl_ref, onw_ref, wout_ref, o_ref,
               p_scr, o_scr, st_scr, *, tt, d, layer, heads):
    t = pl.program_id(1)

    @pl.when(t == 0)
    def _():
        st_scr[...] = jnp.zeros_like(st_scr)

    x = x_ref[0]
    h = _norm_mod(x, nw_ref[...], sh_ref[0], sc_ref[0]).astype(BF16)
    p_scr[...] = _dot(h, win_ref[...])

    lg = lbl_ref[...]
    rows_l = [lg[i:i + 1, :] for i in range(lg.shape[0])]
    mx = functools.reduce(jnp.maximum, rows_l)
    ex = [jnp.exp(r - mx) for r in rows_l]
    den = functools.reduce(lambda u, w: u + w, ex)
    lb = jnp.zeros_like(den)
    for i in range(1, layer + 1):
        lb = lb + ex[i] / den
    log_lb = jnp.log(lb)
    log_1m = jnp.log1p(-lb)
    one_m = 1.0 - lb

    eye, masks = _gla_masks(GLA_CHUNK)

    def chunk(ci, carry):
        r0 = pl.multiple_of(ci * GLA_CHUNK, GLA_CHUNK)
        rs = pl.ds(r0, GLA_CHUNK)
        q = p_scr[rs, 0:d]
        fz = p_scr[rs, d:2 * d]
        v = p_scr[rs, 2 * d:3 * d]
        e = jnp.exp(_neg_abs(fz))
        u = 1.0 + e
        bv = log_1m + (jnp.minimum(fz, 0.0) - jnp.log(u))
        lf = jnp.maximum(log_lb, bv) + jnp.log(1.0 + jnp.exp(_neg_abs(log_lb - bv)))
        k = one_m * (jnp.where(fz >= 0.0, e, 1.0) / u)
        o_scr[rs, :] = _gla_chunk(q, k, v, lf, st_scr, eye, masks, heads=heads, dk=d // heads, dv=d // heads,
                                  rows=GLA_CHUNK)
        return carry

    lax.fori_loop(0, tt // GLA_CHUNK, chunk, 0, unroll=GLA_UNROLL)

    o = _rmsnorm(o_scr[...], onw_ref[...]) * _sigmoid(p_scr[:, 3 * d:4 * d])
    out = _dot(o.astype(BF16), wout_ref[...])
    o_ref[0] = x + g_ref[0] * out


def _hg_call(x, nw, sh, sc, g, w_in, lb_logits, onw, w_out, layer):
    b, s, d = x.shape
    heads = d // HG_EXPAND
    tt = min(MIX_ROWS, s)
    tok = pl.BlockSpec((1, tt, d), lambda i, t: (i, t, 0))
    vec = pl.BlockSpec((1, 1, d), lambda i, t: (i, 0, 0))
    return pl.pallas_call(
        functools.partial(_hg_kernel, tt=tt, d=d, layer=layer, heads=heads),
        grid=(b, s // tt),
        in_specs=[tok, _resident((1, d)), vec, vec, vec, _resident(w_in.shape), _resident(lb_logits.shape),
                  _resident((1, d)), _resident(w_out.shape)],
        out_specs=tok,
        out_shape=jax.ShapeDtypeStruct(x.shape, F32),
        scratch_shapes=[pltpu.VMEM((tt, 4 * d), F32), pltpu.VMEM((tt, d), F32),
                        pltpu.VMEM((heads, HG_EXPAND, HG_EXPAND), F32)],
        compiler_params=_params(),
        name="hgrn2_mixer",
    )(x, nw.reshape(1, d), sh, sc, g, w_in, lb_logits, onw.reshape(1, d), w_out)


def _gla_kernel(x_ref, nw_ref, sh_ref, sc_ref, g_ref, win_ref, gw2_ref, gb_ref, onw_ref, wout_ref, o_ref,
                p_scr, o_scr, st_scr, *, tt, kd, vd, heads):
    t = pl.program_id(1)

    @pl.when(t == 0)
    def _():
        st_scr[...] = jnp.zeros_like(st_scr)

    x = x_ref[0]
    h = _norm_mod(x, nw_ref[...], sh_ref[0], sc_ref[0]).astype(BF16)
    p_scr[...] = _dot(h, win_ref[...])
    dk = kd // heads
    dv = vd // heads
    scale = dk ** -0.5
    low0 = 2 * kd + 2 * vd

    eye, masks = _gla_masks(GLA_CHUNK)

    def chunk(ci, carry):
        r0 = pl.multiple_of(ci * GLA_CHUNK, GLA_CHUNK)
        rs = pl.ds(r0, GLA_CHUNK)
        q = p_scr[rs, 0:kd] * scale
        k = p_scr[rs, kd:2 * kd]
        v = p_scr[rs, 2 * kd:2 * kd + vd]
        a_low = p_scr[rs, low0:low0 + LANES].astype(BF16)
        lf = _log_sigmoid(_dot(a_low, gw2_ref[...]) + gb_ref[...]) / GLA_LOGIT_NORM
        o_scr[rs, :] = _gla_chunk(q, k, v, lf, st_scr, eye, masks, heads=heads, dk=dk, dv=dv, rows=GLA_CHUNK)
        return carry

    lax.fori_loop(0, tt // GLA_CHUNK, chunk, 0, unroll=GLA_UNROLL)

    gate = p_scr[:, 2 * kd + vd:2 * kd + 2 * vd]
    onw = onw_ref[...]
    parts = []
    for hd in range(heads):
        vs = slice(hd * dv, (hd + 1) * dv)
        gh = gate[:, vs]
        parts.append(_rmsnorm(o_scr[:, vs], onw) * (gh * _sigmoid(gh)))
    out = _dot(jnp.concatenate(parts, axis=1).astype(BF16), wout_ref[...])
    o_ref[0] = x + g_ref[0] * out


def _gla_call(x, nw, sh, sc, g, w_in, gate_w2, gate_b, onw, w_out):
    b, s, d = x.shape
    kd = gate_w2.shape[1]
    vd = w_out.shape[0]
    heads = GLA_HEADS
    dv = vd // heads
    tt = min(MIX_ROWS, s)
    cols = 2 * kd + 2 * vd + LANES
    w_in_p = jnp.zeros((d, cols), BF16).at[:, :w_in.shape[1]].set(w_in)
    gw2_p = jnp.zeros((LANES, kd), BF16).at[:GLA_RANK, :].set(gate_w2)
    tok = pl.BlockSpec((1, tt, d), lambda i, t: (i, t, 0))
    vec = pl.BlockSpec((1, 1, d), lambda i, t: (i, 0, 0))
    return pl.pallas_call(
        functools.partial(_gla_kernel, tt=tt, kd=kd, vd=vd, heads=heads),
        grid=(b, s // tt),
        in_specs=[tok, _resident((1, d)), vec, vec, vec, _resident(w_in_p.shape), _resident(gw2_p.shape),
                  _resident((1, kd)), _resident((1, dv)), _resident(w_out.shape)],
        out_specs=tok,
        out_shape=jax.ShapeDtypeStruct(x.shape, F32),
        scratch_shapes=[pltpu.VMEM((tt, cols), F32), pltpu.VMEM((tt, vd), F32),
                        pltpu.VMEM((heads, dv, kd // heads), F32)],
        compiler_params=_params(),
        name="gla_mixer",
    )(x, nw.reshape(1, d), sh, sc, g, w_in_p, gw2_p, gate_b.reshape(1, kd), onw.reshape(1, dv), w_out)


def kernel(x, c, ada_w, ada_b, norm_w, final_norm_w, ffn_w13, ffn_w2, rg_w_in, rg_conv_w, rg_conv_b, rg_gate_w,
           rg_gate_b, rg_lambda, rg_w_out, hg_w_in, hg_lb_logits, hg_norm_w, hg_w_out, gla_w_in, gla_gate_w2,
           gla_gate_b, gla_norm_w, gla_w_out):
    depth = ada_w.shape[0]
    b, s, d = x.shape
    assert s % min(FFN_ROWS, s) == 0 and s % min(MIX_ROWS, s) == 0 and min(MIX_ROWS, s) % GLA_CHUNK == 0
    mods = _mod_call(c, ada_w, ada_b).reshape(depth * N_MOD, b, 1, d)
    bf = lambda w: w.astype(BF16)
    i_rg = i_hg = i_gla = 0
    for l in range(depth):
        sh1, sc1, g1, sh2, sc2, g2, sh3, sc3, g3 = [mods[l * N_MOD + j] for j in range(N_MOD)]
        x = _ffn_call(x, norm_w[l, 0], sh1, sc1, g1, bf(ffn_w13[l, 0]), bf(ffn_w2[l, 0]))
        m = l % N_MIXERS
        if m == 0:
            x = _rg_call(x, norm_w[l, 1], sh2, sc2, g2, bf(rg_w_in[i_rg]), rg_conv_w[i_rg], rg_conv_b[i_rg],
                         bf(rg_gate_w[i_rg]), rg_gate_b[i_rg], rg_lambda[i_rg], bf(rg_w_out[i_rg]))
            i_rg += 1
        elif m == 1:
            x = _hg_call(x, norm_w[l, 1], sh2, sc2, g2, bf(hg_w_in[i_hg]), hg_lb_logits, hg_norm_w[i_hg],
                         bf(hg_w_out[i_hg]), l)
            i_hg += 1
        else:
            x = _gla_call(x, norm_w[l, 1], sh2, sc2, g2, bf(gla_w_in[i_gla]), bf(gla_gate_w2[i_gla]),
                          gla_gate_b[i_gla], gla_norm_w[i_gla], bf(gla_w_out[i_gla]))
            i_gla += 1
        x = _ffn_call(x, norm_w[l, 2], sh3, sc3, g3, bf(ffn_w13[l, 1]), bf(ffn_w2[l, 1]),
                      final_w=final_norm_w if l == depth - 1 else None)
    return x
```

```python
import functools

import jax
import jax.numpy as jnp
from jax import lax
from jax.experimental import pallas as pl
from jax.experimental.pallas import tpu as pltpu

F32 = jnp.float32
BF16 = jnp.bfloat16

N_MIXERS = 3
N_MOD = 9
EPS = 1e-6
RG_HEADS = 8
CONV_WIDTH = 4
RG_C = 8.0
HG_EXPAND = 128
GLA_HEADS = 4
GLA_RANK = 16
GLA_LOGIT_NORM = 16.0
LOG2E = 1.4426950408889634

LANES = 128
SUBLANES = 8
VMEM_LIMIT_BYTES = 56 * 1024 * 1024

FFN_ROWS = 512
MIX_ROWS = 512
RG_STEPS = 128
RG_SUB = 32
GLA_CHUNK = 64
GLA_UNROLL = 4


def _params():
    return pltpu.CompilerParams(dimension_semantics=("arbitrary", "arbitrary"),
                                vmem_limit_bytes=VMEM_LIMIT_BYTES)


def _resident(shape):
    zeros = (0,) * len(shape)
    return pl.BlockSpec(shape, lambda *_: zeros, pipeline_mode=pl.Buffered(1))


def _dot(a, b):
    return jnp.dot(a, b, preferred_element_type=F32)


def _dot_nt(a, b):
    return lax.dot_general(a, b, (((1,), (1,)), ((), ())), preferred_element_type=F32)


def _dot_tn(a, b):
    return lax.dot_general(a, b, (((0,), (0,)), ((), ())), preferred_element_type=F32)


def _neg_abs(x):
    bits = lax.bitcast_convert_type(x, jnp.uint32) | jnp.uint32(0x80000000)
    return lax.bitcast_convert_type(bits, F32)


def _sigmoid(x):
    return 1.0 / (1.0 + jnp.exp(-x))


def _softplus(x):
    return jnp.maximum(x, 0.0) + jnp.log1p(jnp.exp(-jnp.abs(x)))


def _log_sigmoid(x):
    return jnp.minimum(x, 0.0) - jnp.log(1.0 + jnp.exp(_neg_abs(x)))


def _rmsnorm(x, w):
    ms = jnp.mean(x * x, axis=-1, keepdims=True)
    return (x * lax.rsqrt(ms + EPS)) * w


def _norm_mod(x, nw, shift, scale):
    return _rmsnorm(x, nw) * (1.0 + scale) + shift


def _mod_kernel(c_ref, w_ref, b_ref, o_ref):
    c = c_ref[...]
    ca = c * _sigmoid(c)
    w = w_ref[0]
    c_hi = ca.astype(BF16)
    c_lo = (ca - c_hi.astype(F32)).astype(BF16)
    w_hi = w.astype(BF16)
    w_lo = (w - w_hi.astype(F32)).astype(BF16)
    acc = _dot(c_hi, w_hi) + (_dot(c_hi, w_lo) + _dot(c_lo, w_hi))
    o_ref[0] = acc + b_ref[0]


def _mod_call(c, ada_w, ada_b):
    depth, d, _ = ada_w.shape
    b = c.shape[0]
    return pl.pallas_call(
        _mod_kernel,
        grid=(depth, N_MOD),
        in_specs=[
            pl.BlockSpec((b, d), lambda l, j: (0, 0)),
            pl.BlockSpec((1, d, d), lambda l, j: (l, 0, j)),
            pl.BlockSpec((1, 1, d), lambda l, j: (l, 0, j)),
        ],
        out_specs=pl.BlockSpec((1, b, d), lambda l, j: (l * N_MOD + j, 0, 0)),
        out_shape=jax.ShapeDtypeStruct((depth * N_MOD, b, d), F32),
        compiler_params=_params(),
        name="adaln_table",
    )(c, ada_w, ada_b.reshape(depth, 1, N_MOD * d))


def _ffn_kernel(x_ref, nw_ref, sh_ref, sc_ref, g_ref, w13_ref, w2_ref, *rest, d_ff, final):
    if final:
        fw_ref, o_ref = rest
    else:
        (o_ref,) = rest
    x = x_ref[0]
    h = _norm_mod(x, nw_ref[...], sh_ref[0], sc_ref[0]).astype(BF16)
    hh = _dot(h, w13_ref[...])
    gate = hh[:, :d_ff]
    up = hh[:, d_ff:]
    act = (gate * _sigmoid(gate) * up).astype(BF16)
    y = _dot(act, w2_ref[...])
    out = x + (0.5 * g_ref[0]) * y
    if final:
        out = _rmsnorm(out, fw_ref[...])
    o_ref[0] = out


def _ffn_call(x, nw, sh, sc, g, w13, w2, final_w=None):
    b, s, d = x.shape
    d_ff = w2.shape[0]
    tm = min(FFN_ROWS, s)
    tok = pl.BlockSpec((1, tm, d), lambda i, t: (i, t, 0))
    vec = pl.BlockSpec((1, 1, d), lambda i, t: (i, 0, 0))
    in_specs = [tok, _resident((1, d)), vec, vec, vec, _resident(w13.shape), _resident(w2.shape)]
    args = [x, nw.reshape(1, d), sh, sc, g, w13, w2]
    if final_w is not None:
        in_specs.append(_resident((1, d)))
        args.append(final_w.reshape(1, d))
    return pl.pallas_call(
        functools.partial(_ffn_kernel, d_ff=d_ff, final=final_w is not None),
        grid=(b, s // tm),
        in_specs=in_specs,
        out_specs=tok,
        out_shape=jax.ShapeDtypeStruct(x.shape, F32),
        compiler_params=_params(),
        name="swiglu_final" if final_w is not None else "swiglu",
    )(*args)


def _gelu_tanh(x):
    return x * (0.5 * (1.0 + jnp.tanh(0.7978845608028654 * (x + 0.044715 * (x * x * x)))))


def _rg_kernel(x_ref, nw_ref, sh_ref, sc_ref, g_ref, perm_ref, permt_ref, win_ref, cw_ref, cb_ref, gw_ref, gb_ref,
               lam_ref, wout_ref, o_ref, conv_scr, h_scr, *, nb, ts, sub, width):
    t = pl.program_id(1)
    rows = nb * sub
    d = x_ref.shape[2]
    blk = width // RG_HEADS
    tail_rows = (CONV_WIDTH - 1) * nb

    @pl.when(t == 0)
    def _():
        conv_scr[...] = jnp.zeros_like(conv_scr)
        h_scr[...] = jnp.zeros_like(h_scr)

    nw, sh, sc, g = nw_ref[...], sh_ref[...], sc_ref[...], g_ref[...]
    cw, cb = cw_ref[...], cb_ref[...]
    decay = -RG_C * _softplus(-lam_ref[...])
    first_rows = lax.broadcasted_iota(jnp.int32, (rows, 1), 0) < nb
    tail = conv_scr[...]
    hcar = h_scr[...]
    for si in range(ts // sub):
        x = x_ref[:, si * sub:(si + 1) * sub, :]
        h = _norm_mod(x, nw, sh, sc).reshape(rows, d).astype(BF16)
        hp = _dot(perm_ref[...], h).astype(BF16)
        yx = _dot(hp, win_ref[...])
        y = _gelu_tanh(yx[:, :width])
        xb = yx[:, width:]

        ext = jnp.concatenate([tail, xb], axis=0)
        xc = cb
        for j in range(CONV_WIDTH):
            xc = xc + ext[j * nb:j * nb + rows] * cw[j:j + 1]
        tail = xb[rows - tail_rows:rows]

        xcb = xc.astype(BF16)
        gx, ga = [], []
        for hd in range(RG_HEADS):
            gts = _dot(xcb[:, hd * blk:(hd + 1) * blk], gw_ref[hd]) + gb_ref[hd]
            gx.append(gts[:, :blk])
            ga.append(gts[:, blk:])
        gate_x = _sigmoid(jnp.concatenate(gx, axis=1))
        gate_a = _sigmoid(jnp.concatenate(ga, axis=1))

        log_a = gate_a * decay
        a = jnp.exp(log_a)
        m2 = -jnp.tanh(log_a) * (a * a + 1.0)
        mult = jnp.where(m2 > 0.0, m2 * lax.rsqrt(m2), 0.0)
        if si == 0:
            mult = jnp.where(jnp.logical_and(first_rows, t == 0), 1.0, mult)
        u = gate_x * xc * mult

        hs = []
        for ti in range(sub):
            hcar = a[ti * nb:(ti + 1) * nb] * hcar + u[ti * nb:(ti + 1) * nb]
            hs.append(hcar)
        z = (jnp.concatenate(hs, axis=0) * y).astype(BF16)
        zn = _dot(permt_ref[...], z).astype(BF16)
        out = _dot(zn, wout_ref[...]).reshape(nb, sub, d)
        o_ref[:, si * sub:(si + 1) * sub, :] = x + g * out
    conv_scr[...] = tail
    h_scr[...] = hcar


def _rg_call(x, nw, sh, sc, g, w_in, conv_w, conv_b, gate_w, gate_b, lam, w_out):
    b, s, d = x.shape
    width = w_out.shape[0]
    blk = width // RG_HEADS
    nb = SUBLANES
    ts = min(RG_STEPS, s)
    sub = min(RG_SUB, ts)
    rows = nb * sub
    assert b % nb == 0 and s % ts == 0 and ts % sub == 0 and sub % SUBLANES == 0
    r = jnp.arange(rows, dtype=jnp.int32)
    perm = ((r[:, None] % nb) * sub + r[:, None] // nb == r[None, :]).astype(BF16)
    tok = pl.BlockSpec((nb, ts, d), lambda i, t: (i, t, 0))
    vec = pl.BlockSpec((nb, 1, d), lambda i, t: (i, 0, 0))
    return pl.pallas_call(
        functools.partial(_rg_kernel, nb=nb, ts=ts, sub=sub, width=width),
        grid=(b // nb, s // ts),
        in_specs=[tok, _resident((1, d)), vec, vec, vec, _resident((rows, rows)), _resident((rows, rows)),
                  _resident(w_in.shape), _resident((CONV_WIDTH, width)), _resident((1, width)),
                  _resident(gate_w.shape), _resident((RG_HEADS, 1, 2 * blk)), _resident((1, width)),
                  _resident(w_out.shape)],
        out_specs=tok,
        out_shape=jax.ShapeDtypeStruct(x.shape, F32),
        scratch_shapes=[pltpu.VMEM(((CONV_WIDTH - 1) * nb, width), F32), pltpu.VMEM((nb, width), F32)],
        compiler_params=_params(),
        name="rglru_mixer",
    )(x, nw.reshape(1, d), sh, sc, g, perm, perm.T, w_in, conv_w, conv_b.reshape(1, width), gate_w,
      gate_b.reshape(RG_HEADS, 1, 2 * blk), lam.reshape(1, width), w_out)


def _cumsum_rows(v, rows):
    n = v.shape[1]
    groups = rows // SUBLANES
    v3 = v.reshape(groups, SUBLANES, n)
    srow = lax.broadcasted_iota(jnp.int32, v3.shape, 1)
    for k in (1, 2, 4):
        v3 = v3 + jnp.where(srow >= k, pltpu.roll(v3, k, 1), 0.0)
    out, carry = [], None
    for gi in range(groups):
        blk = v3[gi] if carry is None else v3[gi] + carry
        out.append(blk)
        carry = blk[SUBLANES - 1:SUBLANES, :]
    return jnp.concatenate(out, axis=0)


def _anchor(b, half, rows):
    n = b.shape[1]
    if 2 * half >= SUBLANES:
        v = b.reshape(rows // (2 * half), 2 * half, n)
        return jnp.broadcast_to(v[:, half - 1:half, :], v.shape).reshape(rows, n)
    v = b.reshape(rows // SUBLANES, SUBLANES, n)
    srow = lax.broadcasted_iota(jnp.int32, v.shape, 1)
    if half == 2:
        lo = jnp.broadcast_to(v[:, 1:2, :], v.shape)
        hi = jnp.broadcast_to(v[:, 5:6, :], v.shape)
        return jnp.where(srow < 4, lo, hi).reshape(rows, n)
    assert half == 1
    return jnp.where((srow & 1) == 1, pltpu.roll(v, 1, 1), v).reshape(rows, n)


def _block_diag(a, b):
    top = jnp.concatenate([a, jnp.zeros_like(b)], axis=1)
    bot = jnp.concatenate([jnp.zeros_like(a), b], axis=1)
    return jnp.concatenate([top, bot], axis=0)


def _gla_masks(rows):
    qi = lax.broadcasted_iota(jnp.int32, (rows, 2 * rows), 0)
    kj = lax.broadcasted_iota(jnp.int32, (rows, 2 * rows), 1) & (rows - 1)
    masks = []
    half, lvl = 1, 0
    while half < rows:
        same = (qi >> (lvl + 1)) == (kj >> (lvl + 1))
        masks.append(same & ((qi & half) != 0) & ((kj & half) == 0))
        half, lvl = half * 2, lvl + 1
    return qi == kj, masks


def _gla_chunk(q, k, v, lf, st_ref, eye, masks, *, heads, dk, dv, rows):
    b = _cumsum_rows(lf * LOG2E, rows)
    b_last = b[rows - 1:rows, :]
    qe = (q * jnp.exp2(b)).astype(BF16)
    kd = (k * jnp.exp2(b_last - b)).astype(BF16)
    eb = jnp.exp2(b_last)
    qb = q.astype(BF16)
    kb = k.astype(BF16)
    vb = v.astype(BF16)
    rown = lax.broadcasted_iota(jnp.int32, b.shape, 0)
    mixed = []
    half = 1
    while half < rows:
        e = jnp.exp2(_neg_abs(b - _anchor(b, half, rows)))
        if half >= SUBLANES:
            qk = jnp.concatenate([(q if (r0 & half) else k)[r0:r0 + half] for r0 in range(0, rows, half)], axis=0)
        else:
            qk = jnp.where((rown & half) != 0, q, k)
        mixed.append((qk * e).astype(BF16))
        half *= 2
    ks = [slice(hd * dk, (hd + 1) * dk) for hd in range(heads)]
    vs = [slice(hd * dv, (hd + 1) * dv) for hd in range(heads)]
    inter = []
    for hd in range(heads):
        st = st_ref[hd]
        inter.append(_dot_nt(qe[:, ks[hd]], st.astype(BF16)))
        st_ref[hd] = st * eb[:, ks[hd]] + _dot_tn(vb[:, vs[hd]], kd[:, ks[hd]])
    scores = []
    for h0 in range(0, heads, 2):
        kc = slice(h0 * dk, (h0 + 2) * dk)
        sc = jnp.where(eye, _dot_nt(qb[:, kc], _block_diag(kb[:, ks[h0]], kb[:, ks[h0 + 1]])), 0.0)
        for m, mk in zip(mixed, masks):
            sc = sc + jnp.where(mk, _dot_nt(m[:, kc], _block_diag(m[:, ks[h0]], m[:, ks[h0 + 1]])), 0.0)
        scores.append(sc.astype(BF16))
    outs = []
    for h0 in range(0, heads, 2):
        o2 = _dot(scores[h0 // 2], _block_diag(vb[:, vs[h0]], vb[:, vs[h0 + 1]]))
        outs.append(o2 + jnp.concatenate([inter[h0], inter[h0 + 1]], axis=1))
    return jnp.concatenate(outs, axis=1)


def _hg_kernel(x_ref, nw_ref, sh_ref, sc_ref, g_ref, win_ref, lbl_ref, onw_ref, wout_ref, o_ref,
               p_scr, o_scr, st_scr, *, tt, d, layer, heads):
    t = pl.program_id(1)

    @pl.when(t == 0)
    def _():
        st_scr[...] = jnp.zeros_like(st_scr)

    x = x_ref[0]
    h = _norm_mod(x, nw_ref[...], sh_ref[0], sc_ref[0]).astype(BF16)
    p_scr[...] = _dot(h, win_ref[...])

    lg = lbl_ref[...]
    rows_l = [lg[i:i + 1, :] for i in range(lg.shape[0])]
    mx = functools.reduce(jnp.maximum, rows_l)
    ex = [jnp.exp(r - mx) for r in rows_l]
    den = functools.reduce(lambda u, w: u + w, ex)
    lb = jnp.zeros_like(den)
    for i in range(1, layer + 1):
        lb = lb + ex[i] / den
    log_lb = jnp.log(lb)
    log_1m = jnp.log1p(-lb)
    one_m = 1.0 - lb

    eye, masks = _gla_masks(GLA_CHUNK)

    def chunk(ci, carry):
        r0 = pl.multiple_of(ci * GLA_CHUNK, GLA_CHUNK)
        rs = pl.ds(r0, GLA_CHUNK)
        q = p_scr[rs, 0:d]
        fz = p_scr[rs, d:2 * d]
        v = p_scr[rs, 2 * d:3 * d]
        e = jnp.exp(_neg_abs(fz))
        u = 1.0 + e
        bv = log_1m + (jnp.minimum(fz, 0.0) - jnp.log(u))
        lf = jnp.maximum(log_lb, bv) + jnp.log(1.0 + jnp.exp(_neg_abs(log_lb - bv)))
        k = one_m * (jnp.where(fz >= 0.0, e, 1.0) / u)
        o_scr[rs, :] = _gla_chunk(q, k, v, lf, st_scr, eye, masks, heads=heads, dk=d // heads, dv=d // heads,
                                  rows=GLA_CHUNK)
        return carry

    lax.fori_loop(0, tt // GLA_CHUNK, chunk, 0, unroll=GLA_UNROLL)

    o = _rmsnorm(o_scr[...], onw_ref[...]) * _sigmoid(p_scr[:, 3 * d:4 * d])
    out = _dot(o.astype(BF16), wout_ref[...])
    o_ref[0] = x + g_ref[0] * out


def _hg_call(x, nw, sh, sc, g, w_in, lb_logits, onw, w_out, layer):
    b, s, d = x.shape
    heads = d // HG_EXPAND
    tt = min(MIX_ROWS, s)
    tok = pl.BlockSpec((1, tt, d), lambda i, t: (i, t, 0))
    vec = pl.BlockSpec((1, 1, d), lambda i, t: (i, 0, 0))
    return pl.pallas_call(
        functools.partial(_hg_kernel, tt=tt, d=d, layer=layer, heads=heads),
        grid=(b, s // tt),
        in_specs=[tok, _resident((1, d)), vec, vec, vec, _resident(w_in.shape), _resident(lb_logits.shape),
                  _resident((1, d)), _resident(w_out.shape)],
        out_specs=tok,
        out_shape=jax.ShapeDtypeStruct(x.shape, F32),
        scratch_shapes=[pltpu.VMEM((tt, 4 * d), F32), pltpu.VMEM((tt, d), F32),
                        pltpu.VMEM((heads, HG_EXPAND, HG_EXPAND), F32)],
        compiler_params=_params(),
        name="hgrn2_mixer",
    )(x, nw.reshape(1, d), sh, sc, g, w_in, lb_logits, onw.reshape(1, d), w_out)


def _gla_kernel(x_ref, nw_ref, sh_ref, sc_ref, g_ref, win_ref, gw2_ref, gb_ref, onw_ref, wout_ref, o_ref,
                p_scr, o_scr, st_scr, *, tt, kd, vd, heads):
    t = pl.program_id(1)

    @pl.when(t == 0)
    def _():
        st_scr[...] = jnp.zeros_like(st_scr)

    x = x_ref[0]
    h = _norm_mod(x, nw_ref[...], sh_ref[0], sc_ref[0]).astype(BF16)
    p_scr[...] = _dot(h, win_ref[...])
    dk = kd // heads
    dv = vd // heads
    scale = dk ** -0.5
    low0 = 2 * kd + 2 * vd

    eye, masks = _gla_masks(GLA_CHUNK)

    def chunk(ci, carry):
        r0 = pl.multiple_of(ci * GLA_CHUNK, GLA_CHUNK)
        rs = pl.ds(r0, GLA_CHUNK)
        q = p_scr[rs, 0:kd] * scale
        k = p_scr[rs, kd:2 * kd]
        v = p_scr[rs, 2 * kd:2 * kd + vd]
        a_low = p_scr[rs, low0:low0 + LANES].astype(BF16)
        lf = _log_sigmoid(_dot(a_low, gw2_ref[...]) + gb_ref[...]) / GLA_LOGIT_NORM
        o_scr[rs, :] = _gla_chunk(q, k, v, lf, st_scr, eye, masks, heads=heads, dk=dk, dv=dv, rows=GLA_CHUNK)
        return carry

    lax.fori_loop(0, tt // GLA_CHUNK, chunk, 0, unroll=GLA_UNROLL)

    gate = p_scr[:, 2 * kd + vd:2 * kd + 2 * vd]
    onw = onw_ref[...]
    parts = []
    for hd in range(heads):
        vs = slice(hd * dv, (hd + 1) * dv)
        gh = gate[:, vs]
        parts.append(_rmsnorm(o_scr[:, vs], onw) * (gh * _sigmoid(gh)))
    out = _dot(jnp.concatenate(parts, axis=1).astype(BF16), wout_ref[...])
    o_ref[0] = x + g_ref[0] * out


def _gla_call(x, nw, sh, sc, g, w_in, gate_w2, gate_b, onw, w_out):
    b, s, d = x.shape
    kd = gate_w2.shape[1]
    vd = w_out.shape[0]
    heads = GLA_HEADS
    dv = vd // heads
    tt = min(MIX_ROWS, s)
    cols = 2 * kd + 2 * vd + LANES
    w_in_p = jnp.zeros((d, cols), BF16).at[:, :w_in.shape[1]].set(w_in)
    gw2_p = jnp.zeros((LANES, kd), BF16).at[:GLA_RANK, :].set(gate_w2)
    tok = pl.BlockSpec((1, tt, d), lambda i, t: (i, t, 0))
    vec = pl.BlockSpec((1, 1, d), lambda i, t: (i, 0, 0))
    return pl.pallas_call(
        functools.partial(_gla_kernel, tt=tt, kd=kd, vd=vd, heads=heads),
        grid=(b, s // tt),
        in_specs=[tok, _resident((1, d)), vec, vec, vec, _resident(w_in_p.shape), _resident(gw2_p.shape),
                  _resident((1, kd)), _resident((1, dv)), _resident(w_out.shape)],
        out_specs=tok,
        out_shape=jax.ShapeDtypeStruct(x.shape, F32),
        scratch_shapes=[pltpu.VMEM((tt, cols), F32), pltpu.VMEM((tt, vd), F32),
                        pltpu.VMEM((heads, dv, kd // heads), F32)],
        compiler_params=_params(),
        name="gla_mixer",
    )(x, nw.reshape(1, d), sh, sc, g, w_in_p, gw2_p, gate_b.reshape(1, kd), onw.reshape(1, dv), w_out)


def kernel(x, c, ada_w, ada_b, norm_w, final_norm_w, ffn_w13, ffn_w2, rg_w_in, rg_conv_w, rg_conv_b, rg_gate_w,
           rg_gate_b, rg_lambda, rg_w_out, hg_w_in, hg_lb_logits, hg_norm_w, hg_w_out, gla_w_in, gla_gate_w2,
           gla_gate_b, gla_norm_w, gla_w_out):
    depth = ada_w.shape[0]
    b, s, d = x.shape
    assert s % min(FFN_ROWS, s) == 0 and s % min(MIX_ROWS, s) == 0 and min(MIX_ROWS, s) % GLA_CHUNK == 0
    mods = _mod_call(c, ada_w, ada_b).reshape(depth * N_MOD, b, 1, d)
    bf = lambda w: w.astype(BF16)
    i_rg = i_hg = i_gla = 0
    for l in range(depth):
        sh1, sc1, g1, sh2, sc2, g2, sh3, sc3, g3 = [mods[l * N_MOD + j] for j in range(N_MOD)]
        x = _ffn_call(x, norm_w[l, 0], sh1, sc1, g1, bf(ffn_w13[l, 0]), bf(ffn_w2[l, 0]))
        m = l % N_MIXERS
        if m == 0:
            x = _rg_call(x, norm_w[l, 1], sh2, sc2, g2, bf(rg_w_in[i_rg]), rg_conv_w[i_rg], rg_conv_b[i_rg],
                         bf(rg_gate_w[i_rg]), rg_gate_b[i_rg], rg_lambda[i_rg], bf(rg_w_out[i_rg]))
            i_rg += 1
        elif m == 1:
            x = _hg_call(x, norm_w[l, 1], sh2, sc2, g2, bf(hg_w_in[i_hg]), hg_lb_logits, hg_norm_w[i_hg],
                         bf(hg_w_out[i_hg]), l)
            i_hg += 1
        else:
            x = _gla_call(x, norm_w[l, 1], sh2, sc2, g2, bf(gla_w_in[i_gla]), bf(gla_gate_w2[i_gla]),
                          gla_gate_b[i_gla], gla_norm_w[i_gla], bf(gla_w_out[i_gla]))
            i_gla += 1
        x = _ffn_call(x, norm_w[l, 2], sh3, sc3, g3, bf(ffn_w13[l, 1]), bf(ffn_w2[l, 1]),
                      final_w=final_norm_w if l == depth - 1 else None)
    return x
```

```python
import functools

import jax
import jax.numpy as jnp
from jax import lax
from jax.experimental import pallas as pl
from jax.experimental.pallas import tpu as pltpu

F32 = jnp.float32
BF16 = jnp.bfloat16

N_MIXERS = 3
N_MOD = 9
EPS = 1e-6
RG_HEADS = 8
CONV_WIDTH = 4
RG_C = 8.0
HG_EXPAND = 128
GLA_HEADS = 4
GLA_RANK = 16
GLA_LOGIT_NORM = 16.0
LOG2E = 1.4426950408889634

LANES = 128
SUBLANES = 8
MXU_COLS = 256
VMEM_LIMIT_BYTES = 56 * 1024 * 1024

FFN_ROWS = 1024
FFN_SUB = 512
FFN_COL_CHUNKS = 2
MIX_ROWS = 512
RG_STEPS = 128
RG_SUB = 32
GLA_CHUNK = 64
GLA_UNROLL = 4


def _params():
    return pltpu.CompilerParams(dimension_semantics=("arbitrary", "arbitrary"),
                                vmem_limit_bytes=VMEM_LIMIT_BYTES)


def _resident(shape):
    zeros = (0,) * len(shape)
    return pl.BlockSpec(shape, lambda *_: zeros, pipeline_mode=pl.Buffered(1))


def _dot(a, b):
    return jnp.dot(a, b, preferred_element_type=F32)


def _dot_nt(a, b):
    return lax.dot_general(a, b, (((1,), (1,)), ((), ())), preferred_element_type=F32)


def _dot_tn(a, b):
    return lax.dot_general(a, b, (((0,), (0,)), ((), ())), preferred_element_type=F32)


def _neg_abs(x):
    bits = lax.bitcast_convert_type(x, jnp.uint32) | jnp.uint32(0x80000000)
    return lax.bitcast_convert_type(bits, F32)


def _sigmoid(x):
    return 1.0 / (1.0 + jnp.exp(-x))


def _softplus(x):
    return jnp.maximum(x, 0.0) + jnp.log1p(jnp.exp(-jnp.abs(x)))


def _log_sigmoid(x):
    return jnp.minimum(x, 0.0) - jnp.log(1.0 + jnp.exp(_neg_abs(x)))


def _rmsnorm(x, w):
    ms = jnp.mean(x * x, axis=-1, keepdims=True)
    return (x * lax.rsqrt(ms + EPS)) * w


def _norm_mod(x, nw, shift, scale):
    return _rmsnorm(x, nw) * (1.0 + scale) + shift


def _mod_kernel(c_ref, w_ref, b_ref, o_ref):
    c = c_ref[...]
    ca = c * _sigmoid(c)
    w = w_ref[0]
    c_hi = ca.astype(BF16)
    c_lo = (ca - c_hi.astype(F32)).astype(BF16)
    w_hi = w.astype(BF16)
    w_lo = (w - w_hi.astype(F32)).astype(BF16)
    acc = _dot(c_hi, w_hi) + (_dot(c_hi, w_lo) + _dot(c_lo, w_hi))
    o_ref[0] = acc + b_ref[0]


def _mod_call(c, ada_w, ada_b):
    depth, d, _ = ada_w.shape
    b = c.shape[0]
    return pl.pallas_call(
        _mod_kernel,
        grid=(depth, N_MOD),
        in_specs=[
            pl.BlockSpec((b, d), lambda l, j: (0, 0)),
            pl.BlockSpec((1, d, d), lambda l, j: (l, 0, j)),
            pl.BlockSpec((1, 1, d), lambda l, j: (l, 0, j)),
        ],
        out_specs=pl.BlockSpec((1, b, d), lambda l, j: (l * N_MOD + j, 0, 0)),
        out_shape=jax.ShapeDtypeStruct((depth * N_MOD, b, d), F32),
        compiler_params=_params(),
        name="adaln_table",
    )(c, ada_w, ada_b.reshape(depth, 1, N_MOD * d))


def _ffn_kernel(x_ref, nw_ref, sh_ref, sc_ref, g_ref, w13_ref, w2_ref, *rest, d_ff, tm, sub, final):
    if final:
        fw_ref, o_ref = rest
    else:
        (o_ref,) = rest
    nw, sh, sc = nw_ref[...], sh_ref[0], sc_ref[0]
    half_g = 0.5 * g_ref[0]
    tiles = -(-d_ff // MXU_COLS)
    per = -(-tiles // FFN_COL_CHUNKS)
    bounds = [min(c * per * MXU_COLS, d_ff) for c in range(FFN_COL_CHUNKS + 1)]
    for r0 in range(0, tm, sub):
        x = x_ref[0, r0:r0 + sub, :]
        h = _norm_mod(x, nw, sh, sc).astype(BF16)
        y = None
        for c0, c1 in zip(bounds[:-1], bounds[1:]):
            gate = _dot(h, w13_ref[:, c0:c1])
            up = _dot(h, w13_ref[:, d_ff + c0:d_ff + c1])
            act = (gate * _sigmoid(gate) * up).astype(BF16)
            part = _dot(act, w2_ref[c0:c1, :])
            y = part if y is None else y + part
        out = x + half_g * y
        if final:
            out = _rmsnorm(out, fw_ref[...])
        o_ref[0, r0:r0 + sub, :] = out


def _ffn_call(x, nw, sh, sc, g, w13, w2, layer, which, final_w=None):
    b, s, d = x.shape
    d_ff = w2.shape[2]
    tm = min(FFN_ROWS, s)
    sub = min(FFN_SUB, tm)
    assert tm % sub == 0 and d_ff % LANES == 0
    tok = pl.BlockSpec((1, tm, d), lambda i, t: (i, t, 0))
    vec = pl.BlockSpec((1, 1, d), lambda i, t: (i, 0, 0))
    pick = lambda i, t: (layer, which, 0, 0)
    w13_spec = pl.BlockSpec((None, None, d, 2 * d_ff), pick, pipeline_mode=pl.Buffered(1))
    w2_spec = pl.BlockSpec((None, None, d_ff, d), pick, pipeline_mode=pl.Buffered(1))
    in_specs = [tok, _resident((1, d)), vec, vec, vec, w13_spec, w2_spec]
    args = [x, nw.reshape(1, d), sh, sc, g, w13, w2]
    if final_w is not None:
        in_specs.append(_resident((1, d)))
        args.append(final_w.reshape(1, d))
    return pl.pallas_call(
        functools.partial(_ffn_kernel, d_ff=d_ff, tm=tm, sub=sub, final=final_w is not None),
        grid=(b, s // tm),
        in_specs=in_specs,
        out_specs=tok,
        out_shape=jax.ShapeDtypeStruct(x.shape, F32),
        compiler_params=_params(),
        name="swiglu_final" if final_w is not None else "swiglu",
    )(*args)


def _gelu_tanh(x):
    return x * (0.5 * (1.0 + jnp.tanh(0.7978845608028654 * (x + 0.044715 * (x * x * x)))))


def _rg_kernel(x_ref, nw_ref, sh_ref, sc_ref, g_ref, perm_ref, permt_ref, win_ref, cw_ref, cb_ref, gw_ref, gb_ref,
               lam_ref, wout_ref, o_ref, conv_scr, h_scr, *, nb, ts, sub, width):
    t = pl.program_id(1)
    rows = nb * sub
    d = x_ref.shape[2]
    blk = width // RG_HEADS
    tail_rows = (CONV_WIDTH - 1) * nb

    @pl.when(t == 0)
    def _():
        conv_scr[...] = jnp.zeros_like(conv_scr)
        h_scr[...] = jnp.zeros_like(h_scr)

    nw, sh, sc, g = nw_ref[...], sh_ref[...], sc_ref[...], g_ref[...]
    cw, cb = cw_ref[...], cb_ref[...]
    decay = -RG_C * _softplus(-lam_ref[...])
    first_rows = lax.broadcasted_iota(jnp.int32, (rows, 1), 0) < nb
    tail = conv_scr[...]
    hcar = h_scr[...]
    for si in range(ts // sub):
        x = x_ref[:, si * sub:(si + 1) * sub, :]
        h = _norm_mod(x, nw, sh, sc).reshape(rows, d).astype(BF16)
        hp = _dot(perm_ref[...], h).astype(BF16)
        yx = _dot(hp, win_ref[...])
        y = _gelu_tanh(yx[:, :width])
        xb = yx[:, width:]

        ext = jnp.concatenate([tail, xb], axis=0)
        xc = cb
        for j in range(CONV_WIDTH):
            xc = xc + ext[j * nb:j * nb + rows] * cw[j:j + 1]
        tail = xb[rows - tail_rows:rows]

        xcb = xc.astype(BF16)
        gx, ga = [], []
        for hd in range(RG_HEADS):
            gts = _dot(xcb[:, hd * blk:(hd + 1) * blk], gw_ref[hd]) + gb_ref[hd]
            gx.append(gts[:, :blk])
            ga.append(gts[:, blk:])
        gate_x = _sigmoid(jnp.concatenate(gx, axis=1))
        gate_a = _sigmoid(jnp.concatenate(ga, axis=1))

        log_a = gate_a * decay
        a = jnp.exp(log_a)
        m2 = -jnp.tanh(log_a) * (a * a + 1.0)
        mult = jnp.where(m2 > 0.0, m2 * lax.rsqrt(m2), 0.0)
        if si == 0:
            mult = jnp.where(jnp.logical_and(first_rows, t == 0), 1.0, mult)
        u = gate_x * xc * mult

        hs = []
        for ti in range(sub):
            hcar = a[ti * nb:(ti + 1) * nb] * hcar + u[ti * nb:(ti + 1) * nb]
            hs.append(hcar)
        z = (jnp.concatenate(hs, axis=0) * y).astype(BF16)
        zn = _dot(permt_ref[...], z).astype(BF16)
        out = _dot(zn, wout_ref[...]).reshape(nb, sub, d)
        o_ref[:, si * sub:(si + 1) * sub, :] = x + g * out
    conv_scr[...] = tail
    h_scr[...] = hcar


def _rg_call(x, nw, sh, sc, g, w_in, conv_w, conv_b, gate_w, gate_b, lam, w_out):
    b, s, d = x.shape
    width = w_out.shape[0]
    blk = width // RG_HEADS
    nb = SUBLANES
    ts = min(RG_STEPS, s)
    sub = min(RG_SUB, ts)
    rows = nb * sub
    assert b % nb == 0 and s % ts == 0 and ts % sub == 0 and sub % SUBLANES == 0
    r = jnp.arange(rows, dtype=jnp.int32)
    perm = ((r[:, None] % nb) * sub + r[:, None] // nb == r[None, :]).astype(BF16)
    tok = pl.BlockSpec((nb, ts, d), lambda i, t: (i, t, 0))
    vec = pl.BlockSpec((nb, 1, d), lambda i, t: (i, 0, 0))
    return pl.pallas_call(
        functools.partial(_rg_kernel, nb=nb, ts=ts, sub=sub, width=width),
        grid=(b // nb, s // ts),
        in_specs=[tok, _resident((1, d)), vec, vec, vec, _resident((rows, rows)), _resident((rows, rows)),
                  _resident(w_in.shape), _resident((CONV_WIDTH, width)), _resident((1, width)),
                  _resident(gate_w.shape), _resident((RG_HEADS, 1, 2 * blk)), _resident((1, width)),
                  _resident(w_out.shape)],
        out_specs=tok,
        out_shape=jax.ShapeDtypeStruct(x.shape, F32),
        scratch_shapes=[pltpu.VMEM(((CONV_WIDTH - 1) * nb, width), F32), pltpu.VMEM((nb, width), F32)],
        compiler_params=_params(),
        name="rglru_mixer",
    )(x, nw.reshape(1, d), sh, sc, g, perm, perm.T, w_in, conv_w, conv_b.reshape(1, width), gate_w,
      gate_b.reshape(RG_HEADS, 1, 2 * blk), lam.reshape(1, width), w_out)


def _cumsum_rows(v, rows):
    n = v.shape[1]
    groups = rows // SUBLANES
    v3 = v.reshape(groups, SUBLANES, n)
    srow = lax.broadcasted_iota(jnp.int32, v3.shape, 1)
    for k in (1, 2, 4):
        v3 = v3 + jnp.where(srow >= k, pltpu.roll(v3, k, 1), 0.0)
    out, carry = [], None
    for gi in range(groups):
        blk = v3[gi] if carry is None else v3[gi] + carry
        out.append(blk)
        carry = blk[SUBLANES - 1:SUBLANES, :]
    return jnp.concatenate(out, axis=0)


def _anchor(b, half, rows):
    n = b.shape[1]
    if 2 * half >= SUBLANES:
        v = b.reshape(rows // (2 * half), 2 * half, n)
        return jnp.broadcast_to(v[:, half - 1:half, :], v.shape).reshape(rows, n)
    v = b.reshape(rows // SUBLANES, SUBLANES, n)
    srow = lax.broadcasted_iota(jnp.int32, v.shape, 1)
    if half == 2:
        lo = jnp.broadcast_to(v[:, 1:2, :], v.shape)
        hi = jnp.broadcast_to(v[:, 5:6, :], v.shape)
        return jnp.where(srow < 4, lo, hi).reshape(rows, n)
    assert half == 1
    return jnp.where((srow & 1) == 1, pltpu.roll(v, 1, 1), v).reshape(rows, n)


def _block_diag(a, b):
    top = jnp.concatenate([a, jnp.zeros_like(b)], axis=1)
    bot = jnp.concatenate([jnp.zeros_like(a), b], axis=1)
    return jnp.concatenate([top, bot], axis=0)


def _gla_masks(rows):
    qi = lax.broadcasted_iota(jnp.int32, (rows, 2 * rows), 0)
    kj = lax.broadcasted_iota(jnp.int32, (rows, 2 * rows), 1) & (rows - 1)
    masks = []
    half, lvl = 1, 0
    while half < rows:
        same = (qi >> (lvl + 1)) == (kj >> (lvl + 1))
        masks.append(same & ((qi & half) != 0) & ((kj & half) == 0))
        half, lvl = half * 2, lvl + 1
    return qi == kj, masks


def _gla_chunk(q, k, v, lf2, st_ref, eye, masks, *, heads, dk, dv, rows):
    b = _cumsum_rows(lf2, rows)
    b_last = b[rows - 1:rows, :]
    qe = (q * jnp.exp2(b)).astype(BF16)
    kd = (k * jnp.exp2(b_last - b)).astype(BF16)
    eb = jnp.exp2(b_last)
    qb = q.astype(BF16)
    kb = k.astype(BF16)
    vb = v.astype(BF16)
    rown = lax.broadcasted_iota(jnp.int32, b.shape, 0)
    mixed = []
    half = 1
    while half < rows:
        if half >= SUBLANES:
            parts = []
            for r0 in range(0, rows, half):
                if r0 & half:
                    parts.append(q[r0:r0 + half] * jnp.exp2(b[r0:r0 + half] - b[r0 - 1:r0]))
                else:
                    parts.append(k[r0:r0 + half] * jnp.exp2(b[r0 + half - 1:r0 + half] - b[r0:r0 + half]))
            mixed.append(jnp.concatenate(parts, axis=0).astype(BF16))
        else:
            e = jnp.exp2(_neg_abs(b - _anchor(b, half, rows)))
            mixed.append((jnp.where((rown & half) != 0, q, k) * e).astype(BF16))
        half *= 2
    ks = [slice(hd * dk, (hd + 1) * dk) for hd in range(heads)]
    vs = [slice(hd * dv, (hd + 1) * dv) for hd in range(heads)]
    inter = []
    for hd in range(heads):
        st = st_ref[hd]
        inter.append(_dot_nt(qe[:, ks[hd]], st.astype(BF16)))
        st_ref[hd] = st * eb[:, ks[hd]] + _dot_tn(vb[:, vs[hd]], kd[:, ks[hd]])
    scores = []
    for h0 in range(0, heads, 2):
        kc = slice(h0 * dk, (h0 + 2) * dk)
        sc = jnp.where(eye, _dot_nt(qb[:, kc], _block_diag(kb[:, ks[h0]], kb[:, ks[h0 + 1]])), 0.0)
        for m, mk in zip(mixed, masks):
            sc = jnp.where(mk, _dot_nt(m[:, kc], _block_diag(m[:, ks[h0]], m[:, ks[h0 + 1]])), sc)
        scores.append(sc.astype(BF16))
    outs = []
    for h0 in range(0, heads, 2):
        o2 = _dot(scores[h0 // 2], _block_diag(vb[:, vs[h0]], vb[:, vs[h0 + 1]]))
        outs.append(o2 + jnp.concatenate([inter[h0], inter[h0 + 1]], axis=1))
    return jnp.concatenate(outs, axis=1)


def _hg_kernel(x_ref, nw_ref, sh_ref, sc_ref, g_ref, win_ref, lbl_ref, onw_ref, wout_ref, o_ref,
               p_scr, o_scr, st_scr, *, tt, d, layer, heads):
    t = pl.program_id(1)

    @pl.when(t == 0)
    def _():
        st_scr[...] = jnp.zeros_like(st_scr)

    x = x_ref[0]
    h = _norm_mod(x, nw_ref[...], sh_ref[0], sc_ref[0]).astype(BF16)
    p_scr[...] = _dot(h, win_ref[...])

    lg = lbl_ref[...]
    rows_l = [lg[i:i + 1, :] for i in range(lg.shape[0])]
    mx = functools.reduce(jnp.maximum, rows_l)
    ex = [jnp.exp(r - mx) for r in rows_l]
    den = functools.reduce(lambda u, w: u + w, ex)
    lb = jnp.zeros_like(den)
    for i in range(1, layer + 1):
        lb = lb + ex[i] / den
    log_lb = jnp.log(lb)
    log_1m = jnp.log1p(-lb)
    one_m = 1.0 - lb

    eye, masks = _gla_masks(GLA_CHUNK)
    log_lb2 = log_lb * LOG2E
    log_1m2 = log_1m * LOG2E

    def chunk(ci, carry):
        r0 = pl.multiple_of(ci * GLA_CHUNK, GLA_CHUNK)
        rs = pl.ds(r0, GLA_CHUNK)
        q = p_scr[rs, 0:d]
        fz = p_scr[rs, d:2 * d]
        v = p_scr[rs, 2 * d:3 * d]
        fz2 = fz * LOG2E
        e = jnp.exp2(_neg_abs(fz2))
        u = 1.0 + e
        bv2 = (log_1m2 + jnp.minimum(fz2, 0.0)) - jnp.log2(u)
        lf2 = jnp.maximum(log_lb2, bv2) + jnp.log2(1.0 + jnp.exp2(_neg_abs(log_lb2 - bv2)))
        k = one_m * (jnp.where(fz >= 0.0, e, 1.0) / u)
        o_scr[rs, :] = _gla_chunk(q, k, v, lf2, st_scr, eye, masks, heads=heads, dk=d // heads, dv=d // heads,
                                  rows=GLA_CHUNK)
        return carry

    lax.fori_loop(0, tt // GLA_CHUNK, chunk, 0, unroll=GLA_UNROLL)

    o = _rmsnorm(o_scr[...], onw_ref[...]) * _sigmoid(p_scr[:, 3 * d:4 * d])
    out = _dot(o.astype(BF16), wout_ref[...])
    o_ref[0] = x + g_ref[0] * out


def _hg_call(x, nw, sh, sc, g, w_in, lb_logits, onw, w_out, layer):
    b, s, d = x.shape
    heads = d // HG_EXPAND
    tt = min(MIX_ROWS, s)
    tok = pl.BlockSpec((1, tt, d), lambda i, t: (i, t, 0))
    vec = pl.BlockSpec((1, 1, d), lambda i, t: (i, 0, 0))
    return pl.pallas_call(
        functools.partial(_hg_kernel, tt=tt, d=d, layer=layer, heads=heads),
        grid=(b, s // tt),
        in_specs=[tok, _resident((1, d)), vec, vec, vec, _resident(w_in.shape), _resident(lb_logits.shape),
                  _resident((1, d)), _resident(w_out.shape)],
        out_specs=tok,
        out_shape=jax.ShapeDtypeStruct(x.shape, F32),
        scratch_shapes=[pltpu.VMEM((tt, 4 * d), F32), pltpu.VMEM((tt, d), F32),
                        pltpu.VMEM((heads, HG_EXPAND, HG_EXPAND), F32)],
        compiler_params=_params(),
        name="hgrn2_mixer",
    )(x, nw.reshape(1, d), sh, sc, g, w_in, lb_logits, onw.reshape(1, d), w_out)


def _gla_kernel(x_ref, nw_ref, sh_ref, sc_ref, g_ref, win_ref, gw2_ref, gb_ref, onw_ref, wout_ref, o_ref,
                p_scr, o_scr, st_scr, *, tt, kd, vd, heads):
    t = pl.program_id(1)

    @pl.when(t == 0)
    def _():
        st_scr[...] = jnp.zeros_like(st_scr)

    x = x_ref[0]
    h = _norm_mod(x, nw_ref[...], sh_ref[0], sc_ref[0]).astype(BF16)
    p_scr[...] = _dot(h, win_ref[...])
    dk = kd // heads
    dv = vd // heads
    scale = dk ** -0.5
    low0 = 2 * kd + 2 * vd

    eye, masks = _gla_masks(GLA_CHUNK)

    def chunk(ci, carry):
        r0 = pl.multiple_of(ci * GLA_CHUNK, GLA_CHUNK)
        rs = pl.ds(r0, GLA_CHUNK)
        q = p_scr[rs, 0:kd] * scale
        k = p_scr[rs, kd:2 * kd]
        v = p_scr[rs, 2 * kd:2 * kd + vd]
        a_low = p_scr[rs, low0:low0 + LANES].astype(BF16)
        z2 = (_dot(a_low, gw2_ref[...]) + gb_ref[...]) * LOG2E
        lf2 = (jnp.minimum(z2, 0.0) - jnp.log2(1.0 + jnp.exp2(_neg_abs(z2)))) * (1.0 / GLA_LOGIT_NORM)
        o_scr[rs, :] = _gla_chunk(q, k, v, lf2, st_scr, eye, masks, heads=heads, dk=dk, dv=dv, rows=GLA_CHUNK)
        return carry

    lax.fori_loop(0, tt // GLA_CHUNK, chunk, 0, unroll=GLA_UNROLL)

    gate = p_scr[:, 2 * kd + vd:2 * kd + 2 * vd]
    onw = onw_ref[...]
    parts = []
    for hd in range(heads):
        vs = slice(hd * dv, (hd + 1) * dv)
        gh = gate[:, vs]
        parts.append(_rmsnorm(o_scr[:, vs], onw) * (gh * _sigmoid(gh)))
    out = _dot(jnp.concatenate(parts, axis=1).astype(BF16), wout_ref[...])
    o_ref[0] = x + g_ref[0] * out


def _gla_call(x, nw, sh, sc, g, w_in, gate_w2, gate_b, onw, w_out):
    b, s, d = x.shape
    kd = gate_w2.shape[1]
    vd = w_out.shape[0]
    heads = GLA_HEADS
    dv = vd // heads
    tt = min(MIX_ROWS, s)
    cols = 2 * kd + 2 * vd + LANES
    w_in_p = jnp.zeros((d, cols), BF16).at[:, :w_in.shape[1]].set(w_in)
    gw2_p = jnp.zeros((LANES, kd), BF16).at[:GLA_RANK, :].set(gate_w2)
    tok = pl.BlockSpec((1, tt, d), lambda i, t: (i, t, 0))
    vec = pl.BlockSpec((1, 1, d), lambda i, t: (i, 0, 0))
    return pl.pallas_call(
        functools.partial(_gla_kernel, tt=tt, kd=kd, vd=vd, heads=heads),
        grid=(b, s // tt),
        in_specs=[tok, _resident((1, d)), vec, vec, vec, _resident(w_in_p.shape), _resident(gw2_p.shape),
                  _resident((1, kd)), _resident((1, dv)), _resident(w_out.shape)],
        out_specs=tok,
        out_shape=jax.ShapeDtypeStruct(x.shape, F32),
        scratch_shapes=[pltpu.VMEM((tt, cols), F32), pltpu.VMEM((tt, vd), F32),
                        pltpu.VMEM((heads, dv, kd // heads), F32)],
        compiler_params=_params(),
        name="gla_mixer",
    )(x, nw.reshape(1, d), sh, sc, g, w_in_p, gw2_p, gate_b.reshape(1, kd), onw.reshape(1, dv), w_out)


def kernel(x, c, ada_w, ada_b, norm_w, final_norm_w, ffn_w13, ffn_w2, rg_w_in, rg_conv_w, rg_conv_b, rg_gate_w,
           rg_gate_b, rg_lambda, rg_w_out, hg_w_in, hg_lb_logits, hg_norm_w, hg_w_out, gla_w_in, gla_gate_w2,
           gla_gate_b, gla_norm_w, gla_w_out):
    depth = ada_w.shape[0]
    b, s, d = x.shape
    assert s % min(FFN_ROWS, s) == 0 and s % min(MIX_ROWS, s) == 0 and min(MIX_ROWS, s) % GLA_CHUNK == 0
    mods = _mod_call(c, ada_w, ada_b).reshape(depth * N_MOD, b, 1, d)
    bf = lambda w: w.astype(BF16)
    w13b, w2b = bf(ffn_w13), bf(ffn_w2)
    i_rg = i_hg = i_gla = 0
    for l in range(depth):
        sh1, sc1, g1, sh2, sc2, g2, sh3, sc3, g3 = [mods[l * N_MOD + j] for j in range(N_MOD)]
        x = _ffn_call(x, norm_w[l, 0], sh1, sc1, g1, w13b, w2b, l, 0)
        m = l % N_MIXERS
        if m == 0:
            x = _rg_call(x, norm_w[l, 1], sh2, sc2, g2, bf(rg_w_in[i_rg]), rg_conv_w[i_rg], rg_conv_b[i_rg],
                         bf(rg_gate_w[i_rg]), rg_gate_b[i_rg], rg_lambda[i_rg], bf(rg_w_out[i_rg]))
            i_rg += 1
        elif m == 1:
            x = _hg_call(x, norm_w[l, 1], sh2, sc2, g2, bf(hg_w_in[i_hg]), hg_lb_logits, hg_norm_w[i_hg],
                         bf(hg_w_out[i_hg]), l)
            i_hg += 1
        else:
            x = _gla_call(x, norm_w[l, 1], sh2, sc2, g2, bf(gla_w_in[i_gla]), bf(gla_gate_w2[i_gla]),
                          gla_gate_b[i_gla], gla_norm_w[i_gla], bf(gla_w_out[i_gla]))
            i_gla += 1
        x = _ffn_call(x, norm_w[l, 2], sh3, sc3, g3, w13b, w2b, l, 1,
                      final_w=final_norm_w if l == depth - 1 else None)
    return x
```

```python
import functools

import jax
import jax.numpy as jnp
from jax import lax
from jax.experimental import pallas as pl
from jax.experimental.pallas import tpu as pltpu

F32 = jnp.float32
BF16 = jnp.bfloat16

N_MIXERS = 3
N_MOD = 9
EPS = 1e-6
RG_HEADS = 8
CONV_WIDTH = 4
RG_C = 8.0
HG_EXPAND = 128
GLA_HEADS = 4
GLA_RANK = 16
GLA_LOGIT_NORM = 16.0
LOG2E = 1.4426950408889634

LANES = 128
SUBLANES = 8
MXU_COLS = 256
VMEM_LIMIT_BYTES = 56 * 1024 * 1024

FFN_ROWS = 1024
FFN_SUB = 512
FFN_COL_CHUNKS = 2
MIX_ROWS = 512
RG_STEPS = 128
RG_SUB = 32
GLA_CHUNK = 64


def _params():
    return pltpu.CompilerParams(dimension_semantics=("arbitrary", "arbitrary"),
                                vmem_limit_bytes=VMEM_LIMIT_BYTES)


def _resident(shape):
    zeros = (0,) * len(shape)
    return pl.BlockSpec(shape, lambda *_: zeros, pipeline_mode=pl.Buffered(1))


def _dot(a, b):
    return jnp.dot(a, b, preferred_element_type=F32)


def _dot_nt(a, b):
    return lax.dot_general(a, b, (((1,), (1,)), ((), ())), preferred_element_type=F32)


def _dot_tn(a, b):
    return lax.dot_general(a, b, (((0,), (0,)), ((), ())), preferred_element_type=F32)


def _neg_abs(x):
    bits = lax.bitcast_convert_type(x, jnp.uint32) | jnp.uint32(0x80000000)
    return lax.bitcast_convert_type(bits, F32)


def _sigmoid(x):
    return 1.0 / (1.0 + jnp.exp(-x))


def _softplus(x):
    return jnp.maximum(x, 0.0) + jnp.log1p(jnp.exp(-jnp.abs(x)))


def _log_sigmoid(x):
    return jnp.minimum(x, 0.0) - jnp.log(1.0 + jnp.exp(_neg_abs(x)))


def _rmsnorm(x, w):
    ms = jnp.mean(x * x, axis=-1, keepdims=True)
    return (x * lax.rsqrt(ms + EPS)) * w


def _norm_mod(x, nw, shift, scale):
    return _rmsnorm(x, nw) * (1.0 + scale) + shift


def _mod_kernel(c_ref, w_ref, b_ref, o_ref):
    c = c_ref[...]
    ca = c * _sigmoid(c)
    w = w_ref[0]
    c_hi = ca.astype(BF16)
    c_lo = (ca - c_hi.astype(F32)).astype(BF16)
    w_hi = w.astype(BF16)
    w_lo = (w - w_hi.astype(F32)).astype(BF16)
    acc = _dot(c_hi, w_hi) + (_dot(c_hi, w_lo) + _dot(c_lo, w_hi))
    o_ref[0] = acc + b_ref[0]


def _mod_call(c, ada_w, ada_b):
    depth, d, _ = ada_w.shape
    b = c.shape[0]
    return pl.pallas_call(
        _mod_kernel,
        grid=(depth, N_MOD),
        in_specs=[
            pl.BlockSpec((b, d), lambda l, j: (0, 0)),
            pl.BlockSpec((1, d, d), lambda l, j: (l, 0, j)),
            pl.BlockSpec((1, 1, d), lambda l, j: (l, 0, j)),
        ],
        out_specs=pl.BlockSpec((1, b, d), lambda l, j: (l * N_MOD + j, 0, 0)),
        out_shape=jax.ShapeDtypeStruct((depth * N_MOD, b, d), F32),
        compiler_params=_params(),
        name="adaln_table",
    )(c, ada_w, ada_b.reshape(depth, 1, N_MOD * d))


def _ffn_kernel(x_ref, nw_ref, sh_ref, sc_ref, g_ref, w13_ref, w2_ref, *rest, d_ff, tm, sub, final):
    if final:
        fw_ref, o_ref = rest
    else:
        (o_ref,) = rest
    nw, sh, sc = nw_ref[...], sh_ref[0], sc_ref[0]
    half_g = 0.5 * g_ref[0]
    tiles = -(-d_ff // MXU_COLS)
    per = -(-tiles // FFN_COL_CHUNKS)
    bounds = [min(c * per * MXU_COLS, d_ff) for c in range(FFN_COL_CHUNKS + 1)]
    for r0 in range(0, tm, sub):
        x = x_ref[0, r0:r0 + sub, :]
        h = _norm_mod(x, nw, sh, sc).astype(BF16)
        y = None
        for c0, c1 in zip(bounds[:-1], bounds[1:]):
            gate = _dot(h, w13_ref[:, c0:c1])
            up = _dot(h, w13_ref[:, d_ff + c0:d_ff + c1])
            act = (gate * _sigmoid(gate) * up).astype(BF16)
            part = _dot(act, w2_ref[c0:c1, :])
            y = part if y is None else y + part
        out = x + half_g * y
        if final:
            out = _rmsnorm(out, fw_ref[...])
        o_ref[0, r0:r0 + sub, :] = out


def _ffn_call(x, nw, sh, sc, g, w13, w2, layer, which, final_w=None):
    b, s, d = x.shape
    d_ff = w2.shape[2]
    tm = min(FFN_ROWS, s)
    sub = min(FFN_SUB, tm)
    assert tm % sub == 0 and d_ff % LANES == 0
    tok = pl.BlockSpec((1, tm, d), lambda i, t: (i, t, 0))
    vec = pl.BlockSpec((1, 1, d), lambda i, t: (i, 0, 0))
    pick = lambda i, t: (layer, which, 0, 0)
    w13_spec = pl.BlockSpec((None, None, d, 2 * d_ff), pick, pipeline_mode=pl.Buffered(1))
    w2_spec = pl.BlockSpec((None, None, d_ff, d), pick, pipeline_mode=pl.Buffered(1))
    in_specs = [tok, _resident((1, d)), vec, vec, vec, w13_spec, w2_spec]
    args = [x, nw.reshape(1, d), sh, sc, g, w13, w2]
    if final_w is not None:
        in_specs.append(_resident((1, d)))
        args.append(final_w.reshape(1, d))
    return pl.pallas_call(
        functools.partial(_ffn_kernel, d_ff=d_ff, tm=tm, sub=sub, final=final_w is not None),
        grid=(b, s // tm),
        in_specs=in_specs,
        out_specs=tok,
        out_shape=jax.ShapeDtypeStruct(x.shape, F32),
        compiler_params=_params(),
        name="swiglu_final" if final_w is not None else "swiglu",
    )(*args)


def _gelu_tanh(x):
    return x * (0.5 * (1.0 + jnp.tanh(0.7978845608028654 * (x + 0.044715 * (x * x * x)))))


def _rg_kernel(x_ref, nw_ref, sh_ref, sc_ref, g_ref, perm_ref, permt_ref, win_ref, cw_ref, cb_ref, gw_ref, gb_ref,
               lam_ref, wout_ref, o_ref, conv_scr, h_scr, *, nb, ts, sub, width):
    t = pl.program_id(1)
    rows = nb * sub
    d = x_ref.shape[2]
    blk = width // RG_HEADS
    tail_rows = (CONV_WIDTH - 1) * nb

    @pl.when(t == 0)
    def _():
        conv_scr[...] = jnp.zeros_like(conv_scr)
        h_scr[...] = jnp.zeros_like(h_scr)

    nw, sh, sc, g = nw_ref[...], sh_ref[...], sc_ref[...], g_ref[...]
    cw, cb = cw_ref[...], cb_ref[...]
    decay = -RG_C * _softplus(-lam_ref[...])
    first_rows = lax.broadcasted_iota(jnp.int32, (rows, 1), 0) < nb
    nsub = ts // sub
    st = {"tail": conv_scr[...], "h": h_scr[...]}
    xs, ys, xbs, xcs, gxs, gas = {}, {}, {}, {}, {}, {}

    def project(si):
        x = x_ref[:, si * sub:(si + 1) * sub, :]
        h = _norm_mod(x, nw, sh, sc).reshape(rows, d).astype(BF16)
        hp = _dot(perm_ref[...], h).astype(BF16)
        yx = _dot(hp, win_ref[...])
        xs[si] = x
        ys[si] = _gelu_tanh(yx[:, :width])
        xbs[si] = yx[:, width:]

    def conv_gates(si):
        xb = xbs.pop(si)
        ext = jnp.concatenate([st["tail"], xb], axis=0)
        xc = cb
        for j in range(CONV_WIDTH):
            xc = xc + ext[j * nb:j * nb + rows] * cw[j:j + 1]
        st["tail"] = xb[rows - tail_rows:rows]
        xcb = xc.astype(BF16)
        gx, ga = [], []
        for hd in range(RG_HEADS):
            gts = _dot(xcb[:, hd * blk:(hd + 1) * blk], gw_ref[hd]) + gb_ref[hd]
            gx.append(gts[:, :blk])
            ga.append(gts[:, blk:])
        xcs[si] = xc
        gxs[si] = jnp.concatenate(gx, axis=1)
        gas[si] = jnp.concatenate(ga, axis=1)

    def recur_out(si):
        log_a = _sigmoid(gas.pop(si)) * decay
        a = jnp.exp(log_a)
        m2 = -jnp.tanh(log_a) * (a * a + 1.0)
        mult = jnp.where(m2 > 0.0, m2 * lax.rsqrt(m2), 0.0)
        if si == 0:
            mult = jnp.where(jnp.logical_and(first_rows, t == 0), 1.0, mult)
        u = _sigmoid(gxs.pop(si)) * xcs.pop(si) * mult
        hcar = st["h"]
        hs = []
        for ti in range(sub):
            hcar = a[ti * nb:(ti + 1) * nb] * hcar + u[ti * nb:(ti + 1) * nb]
            hs.append(hcar)
        st["h"] = hcar
        z = (jnp.concatenate(hs, axis=0) * ys.pop(si)).astype(BF16)
        zn = _dot(permt_ref[...], z).astype(BF16)
        out = _dot(zn, wout_ref[...]).reshape(nb, sub, d)
        o_ref[:, si * sub:(si + 1) * sub, :] = xs.pop(si) + g * out

    for step in range(nsub + 2):
        if step < nsub:
            project(step)
        if 0 <= step - 1 < nsub:
            conv_gates(step - 1)
        if 0 <= step - 2 < nsub:
            recur_out(step - 2)
    conv_scr[...] = st["tail"]
    h_scr[...] = st["h"]


def _rg_call(x, nw, sh, sc, g, w_in, conv_w, conv_b, gate_w, gate_b, lam, w_out):
    b, s, d = x.shape
    width = w_out.shape[0]
    blk = width // RG_HEADS
    nb = SUBLANES
    ts = min(RG_STEPS, s)
    sub = min(RG_SUB, ts)
    rows = nb * sub
    assert b % nb == 0 and s % ts == 0 and ts % sub == 0 and sub % SUBLANES == 0
    r = jnp.arange(rows, dtype=jnp.int32)
    perm = ((r[:, None] % nb) * sub + r[:, None] // nb == r[None, :]).astype(BF16)
    tok = pl.BlockSpec((nb, ts, d), lambda i, t: (i, t, 0))
    vec = pl.BlockSpec((nb, 1, d), lambda i, t: (i, 0, 0))
    return pl.pallas_call(
        functools.partial(_rg_kernel, nb=nb, ts=ts, sub=sub, width=width),
        grid=(b // nb, s // ts),
        in_specs=[tok, _resident((1, d)), vec, vec, vec, _resident((rows, rows)), _resident((rows, rows)),
                  _resident(w_in.shape), _resident((CONV_WIDTH, width)), _resident((1, width)),
                  _resident(gate_w.shape), _resident((RG_HEADS, 1, 2 * blk)), _resident((1, width)),
                  _resident(w_out.shape)],
        out_specs=tok,
        out_shape=jax.ShapeDtypeStruct(x.shape, F32),
        scratch_shapes=[pltpu.VMEM(((CONV_WIDTH - 1) * nb, width), F32), pltpu.VMEM((nb, width), F32)],
        compiler_params=_params(),
        name="rglru_mixer",
    )(x, nw.reshape(1, d), sh, sc, g, perm, perm.T, w_in, conv_w, conv_b.reshape(1, width), gate_w,
      gate_b.reshape(RG_HEADS, 1, 2 * blk), lam.reshape(1, width), w_out)


def _cumsum_rows(v, rows):
    n = v.shape[1]
    groups = rows // SUBLANES
    v3 = v.reshape(groups, SUBLANES, n)
    srow = lax.broadcasted_iota(jnp.int32, v3.shape, 1)
    for k in (1, 2, 4):
        v3 = v3 + jnp.where(srow >= k, pltpu.roll(v3, k, 1), 0.0)
    out, carry = [], None
    for gi in range(groups):
        blk = v3[gi] if carry is None else v3[gi] + carry
        out.append(blk)
        carry = blk[SUBLANES - 1:SUBLANES, :]
    return jnp.concatenate(out, axis=0)


def _anchor(b, half, rows):
    n = b.shape[1]
    if 2 * half >= SUBLANES:
        v = b.reshape(rows // (2 * half), 2 * half, n)
        return jnp.broadcast_to(v[:, half - 1:half, :], v.shape).reshape(rows, n)
    v = b.reshape(rows // SUBLANES, SUBLANES, n)
    srow = lax.broadcasted_iota(jnp.int32, v.shape, 1)
    if half == 2:
        lo = jnp.broadcast_to(v[:, 1:2, :], v.shape)
        hi = jnp.broadcast_to(v[:, 5:6, :], v.shape)
        return jnp.where(srow < 4, lo, hi).reshape(rows, n)
    assert half == 1
    return jnp.where((srow & 1) == 1, pltpu.roll(v, 1, 1), v).reshape(rows, n)


def _block_diag(a, b):
    top = jnp.concatenate([a, jnp.zeros_like(b)], axis=1)
    bot = jnp.concatenate([jnp.zeros_like(a), b], axis=1)
    return jnp.concatenate([top, bot], axis=0)


def _gla_masks(rows):
    qi = lax.broadcasted_iota(jnp.int32, (rows, 2 * rows), 0)
    kj = lax.broadcasted_iota(jnp.int32, (rows, 2 * rows), 1) & (rows - 1)
    masks = []
    half, lvl = 1, 0
    while half < rows:
        same = (qi >> (lvl + 1)) == (kj >> (lvl + 1))
        masks.append(same & ((qi & half) != 0) & ((kj & half) == 0))
        half, lvl = half * 2, lvl + 1
    return qi == kj, masks


def _gla_prep(q, k, v, lf2, *, rows):
    b = _cumsum_rows(lf2, rows)
    b_last = b[rows - 1:rows, :]
    qe = (q * jnp.exp2(b)).astype(BF16)
    kd = (k * jnp.exp2(b_last - b)).astype(BF16)
    eb = jnp.exp2(b_last)
    vb = v.astype(BF16)
    rown = lax.broadcasted_iota(jnp.int32, b.shape, 0)
    mixed = []
    half = 1
    while half < rows:
        if half >= SUBLANES:
            parts = []
            for r0 in range(0, rows, half):
                if r0 & half:
                    parts.append(q[r0:r0 + half] * jnp.exp2(b[r0:r0 + half] - b[r0 - 1:r0]))
                else:
                    parts.append(k[r0:r0 + half] * jnp.exp2(b[r0 + half - 1:r0 + half] - b[r0:r0 + half]))
            mixed.append(jnp.concatenate(parts, axis=0).astype(BF16))
        else:
            e = jnp.exp2(_neg_abs(b - _anchor(b, half, rows)))
            mixed.append((jnp.where((rown & half) != 0, q, k) * e).astype(BF16))
        half *= 2
    return dict(qe=qe, kd=kd, eb=eb, vb=vb, mixed=mixed, qk=q * k)


def _gla_scores(ctx, st_ref, eye, masks, *, heads, dk, dv, rows):
    qe, kd, eb, vb, mixed, qk = (ctx[n] for n in ("qe", "kd", "eb", "vb", "mixed", "qk"))
    ks = [slice(hd * dk, (hd + 1) * dk) for hd in range(heads)]
    vs = [slice(hd * dv, (hd + 1) * dv) for hd in range(heads)]
    inter = []
    for hd in range(heads):
        st = st_ref[hd]
        inter.append(_dot_nt(qe[:, ks[hd]], st.astype(BF16)))
        st_ref[hd] = st * eb[:, ks[hd]] + _dot_tn(vb[:, vs[hd]], kd[:, ks[hd]])
    left = lax.broadcasted_iota(jnp.int32, (rows, 2 * rows), 1) < rows
    scores = []
    for h0 in range(0, heads, 2):
        kc = slice(h0 * dk, (h0 + 2) * dk)
        diag = jnp.where(left, jnp.sum(qk[:, ks[h0]], axis=-1, keepdims=True),
                         jnp.sum(qk[:, ks[h0 + 1]], axis=-1, keepdims=True))
        sc = jnp.where(eye, diag, 0.0)
        for m, mk in zip(mixed, masks):
            sc = jnp.where(mk, _dot_nt(m[:, kc], _block_diag(m[:, ks[h0]], m[:, ks[h0 + 1]])), sc)
        scores.append(sc.astype(BF16))
    ctx["inter"] = inter
    ctx["scores"] = scores


def _gla_out(ctx, *, heads, dv):
    vb, inter, scores = ctx["vb"], ctx["inter"], ctx["scores"]
    vs = [slice(hd * dv, (hd + 1) * dv) for hd in range(heads)]
    outs = []
    for h0 in range(0, heads, 2):
        o2 = _dot(scores[h0 // 2], _block_diag(vb[:, vs[h0]], vb[:, vs[h0 + 1]]))
        outs.append(o2 + jnp.concatenate([inter[h0], inter[h0 + 1]], axis=1))
    return jnp.concatenate(outs, axis=1)


def _gla_tile(n_chunks, load, store, st_ref, *, heads, dk, dv, rows):
    eye, masks = _gla_masks(rows)
    ctxs = {}
    for step in range(n_chunks + 2):
        if step < n_chunks:
            ctxs[step] = _gla_prep(*load(step), rows=rows)
        if 0 <= step - 1 < n_chunks:
            _gla_scores(ctxs[step - 1], st_ref, eye, masks, heads=heads, dk=dk, dv=dv, rows=rows)
        if 0 <= step - 2 < n_chunks:
            store(step - 2, _gla_out(ctxs.pop(step - 2), heads=heads, dv=dv))


def _hg_kernel(x_ref, nw_ref, sh_ref, sc_ref, g_ref, win_ref, lbl_ref, onw_ref, wout_ref, o_ref,
               p_scr, o_scr, st_scr, *, tt, d, layer, heads):
    t = pl.program_id(1)

    @pl.when(t == 0)
    def _():
        st_scr[...] = jnp.zeros_like(st_scr)

    x = x_ref[0]
    h = _norm_mod(x, nw_ref[...], sh_ref[0], sc_ref[0]).astype(BF16)
    p_scr[...] = _dot(h, win_ref[...])

    lg = lbl_ref[...]
    rows_l = [lg[i:i + 1, :] for i in range(lg.shape[0])]
    mx = functools.reduce(jnp.maximum, rows_l)
    ex = [jnp.exp(r - mx) for r in rows_l]
    den = functools.reduce(lambda u, w: u + w, ex)
    lb = jnp.zeros_like(den)
    for i in range(1, layer + 1):
        lb = lb + ex[i] / den
    log_lb = jnp.log(lb)
    log_1m = jnp.log1p(-lb)
    one_m = 1.0 - lb

    log_lb2 = log_lb * LOG2E
    log_1m2 = log_1m * LOG2E

    def load(ci):
        rs = slice(ci * GLA_CHUNK, (ci + 1) * GLA_CHUNK)
        q = p_scr[rs, 0:d]
        fz = p_scr[rs, d:2 * d]
        v = p_scr[rs, 2 * d:3 * d]
        fz2 = fz * LOG2E
        e = jnp.exp2(_neg_abs(fz2))
        u = 1.0 + e
        bv2 = (log_1m2 + jnp.minimum(fz2, 0.0)) - jnp.log2(u)
        lf2 = jnp.maximum(log_lb2, bv2) + jnp.log2(1.0 + jnp.exp2(_neg_abs(log_lb2 - bv2)))
        k = one_m * (jnp.where(fz >= 0.0, e, 1.0) / u)
        return q, k, v, lf2

    def store(ci, o):
        o_scr[ci * GLA_CHUNK:(ci + 1) * GLA_CHUNK, :] = o

    _gla_tile(tt // GLA_CHUNK, load, store, st_scr, heads=heads, dk=d // heads, dv=d // heads, rows=GLA_CHUNK)

    o = _rmsnorm(o_scr[...], onw_ref[...]) * _sigmoid(p_scr[:, 3 * d:4 * d])
    out = _dot(o.astype(BF16), wout_ref[...])
    o_ref[0] = x + g_ref[0] * out


def _hg_call(x, nw, sh, sc, g, w_in, lb_logits, onw, w_out, layer):
    b, s, d = x.shape
    heads = d // HG_EXPAND
    tt = min(MIX_ROWS, s)
    tok = pl.BlockSpec((1, tt, d), lambda i, t: (i, t, 0))
    vec = pl.BlockSpec((1, 1, d), lambda i, t: (i, 0, 0))
    return pl.pallas_call(
        functools.partial(_hg_kernel, tt=tt, d=d, layer=layer, heads=heads),
        grid=(b, s // tt),
        in_specs=[tok, _resident((1, d)), vec, vec, vec, _resident(w_in.shape), _resident(lb_logits.shape),
                  _resident((1, d)), _resident(w_out.shape)],
        out_specs=tok,
        out_shape=jax.ShapeDtypeStruct(x.shape, F32),
        scratch_shapes=[pltpu.VMEM((tt, 4 * d), F32), pltpu.VMEM((tt, d), F32),
                        pltpu.VMEM((heads, HG_EXPAND, HG_EXPAND), F32)],
        compiler_params=_params(),
        name="hgrn2_mixer",
    )(x, nw.reshape(1, d), sh, sc, g, w_in, lb_logits, onw.reshape(1, d), w_out)


def _gla_kernel(x_ref, nw_ref, sh_ref, sc_ref, g_ref, win_ref, gw2_ref, gb_ref, onw_ref, wout_ref, o_ref,
                p_scr, o_scr, st_scr, *, tt, kd, vd, heads):
    t = pl.program_id(1)

    @pl.when(t == 0)
    def _():
        st_scr[...] = jnp.zeros_like(st_scr)

    x = x_ref[0]
    h = _norm_mod(x, nw_ref[...], sh_ref[0], sc_ref[0]).astype(BF16)
    p_scr[...] = _dot(h, win_ref[...])
    dk = kd // heads
    dv = vd // heads
    scale = dk ** -0.5
    low0 = 2 * kd + 2 * vd

    def load(ci):
        rs = slice(ci * GLA_CHUNK, (ci + 1) * GLA_CHUNK)
        q = p_scr[rs, 0:kd] * scale
        k = p_scr[rs, kd:2 * kd]
        v = p_scr[rs, 2 * kd:2 * kd + vd]
        a_low = p_scr[rs, low0:low0 + LANES].astype(BF16)
        z2 = (_dot(a_low, gw2_ref[...]) + gb_ref[...]) * LOG2E
        lf2 = (jnp.minimum(z2, 0.0) - jnp.log2(1.0 + jnp.exp2(_neg_abs(z2)))) * (1.0 / GLA_LOGIT_NORM)
        return q, k, v, lf2

    def store(ci, o):
        o_scr[ci * GLA_CHUNK:(ci + 1) * GLA_CHUNK, :] = o

    _gla_tile(tt // GLA_CHUNK, load, store, st_scr, heads=heads, dk=dk, dv=dv, rows=GLA_CHUNK)

    gate = p_scr[:, 2 * kd + vd:2 * kd + 2 * vd]
    onw = onw_ref[...]
    parts = []
    for hd in range(heads):
        vs = slice(hd * dv, (hd + 1) * dv)
        gh = gate[:, vs]
        parts.append(_rmsnorm(o_scr[:, vs], onw) * (gh * _sigmoid(gh)))
    out = _dot(jnp.concatenate(parts, axis=1).astype(BF16), wout_ref[...])
    o_ref[0] = x + g_ref[0] * out


def _gla_call(x, nw, sh, sc, g, w_in, gate_w2, gate_b, onw, w_out):
    b, s, d = x.shape
    kd = gate_w2.shape[1]
    vd = w_out.shape[0]
    heads = GLA_HEADS
    dv = vd // heads
    tt = min(MIX_ROWS, s)
    cols = 2 * kd + 2 * vd + LANES
    w_in_p = jnp.zeros((d, cols), BF16).at[:, :w_in.shape[1]].set(w_in)
    gw2_p = jnp.zeros((LANES, kd), BF16).at[:GLA_RANK, :].set(gate_w2)
    tok = pl.BlockSpec((1, tt, d), lambda i, t: (i, t, 0))
    vec = pl.BlockSpec((1, 1, d), lambda i, t: (i, 0, 0))
    return pl.pallas_call(
        functools.partial(_gla_kernel, tt=tt, kd=kd, vd=vd, heads=heads),
        grid=(b, s // tt),
        in_specs=[tok, _resident((1, d)), vec, vec, vec, _resident(w_in_p.shape), _resident(gw2_p.shape),
                  _resident((1, kd)), _resident((1, dv)), _resident(w_out.shape)],
        out_specs=tok,
        out_shape=jax.ShapeDtypeStruct(x.shape, F32),
        scratch_shapes=[pltpu.VMEM((tt, cols), F32), pltpu.VMEM((tt, vd), F32),
                        pltpu.VMEM((heads, dv, kd // heads), F32)],
        compiler_params=_params(),
        name="gla_mixer",
    )(x, nw.reshape(1, d), sh, sc, g, w_in_p, gw2_p, gate_b.reshape(1, kd), onw.reshape(1, dv), w_out)


def kernel(x, c, ada_w, ada_b, norm_w, final_norm_w, ffn_w13, ffn_w2, rg_w_in, rg_conv_w, rg_conv_b, rg_gate_w,
           rg_gate_b, rg_lambda, rg_w_out, hg_w_in, hg_lb_logits, hg_norm_w, hg_w_out, gla_w_in, gla_gate_w2,
           gla_gate_b, gla_norm_w, gla_w_out):
    depth = ada_w.shape[0]
    b, s, d = x.shape
    assert s % min(FFN_ROWS, s) == 0 and s % min(MIX_ROWS, s) == 0 and min(MIX_ROWS, s) % GLA_CHUNK == 0
    mods = _mod_call(c, ada_w, ada_b).reshape(depth * N_MOD, b, 1, d)
    bf = lambda w: w.astype(BF16)
    w13b, w2b = bf(ffn_w13), bf(ffn_w2)
    i_rg = i_hg = i_gla = 0
    for l in range(depth):
        sh1, sc1, g1, sh2, sc2, g2, sh3, sc3, g3 = [mods[l * N_MOD + j] for j in range(N_MOD)]
        x = _ffn_call(x, norm_w[l, 0], sh1, sc1, g1, w13b, w2b, l, 0)
        m = l % N_MIXERS
        if m == 0:
            x = _rg_call(x, norm_w[l, 1], sh2, sc2, g2, bf(rg_w_in[i_rg]), rg_conv_w[i_rg], rg_conv_b[i_rg],
                         bf(rg_gate_w[i_rg]), rg_gate_b[i_rg], rg_lambda[i_rg], bf(rg_w_out[i_rg]))
            i_rg += 1
        elif m == 1:
            x = _hg_call(x, norm_w[l, 1], sh2, sc2, g2, bf(hg_w_in[i_hg]), hg_lb_logits, hg_norm_w[i_hg],
                         bf(hg_w_out[i_hg]), l)
            i_hg += 1
        else:
            x = _gla_call(x, norm_w[l, 1], sh2, sc2, g2, bf(gla_w_in[i_gla]), bf(gla_gate_w2[i_gla]),
                          gla_gate_b[i_gla], gla_norm_w[i_gla], bf(gla_w_out[i_gla]))
            i_gla += 1
        x = _ffn_call(x, norm_w[l, 2], sh3, sc3, g3, w13b, w2b, l, 1,
                      final_w=final_norm_w if l == depth - 1 else None)
    return x
```

```python
import functools

import jax
import jax.numpy as jnp
from jax import lax
from jax.experimental import pallas as pl
from jax.experimental.pallas import tpu as pltpu

F32 = jnp.float32
BF16 = jnp.bfloat16

N_MIXERS = 3
N_MOD = 9
EPS = 1e-6
RG_HEADS = 8
CONV_WIDTH = 4
RG_C = 8.0
HG_EXPAND = 128
GLA_HEADS = 4
GLA_RANK = 16
GLA_LOGIT_NORM = 16.0
LOG2E = 1.4426950408889634

LANES = 128
SUBLANES = 8
MXU_COLS = 256
VMEM_LIMIT_BYTES = 56 * 1024 * 1024

FFN_ROWS = 1024
FFN_SUB = 512
FFN_COL_CHUNKS = 2
MIX_ROWS = 512
MIX_SUB = 256
RG_STEPS = 128
RG_SUB = 32
GLA_CHUNK = 64


def _params():
    return pltpu.CompilerParams(dimension_semantics=("arbitrary", "arbitrary"),
                                vmem_limit_bytes=VMEM_LIMIT_BYTES)


def _resident(shape):
    zeros = (0,) * len(shape)
    return pl.BlockSpec(shape, lambda *_: zeros, pipeline_mode=pl.Buffered(1))


def _dot(a, b):
    return jnp.dot(a, b, preferred_element_type=F32)


def _dot_nt(a, b):
    return lax.dot_general(a, b, (((1,), (1,)), ((), ())), preferred_element_type=F32)


def _dot_tn(a, b):
    return lax.dot_general(a, b, (((0,), (0,)), ((), ())), preferred_element_type=F32)


def _neg_abs(x):
    bits = lax.bitcast_convert_type(x, jnp.uint32) | jnp.uint32(0x80000000)
    return lax.bitcast_convert_type(bits, F32)


def _sigmoid(x):
    return 1.0 / (1.0 + jnp.exp(-x))


def _softplus(x):
    return jnp.maximum(x, 0.0) + jnp.log1p(jnp.exp(-jnp.abs(x)))


def _log_sigmoid(x):
    return jnp.minimum(x, 0.0) - jnp.log(1.0 + jnp.exp(_neg_abs(x)))


def _rmsnorm(x, w):
    ms = jnp.mean(x * x, axis=-1, keepdims=True)
    return (x * lax.rsqrt(ms + EPS)) * w


def _norm_mod(x, nw, shift, scale):
    return _rmsnorm(x, nw) * (1.0 + scale) + shift


def _mod_kernel(c_ref, w_ref, b_ref, o_ref):
    c = c_ref[...]
    ca = c * _sigmoid(c)
    w = w_ref[0]
    c_hi = ca.astype(BF16)
    c_lo = (ca - c_hi.astype(F32)).astype(BF16)
    w_hi = w.astype(BF16)
    w_lo = (w - w_hi.astype(F32)).astype(BF16)
    acc = _dot(c_hi, w_hi) + (_dot(c_hi, w_lo) + _dot(c_lo, w_hi))
    o_ref[0] = acc + b_ref[0]


def _mod_call(c, ada_w, ada_b):
    depth, d, _ = ada_w.shape
    b = c.shape[0]
    return pl.pallas_call(
        _mod_kernel,
        grid=(depth, N_MOD),
        in_specs=[
            pl.BlockSpec((b, d), lambda l, j: (0, 0)),
            pl.BlockSpec((1, d, d), lambda l, j: (l, 0, j)),
            pl.BlockSpec((1, 1, d), lambda l, j: (l, 0, j)),
        ],
        out_specs=pl.BlockSpec((1, b, d), lambda l, j: (l * N_MOD + j, 0, 0)),
        out_shape=jax.ShapeDtypeStruct((depth * N_MOD, b, d), F32),
        compiler_params=_params(),
        name="adaln_table",
    )(c, ada_w, ada_b.reshape(depth, 1, N_MOD * d))


def _ffn_kernel(x_ref, nw_ref, sh_ref, sc_ref, g_ref, w13_ref, w2_ref, *rest, d_ff, tm, sub, final):
    if final:
        fw_ref, o_ref = rest
    else:
        (o_ref,) = rest
    nw, sh, sc = nw_ref[...], sh_ref[0], sc_ref[0]
    half_g = 0.5 * g_ref[0]
    tiles = -(-d_ff // MXU_COLS)
    per = -(-tiles // FFN_COL_CHUNKS)
    bounds = [min(c * per * MXU_COLS, d_ff) for c in range(FFN_COL_CHUNKS + 1)]
    for r0 in range(0, tm, sub):
        x = x_ref[0, r0:r0 + sub, :]
        h = _norm_mod(x, nw, sh, sc).astype(BF16)
        y = None
        for c0, c1 in zip(bounds[:-1], bounds[1:]):
            gate = _dot(h, w13_ref[:, c0:c1])
            up = _dot(h, w13_ref[:, d_ff + c0:d_ff + c1])
            act = (gate * _sigmoid(gate) * up).astype(BF16)
            part = _dot(act, w2_ref[c0:c1, :])
            y = part if y is None else y + part
        out = x + half_g * y
        if final:
            out = _rmsnorm(out, fw_ref[...])
        o_ref[0, r0:r0 + sub, :] = out


def _ffn_call(x, nw, sh, sc, g, w13, w2, layer, which, final_w=None):
    b, s, d = x.shape
    d_ff = w2.shape[2]
    tm = min(FFN_ROWS, s)
    sub = min(FFN_SUB, tm)
    assert tm % sub == 0 and d_ff % LANES == 0
    tok = pl.BlockSpec((1, tm, d), lambda i, t: (i, t, 0))
    vec = pl.BlockSpec((1, 1, d), lambda i, t: (i, 0, 0))
    pick = lambda i, t: (layer, which, 0, 0)
    w13_spec = pl.BlockSpec((None, None, d, 2 * d_ff), pick, pipeline_mode=pl.Buffered(1))
    w2_spec = pl.BlockSpec((None, None, d_ff, d), pick, pipeline_mode=pl.Buffered(1))
    in_specs = [tok, _resident((1, d)), vec, vec, vec, w13_spec, w2_spec]
    args = [x, nw.reshape(1, d), sh, sc, g, w13, w2]
    if final_w is not None:
        in_specs.append(_resident((1, d)))
        args.append(final_w.reshape(1, d))
    return pl.pallas_call(
        functools.partial(_ffn_kernel, d_ff=d_ff, tm=tm, sub=sub, final=final_w is not None),
        grid=(b, s // tm),
        in_specs=in_specs,
        out_specs=tok,
        out_shape=jax.ShapeDtypeStruct(x.shape, F32),
        compiler_params=_params(),
        name="swiglu_final" if final_w is not None else "swiglu",
    )(*args)


def _gelu_tanh(x):
    return x * (0.5 * (1.0 + jnp.tanh(0.7978845608028654 * (x + 0.044715 * (x * x * x)))))


def _rg_kernel(x_ref, nw_ref, sh_ref, sc_ref, g_ref, perm_ref, permt_ref, win_ref, cw_ref, cb_ref, gw_ref, gb_ref,
               lam_ref, wout_ref, o_ref, conv_scr, h_scr, *, nb, ts, sub, width):
    t = pl.program_id(1)
    rows = nb * sub
    d = x_ref.shape[2]
    blk = width // RG_HEADS
    tail_rows = (CONV_WIDTH - 1) * nb

    @pl.when(t == 0)
    def _():
        conv_scr[...] = jnp.zeros_like(conv_scr)
        h_scr[...] = jnp.zeros_like(h_scr)

    nw, sh, sc, g = nw_ref[...], sh_ref[...], sc_ref[...], g_ref[...]
    cw, cb = cw_ref[...], cb_ref[...]
    decay = -RG_C * _softplus(-lam_ref[...])
    first_rows = lax.broadcasted_iota(jnp.int32, (rows, 1), 0) < nb
    nsub = ts // sub
    st = {"tail": conv_scr[...], "h": h_scr[...]}
    xs, ys, xbs, xcs, gxs, gas = {}, {}, {}, {}, {}, {}

    def project(si):
        x = x_ref[:, si * sub:(si + 1) * sub, :]
        h = _norm_mod(x, nw, sh, sc).reshape(rows, d).astype(BF16)
        hp = _dot(perm_ref[...], h).astype(BF16)
        yx = _dot(hp, win_ref[...])
        xs[si] = x
        ys[si] = _gelu_tanh(yx[:, :width])
        xbs[si] = yx[:, width:]

    def conv_gates(si):
        xb = xbs.pop(si)
        ext = jnp.concatenate([st["tail"], xb], axis=0)
        xc = cb
        for j in range(CONV_WIDTH):
            xc = xc + ext[j * nb:j * nb + rows] * cw[j:j + 1]
        st["tail"] = xb[rows - tail_rows:rows]
        xcb = xc.astype(BF16)
        gx, ga = [], []
        for hd in range(RG_HEADS):
            gts = _dot(xcb[:, hd * blk:(hd + 1) * blk], gw_ref[hd]) + gb_ref[hd]
            gx.append(gts[:, :blk])
            ga.append(gts[:, blk:])
        xcs[si] = xc
        gxs[si] = jnp.concatenate(gx, axis=1)
        gas[si] = jnp.concatenate(ga, axis=1)

    def recur_out(si):
        log_a = _sigmoid(gas.pop(si)) * decay
        a = jnp.exp(log_a)
        m2 = -jnp.tanh(log_a) * (a * a + 1.0)
        mult = jnp.where(m2 > 0.0, m2 * lax.rsqrt(m2), 0.0)
        if si == 0:
            mult = jnp.where(jnp.logical_and(first_rows, t == 0), 1.0, mult)
        u = _sigmoid(gxs.pop(si)) * xcs.pop(si) * mult
        hcar = st["h"]
        hs = []
        for ti in range(sub):
            hcar = a[ti * nb:(ti + 1) * nb] * hcar + u[ti * nb:(ti + 1) * nb]
            hs.append(hcar)
        st["h"] = hcar
        z = (jnp.concatenate(hs, axis=0) * ys.pop(si)).astype(BF16)
        zn = _dot(permt_ref[...], z).astype(BF16)
        out = _dot(zn, wout_ref[...]).reshape(nb, sub, d)
        o_ref[:, si * sub:(si + 1) * sub, :] = xs.pop(si) + g * out

    for step in range(nsub + 2):
        if step < nsub:
            project(step)
        if 0 <= step - 1 < nsub:
            conv_gates(step - 1)
        if 0 <= step - 2 < nsub:
            recur_out(step - 2)
    conv_scr[...] = st["tail"]
    h_scr[...] = st["h"]


def _rg_call(x, nw, sh, sc, g, w_in, conv_w, conv_b, gate_w, gate_b, lam, w_out):
    b, s, d = x.shape
    width = w_out.shape[0]
    blk = width // RG_HEADS
    nb = SUBLANES
    ts = min(RG_STEPS, s)
    sub = min(RG_SUB, ts)
    rows = nb * sub
    assert b % nb == 0 and s % ts == 0 and ts % sub == 0 and sub % SUBLANES == 0
    r = jnp.arange(rows, dtype=jnp.int32)
    perm = ((r[:, None] % nb) * sub + r[:, None] // nb == r[None, :]).astype(BF16)
    tok = pl.BlockSpec((nb, ts, d), lambda i, t: (i, t, 0))
    vec = pl.BlockSpec((nb, 1, d), lambda i, t: (i, 0, 0))
    return pl.pallas_call(
        functools.partial(_rg_kernel, nb=nb, ts=ts, sub=sub, width=width),
        grid=(b // nb, s // ts),
        in_specs=[tok, _resident((1, d)), vec, vec, vec, _resident((rows, rows)), _resident((rows, rows)),
                  _resident(w_in.shape), _resident((CONV_WIDTH, width)), _resident((1, width)),
                  _resident(gate_w.shape), _resident((RG_HEADS, 1, 2 * blk)), _resident((1, width)),
                  _resident(w_out.shape)],
        out_specs=tok,
        out_shape=jax.ShapeDtypeStruct(x.shape, F32),
        scratch_shapes=[pltpu.VMEM(((CONV_WIDTH - 1) * nb, width), F32), pltpu.VMEM((nb, width), F32)],
        compiler_params=_params(),
        name="rglru_mixer",
    )(x, nw.reshape(1, d), sh, sc, g, perm, perm.T, w_in, conv_w, conv_b.reshape(1, width), gate_w,
      gate_b.reshape(RG_HEADS, 1, 2 * blk), lam.reshape(1, width), w_out)


def _cumsum_rows(v, rows):
    n = v.shape[1]
    groups = rows // SUBLANES
    v3 = v.reshape(groups, SUBLANES, n)
    srow = lax.broadcasted_iota(jnp.int32, v3.shape, 1)
    for k in (1, 2, 4):
        v3 = v3 + jnp.where(srow >= k, pltpu.roll(v3, k, 1), 0.0)
    out, carry = [], None
    for gi in range(groups):
        blk = v3[gi] if carry is None else v3[gi] + carry
        out.append(blk)
        carry = blk[SUBLANES - 1:SUBLANES, :]
    return jnp.concatenate(out, axis=0)


def _anchor(b, half, rows):
    n = b.shape[1]
    if 2 * half >= SUBLANES:
        v = b.reshape(rows // (2 * half), 2 * half, n)
        return jnp.broadcast_to(v[:, half - 1:half, :], v.shape).reshape(rows, n)
    v = b.reshape(rows // SUBLANES, SUBLANES, n)
    srow = lax.broadcasted_iota(jnp.int32, v.shape, 1)
    if half == 2:
        lo = jnp.broadcast_to(v[:, 1:2, :], v.shape)
        hi = jnp.broadcast_to(v[:, 5:6, :], v.shape)
        return jnp.where(srow < 4, lo, hi).reshape(rows, n)
    assert half == 1
    return jnp.where((srow & 1) == 1, pltpu.roll(v, 1, 1), v).reshape(rows, n)


def _block_diag(a, b):
    top = jnp.concatenate([a, jnp.zeros_like(b)], axis=1)
    bot = jnp.concatenate([jnp.zeros_like(a), b], axis=1)
    return jnp.concatenate([top, bot], axis=0)


def _gla_masks(rows):
    qi = lax.broadcasted_iota(jnp.int32, (rows, 2 * rows), 0)
    kj = lax.broadcasted_iota(jnp.int32, (rows, 2 * rows), 1) & (rows - 1)
    masks = []
    half, lvl = 1, 0
    while half < rows:
        same = (qi >> (lvl + 1)) == (kj >> (lvl + 1))
        masks.append(same & ((qi & half) != 0) & ((kj & half) == 0))
        half, lvl = half * 2, lvl + 1
    return qi == kj, masks


def _gla_prep(q, k, v, lf2, *, rows):
    b = _cumsum_rows(lf2, rows)
    b_last = b[rows - 1:rows, :]
    qe = (q * jnp.exp2(b)).astype(BF16)
    kd = (k * jnp.exp2(b_last - b)).astype(BF16)
    eb = jnp.exp2(b_last)
    vb = v.astype(BF16)
    rown = lax.broadcasted_iota(jnp.int32, b.shape, 0)
    mixed = []
    half = 1
    while half < rows:
        if half >= SUBLANES:
            parts = []
            for r0 in range(0, rows, half):
                if r0 & half:
                    parts.append(q[r0:r0 + half] * jnp.exp2(b[r0:r0 + half] - b[r0 - 1:r0]))
                else:
                    parts.append(k[r0:r0 + half] * jnp.exp2(b[r0 + half - 1:r0 + half] - b[r0:r0 + half]))
            mixed.append(jnp.concatenate(parts, axis=0).astype(BF16))
        else:
            e = jnp.exp2(_neg_abs(b - _anchor(b, half, rows)))
            mixed.append((jnp.where((rown & half) != 0, q, k) * e).astype(BF16))
        half *= 2
    return dict(qe=qe, kd=kd, eb=eb, vb=vb, mixed=mixed, qk=q * k)


def _gla_scores(ctx, st_ref, eye, masks, *, heads, dk, dv, rows):
    qe, kd, eb, vb, mixed, qk = (ctx[n] for n in ("qe", "kd", "eb", "vb", "mixed", "qk"))
    ks = [slice(hd * dk, (hd + 1) * dk) for hd in range(heads)]
    vs = [slice(hd * dv, (hd + 1) * dv) for hd in range(heads)]
    inter = []
    for hd in range(heads):
        st = st_ref[hd]
        inter.append(_dot_nt(qe[:, ks[hd]], st.astype(BF16)))
        st_ref[hd] = st * eb[:, ks[hd]] + _dot_tn(vb[:, vs[hd]], kd[:, ks[hd]])
    left = lax.broadcasted_iota(jnp.int32, (rows, 2 * rows), 1) < rows
    scores = []
    for h0 in range(0, heads, 2):
        kc = slice(h0 * dk, (h0 + 2) * dk)
        diag = jnp.where(left, jnp.sum(qk[:, ks[h0]], axis=-1, keepdims=True),
                         jnp.sum(qk[:, ks[h0 + 1]], axis=-1, keepdims=True))
        sc = jnp.where(eye, diag, 0.0)
        for m, mk in zip(mixed, masks):
            sc = jnp.where(mk, _dot_nt(m[:, kc], _block_diag(m[:, ks[h0]], m[:, ks[h0 + 1]])), sc)
        scores.append(sc.astype(BF16))
    ctx["inter"] = inter
    ctx["scores"] = scores


def _gla_out(ctx, *, heads, dv):
    vb, inter, scores = ctx["vb"], ctx["inter"], ctx["scores"]
    vs = [slice(hd * dv, (hd + 1) * dv) for hd in range(heads)]
    outs = []
    for h0 in range(0, heads, 2):
        o2 = _dot(scores[h0 // 2], _block_diag(vb[:, vs[h0]], vb[:, vs[h0 + 1]]))
        outs.append(o2 + jnp.concatenate([inter[h0], inter[h0 + 1]], axis=1))
    return jnp.concatenate(outs, axis=1)


def _gla_tile(n_chunks, cps, project, load, store, finish, st_ref, *, heads, dk, dv, rows):
    eye, masks = _gla_masks(rows)
    n_sub = n_chunks // cps
    ctxs = {}
    project(0)
    for step in range(n_chunks + 2):
        if step < n_chunks:
            ctxs[step] = _gla_prep(*load(step), rows=rows)
            if step % cps == 0 and step // cps + 1 < n_sub:
                project(step // cps + 1)
        if 0 <= step - 1 < n_chunks:
            _gla_scores(ctxs[step - 1], st_ref, eye, masks, heads=heads, dk=dk, dv=dv, rows=rows)
        if 0 <= step - 2 < n_chunks:
            store(step - 2, _gla_out(ctxs.pop(step - 2), heads=heads, dv=dv))
            if (step - 2) % cps == cps - 1:
                finish((step - 2) // cps)


def _hg_kernel(x_ref, nw_ref, sh_ref, sc_ref, g_ref, win_ref, lbl_ref, onw_ref, wout_ref, o_ref,
               st_scr, *, tt, sub, d, layer, heads):
    t = pl.program_id(1)

    @pl.when(t == 0)
    def _():
        st_scr[...] = jnp.zeros_like(st_scr)

    lg = lbl_ref[...]
    rows_l = [lg[i:i + 1, :] for i in range(lg.shape[0])]
    mx = functools.reduce(jnp.maximum, rows_l)
    ex = [jnp.exp(r - mx) for r in rows_l]
    den = functools.reduce(lambda u, w: u + w, ex)
    lb = jnp.zeros_like(den)
    for i in range(1, layer + 1):
        lb = lb + ex[i] / den
    log_lb = jnp.log(lb)
    log_1m = jnp.log1p(-lb)
    one_m = 1.0 - lb

    log_lb2 = log_lb * LOG2E
    log_1m2 = log_1m * LOG2E

    nw, sh, sc, g, onw = nw_ref[...], sh_ref[0], sc_ref[0], g_ref[0], onw_ref[...]
    cps = sub // GLA_CHUNK
    xs, ps, outs = {}, {}, {}

    def project(si):
        x = x_ref[0, si * sub:(si + 1) * sub, :]
        xs[si] = x
        ps[si] = _dot(_norm_mod(x, nw, sh, sc).astype(BF16), win_ref[...])

    def load(ci):
        p = ps[ci // cps]
        rs = slice((ci % cps) * GLA_CHUNK, (ci % cps + 1) * GLA_CHUNK)
        q = p[rs, 0:d]
        fz = p[rs, d:2 * d]
        v = p[rs, 2 * d:3 * d]
        fz2 = fz * LOG2E
        e = jnp.exp2(_neg_abs(fz2))
        u = 1.0 + e
        bv2 = (log_1m2 + jnp.minimum(fz2, 0.0)) - jnp.log2(u)
        lf2 = jnp.maximum(log_lb2, bv2) + jnp.log2(1.0 + jnp.exp2(_neg_abs(log_lb2 - bv2)))
        k = one_m * (jnp.where(fz >= 0.0, e, 1.0) / u)
        return q, k, v, lf2

    def store(ci, o):
        outs[ci] = o

    def finish(si):
        o = jnp.concatenate([outs.pop(ci) for ci in range(si * cps, (si + 1) * cps)], axis=0)
        o = _rmsnorm(o, onw) * _sigmoid(ps.pop(si)[:, 3 * d:4 * d])
        out = _dot(o.astype(BF16), wout_ref[...])
        o_ref[0, si * sub:(si + 1) * sub, :] = xs.pop(si) + g * out

    _gla_tile(tt // GLA_CHUNK, cps, project, load, store, finish, st_scr, heads=heads, dk=d // heads,
              dv=d // heads, rows=GLA_CHUNK)


def _hg_call(x, nw, sh, sc, g, w_in, lb_logits, onw, w_out, layer):
    b, s, d = x.shape
    heads = d // HG_EXPAND
    tt = min(MIX_ROWS, s)
    sub = min(MIX_SUB, tt)
    assert tt % sub == 0 and sub % GLA_CHUNK == 0
    tok = pl.BlockSpec((1, tt, d), lambda i, t: (i, t, 0))
    vec = pl.BlockSpec((1, 1, d), lambda i, t: (i, 0, 0))
    return pl.pallas_call(
        functools.partial(_hg_kernel, tt=tt, sub=sub, d=d, layer=layer, heads=heads),
        grid=(b, s // tt),
        in_specs=[tok, _resident((1, d)), vec, vec, vec, _resident(w_in.shape), _resident(lb_logits.shape),
                  _resident((1, d)), _resident(w_out.shape)],
        out_specs=tok,
        out_shape=jax.ShapeDtypeStruct(x.shape, F32),
        scratch_shapes=[pltpu.VMEM((heads, HG_EXPAND, HG_EXPAND), F32)],
        compiler_params=_params(),
        name="hgrn2_mixer",
    )(x, nw.reshape(1, d), sh, sc, g, w_in, lb_logits, onw.reshape(1, d), w_out)


def _gla_kernel(x_ref, nw_ref, sh_ref, sc_ref, g_ref, win_ref, gw2_ref, gb_ref, onw_ref, wout_ref, o_ref,
                st_scr, *, tt, sub, kd, vd, heads):
    t = pl.program_id(1)

    @pl.when(t == 0)
    def _():
        st_scr[...] = jnp.zeros_like(st_scr)

    dk = kd // heads
    dv = vd // heads
    scale = dk ** -0.5
    low0 = 2 * kd + 2 * vd
    nw, sh, sc, g, onw = nw_ref[...], sh_ref[0], sc_ref[0], g_ref[0], onw_ref[...]
    cps = sub // GLA_CHUNK
    xs, ps, outs = {}, {}, {}

    def project(si):
        x = x_ref[0, si * sub:(si + 1) * sub, :]
        xs[si] = x
        ps[si] = _dot(_norm_mod(x, nw, sh, sc).astype(BF16), win_ref[...])

    def load(ci):
        p = ps[ci // cps]
        rs = slice((ci % cps) * GLA_CHUNK, (ci % cps + 1) * GLA_CHUNK)
        q = p[rs, 0:kd] * scale
        k = p[rs, kd:2 * kd]
        v = p[rs, 2 * kd:2 * kd + vd]
        a_low = p[rs, low0:low0 + LANES].astype(BF16)
        z2 = (_dot(a_low, gw2_ref[...]) + gb_ref[...]) * LOG2E
        lf2 = (jnp.minimum(z2, 0.0) - jnp.log2(1.0 + jnp.exp2(_neg_abs(z2)))) * (1.0 / GLA_LOGIT_NORM)
        return q, k, v, lf2

    def store(ci, o):
        outs[ci] = o

    def finish(si):
        o = jnp.concatenate([outs.pop(ci) for ci in range(si * cps, (si + 1) * cps)], axis=0)
        gate = ps.pop(si)[:, 2 * kd + vd:2 * kd + 2 * vd]
        parts = []
        for hd in range(heads):
            vs = slice(hd * dv, (hd + 1) * dv)
            gh = gate[:, vs]
            parts.append(_rmsnorm(o[:, vs], onw) * (gh * _sigmoid(gh)))
        out = _dot(jnp.concatenate(parts, axis=1).astype(BF16), wout_ref[...])
        o_ref[0, si * sub:(si + 1) * sub, :] = xs.pop(si) + g * out

    _gla_tile(tt // GLA_CHUNK, cps, project, load, store, finish, st_scr, heads=heads, dk=dk, dv=dv,
              rows=GLA_CHUNK)


def _gla_call(x, nw, sh, sc, g, w_in, gate_w2, gate_b, onw, w_out):
    b, s, d = x.shape
    kd = gate_w2.shape[1]
    vd = w_out.shape[0]
    heads = GLA_HEADS
    dv = vd // heads
    tt = min(MIX_ROWS, s)
    sub = min(MIX_SUB, tt)
    assert tt % sub == 0 and sub % GLA_CHUNK == 0
    cols = 2 * kd + 2 * vd + LANES
    w_in_p = jnp.zeros((d, cols), BF16).at[:, :w_in.shape[1]].set(w_in)
    gw2_p = jnp.zeros((LANES, kd), BF16).at[:GLA_RANK, :].set(gate_w2)
    tok = pl.BlockSpec((1, tt, d), lambda i, t: (i, t, 0))
    vec = pl.BlockSpec((1, 1, d), lambda i, t: (i, 0, 0))
    return pl.pallas_call(
        functools.partial(_gla_kernel, tt=tt, sub=sub, kd=kd, vd=vd, heads=heads),
        grid=(b, s // tt),
        in_specs=[tok, _resident((1, d)), vec, vec, vec, _resident(w_in_p.shape), _resident(gw2_p.shape),
                  _resident((1, kd)), _resident((1, dv)), _resident(w_out.shape)],
        out_specs=tok,
        out_shape=jax.ShapeDtypeStruct(x.shape, F32),
        scratch_shapes=[pltpu.VMEM((heads, dv, kd // heads), F32)],
        compiler_params=_params(),
        name="gla_mixer",
    )(x, nw.reshape(1, d), sh, sc, g, w_in_p, gw2_p, gate_b.reshape(1, kd), onw.reshape(1, dv), w_out)


def kernel(x, c, ada_w, ada_b, norm_w, final_norm_w, ffn_w13, ffn_w2, rg_w_in, rg_conv_w, rg_conv_b, rg_gate_w,
           rg_gate_b, rg_lambda, rg_w_out, hg_w_in, hg_lb_logits, hg_norm_w, hg_w_out, gla_w_in, gla_gate_w2,
           gla_gate_b, gla_norm_w, gla_w_out):
    depth = ada_w.shape[0]
    b, s, d = x.shape
    assert s % min(FFN_ROWS, s) == 0 and s % min(MIX_ROWS, s) == 0 and min(MIX_ROWS, s) % GLA_CHUNK == 0
    mods = _mod_call(c, ada_w, ada_b).reshape(depth * N_MOD, b, 1, d)
    bf = lambda w: w.astype(BF16)
    w13b, w2b = bf(ffn_w13), bf(ffn_w2)
    i_rg = i_hg = i_gla = 0
    for l in range(depth):
        sh1, sc1, g1, sh2, sc2, g2, sh3, sc3, g3 = [mods[l * N_MOD + j] for j in range(N_MOD)]
        x = _ffn_call(x, norm_w[l, 0], sh1, sc1, g1, w13b, w2b, l, 0)
        m = l % N_MIXERS
        if m == 0:
            x = _rg_call(x, norm_w[l, 1], sh2, sc2, g2, bf(rg_w_in[i_rg]), rg_conv_w[i_rg], rg_conv_b[i_rg],
                         bf(rg_gate_w[i_rg]), rg_gate_b[i_rg], rg_lambda[i_rg], bf(rg_w_out[i_rg]))
            i_rg += 1
        elif m == 1:
            x = _hg_call(x, norm_w[l, 1], sh2, sc2, g2, bf(hg_w_in[i_hg]), hg_lb_logits, hg_norm_w[i_hg],
                         bf(hg_w_out[i_hg]), l)
            i_hg += 1
        else:
            x = _gla_call(x, norm_w[l, 1], sh2, sc2, g2, bf(gla_w_in[i_gla]), bf(gla_gate_w2[i_gla]),
                          gla_gate_b[i_gla], gla_norm_w[i_gla], bf(gla_w_out[i_gla]))
            i_gla += 1
        x = _ffn_call(x, norm_w[l, 2], sh3, sc3, g3, w13b, w2b, l, 1,
                      final_w=final_norm_w if l == depth - 1 else None)
    return x
```

```python
import functools

import jax
import jax.numpy as jnp
from jax import lax
from jax.experimental import pallas as pl
from jax.experimental.pallas import tpu as pltpu

F32 = jnp.float32
BF16 = jnp.bfloat16

N_MIXERS = 3
N_MOD = 9
EPS = 1e-6
RG_HEADS = 8
CONV_WIDTH = 4
RG_C = 8.0
HG_EXPAND = 128
GLA_HEADS = 4
GLA_RANK = 16
GLA_LOGIT_NORM = 16.0
LOG2E = 1.4426950408889634

LANES = 128
SUBLANES = 8
MXU_COLS = 256
VMEM_LIMIT_BYTES = 56 * 1024 * 1024

FFN_ROWS = 2048
FFN_SUB = 512
FFN_COL_CHUNKS = 2
MIX_ROWS = 512
RG_STEPS = 128
RG_SUB = 32
GLA_CHUNK = 64


def _params():
    return pltpu.CompilerParams(dimension_semantics=("arbitrary", "arbitrary"),
                                vmem_limit_bytes=VMEM_LIMIT_BYTES)


def _resident(shape):
    zeros = (0,) * len(shape)
    return pl.BlockSpec(shape, lambda *_: zeros, pipeline_mode=pl.Buffered(1))


def _dot(a, b):
    return jnp.dot(a, b, preferred_element_type=F32)


def _dot_nt(a, b):
    return lax.dot_general(a, b, (((1,), (1,)), ((), ())), preferred_element_type=F32)


def _dot_tn(a, b):
    return lax.dot_general(a, b, (((0,), (0,)), ((), ())), preferred_element_type=F32)


def _neg_abs(x):
    bits = lax.bitcast_convert_type(x, jnp.uint32) | jnp.uint32(0x80000000)
    return lax.bitcast_convert_type(bits, F32)


def _sigmoid(x):
    return 1.0 / (1.0 + jnp.exp(-x))


def _softplus(x):
    return jnp.maximum(x, 0.0) + jnp.log1p(jnp.exp(-jnp.abs(x)))


def _log_sigmoid(x):
    return jnp.minimum(x, 0.0) - jnp.log(1.0 + jnp.exp(_neg_abs(x)))


def _rmsnorm(x, w):
    ms = jnp.mean(x * x, axis=-1, keepdims=True)
    return (x * lax.rsqrt(ms + EPS)) * w


def _norm_mod(x, nw, shift, scale):
    return _rmsnorm(x, nw) * (1.0 + scale) + shift


def _mod_kernel(c_ref, w_ref, b_ref, o_ref):
    c = c_ref[...]
    ca = c * _sigmoid(c)
    w = w_ref[0]
    c_hi = ca.astype(BF16)
    c_lo = (ca - c_hi.astype(F32)).astype(BF16)
    w_hi = w.astype(BF16)
    w_lo = (w - w_hi.astype(F32)).astype(BF16)
    acc = _dot(c_hi, w_hi) + (_dot(c_hi, w_lo) + _dot(c_lo, w_hi))
    o_ref[0] = acc + b_ref[0]


def _mod_call(c, ada_w, ada_b):
    depth, d, _ = ada_w.shape
    b = c.shape[0]
    return pl.pallas_call(
        _mod_kernel,
        grid=(depth, N_MOD),
        in_specs=[
            pl.BlockSpec((b, d), lambda l, j: (0, 0)),
            pl.BlockSpec((1, d, d), lambda l, j: (l, 0, j)),
            pl.BlockSpec((1, 1, d), lambda l, j: (l, 0, j)),
        ],
        out_specs=pl.BlockSpec((1, b, d), lambda l, j: (l * N_MOD + j, 0, 0)),
        out_shape=jax.ShapeDtypeStruct((depth * N_MOD, b, d), F32),
        compiler_params=_params(),
        name="adaln_table",
    )(c, ada_w, ada_b.reshape(depth, 1, N_MOD * d))


def _ffn_kernel(x_ref, nw_ref, sh_ref, sc_ref, g_ref, w13_ref, w2_ref, *rest, d_ff, tm, sub, final):
    if final:
        fw_ref, o_ref = rest
    else:
        (o_ref,) = rest
    nw, sh, sc = nw_ref[...], sh_ref[0], sc_ref[0]
    half_g = 0.5 * g_ref[0]
    tiles = -(-d_ff // MXU_COLS)
    per = -(-tiles // FFN_COL_CHUNKS)
    bounds = [min(c * per * MXU_COLS, d_ff) for c in range(FFN_COL_CHUNKS + 1)]
    for r0 in range(0, tm, sub):
        x = x_ref[0, r0:r0 + sub, :]
        h = _norm_mod(x, nw, sh, sc).astype(BF16)
        y = None
        for c0, c1 in zip(bounds[:-1], bounds[1:]):
            gate = _dot(h, w13_ref[:, c0:c1])
            up = _dot(h, w13_ref[:, d_ff + c0:d_ff + c1])
            act = (gate * _sigmoid(gate) * up).astype(BF16)
            part = _dot(act, w2_ref[c0:c1, :])
            y = part if y is None else y + part
        out = x + half_g * y
        if final:
            out = _rmsnorm(out, fw_ref[...])
        o_ref[0, r0:r0 + sub, :] = out


def _ffn_call(x, nw, sh, sc, g, w13, w2, layer, which, final_w=None):
    b, s, d = x.shape
    d_ff = w2.shape[2]
    tm = min(FFN_ROWS, s)
    sub = min(FFN_SUB, tm)
    assert tm % sub == 0 and d_ff % LANES == 0
    tok = pl.BlockSpec((1, tm, d), lambda i, t: (i, t, 0))
    vec = pl.BlockSpec((1, 1, d), lambda i, t: (i, 0, 0))
    pick = lambda i, t: (layer, which, 0, 0)
    w13_spec = pl.BlockSpec((None, None, d, 2 * d_ff), pick, pipeline_mode=pl.Buffered(1))
    w2_spec = pl.BlockSpec((None, None, d_ff, d), pick, pipeline_mode=pl.Buffered(1))
    in_specs = [tok, _resident((1, d)), vec, vec, vec, w13_spec, w2_spec]
    args = [x, nw.reshape(1, d), sh, sc, g, w13, w2]
    if final_w is not None:
        in_specs.append(_resident((1, d)))
        args.append(final_w.reshape(1, d))
    return pl.pallas_call(
        functools.partial(_ffn_kernel, d_ff=d_ff, tm=tm, sub=sub, final=final_w is not None),
        grid=(b, s // tm),
        in_specs=in_specs,
        out_specs=pl.BlockSpec((1, tm, d), lambda i, t: (i, t, 0), pipeline_mode=pl.Buffered(1)),
        out_shape=jax.ShapeDtypeStruct(x.shape, F32),
        compiler_params=_params(),
        name="swiglu_final" if final_w is not None else "swiglu",
    )(*args)


def _gelu_tanh(x):
    return x * (0.5 * (1.0 + jnp.tanh(0.7978845608028654 * (x + 0.044715 * (x * x * x)))))


def _rg_kernel(x_ref, nw_ref, sh_ref, sc_ref, g_ref, perm_ref, permt_ref, win_ref, cw_ref, cb_ref, gw_ref, gb_ref,
               lam_ref, wout_ref, o_ref, conv_scr, h_scr, *, nb, ts, sub, width):
    t = pl.program_id(1)
    rows = nb * sub
    d = x_ref.shape[2]
    blk = width // RG_HEADS
    tail_rows = (CONV_WIDTH - 1) * nb

    @pl.when(t == 0)
    def _():
        conv_scr[...] = jnp.zeros_like(conv_scr)
        h_scr[...] = jnp.zeros_like(h_scr)

    nw, sh, sc, g = nw_ref[...], sh_ref[...], sc_ref[...], g_ref[...]
    cw, cb = cw_ref[...], cb_ref[...]
    decay = -RG_C * _softplus(-lam_ref[...])
    first_rows = lax.broadcasted_iota(jnp.int32, (rows, 1), 0) < nb
    nsub = ts // sub
    st = {"tail": conv_scr[...], "h": h_scr[...]}
    xs, ys, xbs, xcs, gxs, gas = {}, {}, {}, {}, {}, {}

    def project(si):
        x = x_ref[:, si * sub:(si + 1) * sub, :]
        h = _norm_mod(x, nw, sh, sc).reshape(rows, d).astype(BF16)
        hp = _dot(perm_ref[...], h).astype(BF16)
        yx = _dot(hp, win_ref[...])
        xs[si] = x
        ys[si] = _gelu_tanh(yx[:, :width])
        xbs[si] = yx[:, width:]

    def conv_gates(si):
        xb = xbs.pop(si)
        ext = jnp.concatenate([st["tail"], xb], axis=0)
        xc = cb
        for j in range(CONV_WIDTH):
            xc = xc + ext[j * nb:j * nb + rows] * cw[j:j + 1]
        st["tail"] = xb[rows - tail_rows:rows]
        xcb = xc.astype(BF16)
        gx, ga = [], []
        for hd in range(RG_HEADS):
            gts = _dot(xcb[:, hd * blk:(hd + 1) * blk], gw_ref[hd]) + gb_ref[hd]
            gx.append(gts[:, :blk])
            ga.append(gts[:, blk:])
        xcs[si] = xc
        gxs[si] = jnp.concatenate(gx, axis=1)
        gas[si] = jnp.concatenate(ga, axis=1)

    def recur_out(si):
        log_a = _sigmoid(gas.pop(si)) * decay
        a = jnp.exp(log_a)
        m2 = -jnp.tanh(log_a) * (a * a + 1.0)
        mult = jnp.where(m2 > 0.0, m2 * lax.rsqrt(m2), 0.0)
        if si == 0:
            mult = jnp.where(jnp.logical_and(first_rows, t == 0), 1.0, mult)
        u = _sigmoid(gxs.pop(si)) * xcs.pop(si) * mult
        hcar = st["h"]
        hs = []
        for ti in range(sub):
            hcar = a[ti * nb:(ti + 1) * nb] * hcar + u[ti * nb:(ti + 1) * nb]
            hs.append(hcar)
        st["h"] = hcar
        z = (jnp.concatenate(hs, axis=0) * ys.pop(si)).astype(BF16)
        zn = _dot(permt_ref[...], z).astype(BF16)
        out = _dot(zn, wout_ref[...]).reshape(nb, sub, d)
        o_ref[:, si * sub:(si + 1) * sub, :] = xs.pop(si) + g * out

    for step in range(nsub + 2):
        if step < nsub:
            project(step)
        if 0 <= step - 1 < nsub:
            conv_gates(step - 1)
        if 0 <= step - 2 < nsub:
            recur_out(step - 2)
    conv_scr[...] = st["tail"]
    h_scr[...] = st["h"]


def _rg_call(x, nw, sh, sc, g, w_in, conv_w, conv_b, gate_w, gate_b, lam, w_out):
    b, s, d = x.shape
    width = w_out.shape[0]
    blk = width // RG_HEADS
    nb = SUBLANES
    ts = min(RG_STEPS, s)
    sub = min(RG_SUB, ts)
    rows = nb * sub
    assert b % nb == 0 and s % ts == 0 and ts % sub == 0 and sub % SUBLANES == 0
    r = jnp.arange(rows, dtype=jnp.int32)
    perm = ((r[:, None] % nb) * sub + r[:, None] // nb == r[None, :]).astype(BF16)
    tok = pl.BlockSpec((nb, ts, d), lambda i, t: (i, t, 0))
    vec = pl.BlockSpec((nb, 1, d), lambda i, t: (i, 0, 0))
    return pl.pallas_call(
        functools.partial(_rg_kernel, nb=nb, ts=ts, sub=sub, width=width),
        grid=(b // nb, s // ts),
        in_specs=[tok, _resident((1, d)), vec, vec, vec, _resident((rows, rows)), _resident((rows, rows)),
                  _resident(w_in.shape), _resident((CONV_WIDTH, width)), _resident((1, width)),
                  _resident(gate_w.shape), _resident((RG_HEADS, 1, 2 * blk)), _resident((1, width)),
                  _resident(w_out.shape)],
        out_specs=tok,
        out_shape=jax.ShapeDtypeStruct(x.shape, F32),
        scratch_shapes=[pltpu.VMEM(((CONV_WIDTH - 1) * nb, width), F32), pltpu.VMEM((nb, width), F32)],
        compiler_params=_params(),
        name="rglru_mixer",
    )(x, nw.reshape(1, d), sh, sc, g, perm, perm.T, w_in, conv_w, conv_b.reshape(1, width), gate_w,
      gate_b.reshape(RG_HEADS, 1, 2 * blk), lam.reshape(1, width), w_out)


def _cumsum_rows(v, rows):
    n = v.shape[1]
    groups = rows // SUBLANES
    v3 = v.reshape(groups, SUBLANES, n)
    srow = lax.broadcasted_iota(jnp.int32, v3.shape, 1)
    for k in (1, 2, 4):
        v3 = v3 + jnp.where(srow >= k, pltpu.roll(v3, k, 1), 0.0)
    out, carry = [], None
    for gi in range(groups):
        blk = v3[gi] if carry is None else v3[gi] + carry
        out.append(blk)
        carry = blk[SUBLANES - 1:SUBLANES, :]
    return jnp.concatenate(out, axis=0)


def _anchor(b, half, rows):
    n = b.shape[1]
    if 2 * half >= SUBLANES:
        v = b.reshape(rows // (2 * half), 2 * half, n)
        return jnp.broadcast_to(v[:, half - 1:half, :], v.shape).reshape(rows, n)
    v = b.reshape(rows // SUBLANES, SUBLANES, n)
    srow = lax.broadcasted_iota(jnp.int32, v.shape, 1)
    if half == 2:
        lo = jnp.broadcast_to(v[:, 1:2, :], v.shape)
        hi = jnp.broadcast_to(v[:, 5:6, :], v.shape)
        return jnp.where(srow < 4, lo, hi).reshape(rows, n)
    assert half == 1
    return jnp.where((srow & 1) == 1, pltpu.roll(v, 1, 1), v).reshape(rows, n)


def _block_diag(a, b):
    top = jnp.concatenate([a, jnp.zeros_like(b)], axis=1)
    bot = jnp.concatenate([jnp.zeros_like(a), b], axis=1)
    return jnp.concatenate([top, bot], axis=0)


def _gla_masks(rows):
    qi = lax.broadcasted_iota(jnp.int32, (rows, 2 * rows), 0)
    kj = lax.broadcasted_iota(jnp.int32, (rows, 2 * rows), 1) & (rows - 1)
    masks = []
    half, lvl = 1, 0
    while half < rows:
        same = (qi >> (lvl + 1)) == (kj >> (lvl + 1))
        masks.append(same & ((qi & half) != 0) & ((kj & half) == 0))
        half, lvl = half * 2, lvl + 1
    return qi == kj, masks


def _gla_prep(q, k, v, lf2, *, rows):
    b = _cumsum_rows(lf2, rows)
    b_last = b[rows - 1:rows, :]
    qe = (q * jnp.exp2(b)).astype(BF16)
    kd = (k * jnp.exp2(b_last - b)).astype(BF16)
    eb = jnp.exp2(b_last)
    vb = v.astype(BF16)
    rown = lax.broadcasted_iota(jnp.int32, b.shape, 0)
    mixed = []
    half = 1
    while half < rows:
        if half >= SUBLANES:
            parts = []
            for r0 in range(0, rows, half):
                if r0 & half:
                    parts.append(q[r0:r0 + half] * jnp.exp2(b[r0:r0 + half] - b[r0 - 1:r0]))
                else:
                    parts.append(k[r0:r0 + half] * jnp.exp2(b[r0 + half - 1:r0 + half] - b[r0:r0 + half]))
            mixed.append(jnp.concatenate(parts, axis=0).astype(BF16))
        else:
            e = jnp.exp2(_neg_abs(b - _anchor(b, half, rows)))
            mixed.append((jnp.where((rown & half) != 0, q, k) * e).astype(BF16))
        half *= 2
    return dict(qe=qe, kd=kd, eb=eb, vb=vb, mixed=mixed, qk=q * k)


def _gla_scores(ctx, st_ref, eye, masks, *, heads, dk, dv, rows):
    qe, kd, eb, vb, mixed, qk = (ctx[n] for n in ("qe", "kd", "eb", "vb", "mixed", "qk"))
    ks = [slice(hd * dk, (hd + 1) * dk) for hd in range(heads)]
    vs = [slice(hd * dv, (hd + 1) * dv) for hd in range(heads)]
    inter = []
    for hd in range(heads):
        st = st_ref[hd]
        inter.append(_dot_nt(qe[:, ks[hd]], st.astype(BF16)))
        st_ref[hd] = st * eb[:, ks[hd]] + _dot_tn(vb[:, vs[hd]], kd[:, ks[hd]])
    left = lax.broadcasted_iota(jnp.int32, (rows, 2 * rows), 1) < rows
    scores = []
    for h0 in range(0, heads, 2):
        kc = slice(h0 * dk, (h0 + 2) * dk)
        diag = jnp.where(left, jnp.sum(qk[:, ks[h0]], axis=-1, keepdims=True),
                         jnp.sum(qk[:, ks[h0 + 1]], axis=-1, keepdims=True))
        sc = jnp.where(eye, diag, 0.0)
        for m, mk in zip(mixed, masks):
            sc = jnp.where(mk, _dot_nt(m[:, kc], _block_diag(m[:, ks[h0]], m[:, ks[h0 + 1]])), sc)
        scores.append(sc.astype(BF16))
    ctx["inter"] = inter
    ctx["scores"] = scores


def _gla_out(ctx, *, heads, dv):
    vb, inter, scores = ctx["vb"], ctx["inter"], ctx["scores"]
    vs = [slice(hd * dv, (hd + 1) * dv) for hd in range(heads)]
    outs = []
    for h0 in range(0, heads, 2):
        o2 = _dot(scores[h0 // 2], _block_diag(vb[:, vs[h0]], vb[:, vs[h0 + 1]]))
        outs.append(o2 + jnp.concatenate([inter[h0], inter[h0 + 1]], axis=1))
    return jnp.concatenate(outs, axis=1)


def _gla_tile(n_chunks, load, store, st_ref, *, heads, dk, dv, rows):
    eye, masks = _gla_masks(rows)
    ctxs = {}
    for step in range(n_chunks + 2):
        if step < n_chunks:
            ctxs[step] = _gla_prep(*load(step), rows=rows)
        if 0 <= step - 1 < n_chunks:
            _gla_scores(ctxs[step - 1], st_ref, eye, masks, heads=heads, dk=dk, dv=dv, rows=rows)
        if 0 <= step - 2 < n_chunks:
            store(step - 2, _gla_out(ctxs.pop(step - 2), heads=heads, dv=dv))


def _hg_kernel(x_ref, nw_ref, sh_ref, sc_ref, g_ref, win_ref, lbl_ref, onw_ref, wout_ref, o_ref,
               p_scr, o_scr, st_scr, *, tt, d, layer, heads):
    t = pl.program_id(1)

    @pl.when(t == 0)
    def _():
        st_scr[...] = jnp.zeros_like(st_scr)

    x = x_ref[0]
    h = _norm_mod(x, nw_ref[...], sh_ref[0], sc_ref[0]).astype(BF16)
    p_scr[...] = _dot(h, win_ref[...])

    lg = lbl_ref[...]
    rows_l = [lg[i:i + 1, :] for i in range(lg.shape[0])]
    mx = functools.reduce(jnp.maximum, rows_l)
    ex = [jnp.exp(r - mx) for r in rows_l]
    den = functools.reduce(lambda u, w: u + w, ex)
    lb = jnp.zeros_like(den)
    for i in range(1, layer + 1):
        lb = lb + ex[i] / den
    log_lb = jnp.log(lb)
    log_1m = jnp.log1p(-lb)
    one_m = 1.0 - lb

    log_lb2 = log_lb * LOG2E
    log_1m2 = log_1m * LOG2E

    def load(ci):
        rs = slice(ci * GLA_CHUNK, (ci + 1) * GLA_CHUNK)
        q = p_scr[rs, 0:d]
        fz = p_scr[rs, d:2 * d]
        v = p_scr[rs, 2 * d:3 * d]
        fz2 = fz * LOG2E
        e = jnp.exp2(_neg_abs(fz2))
        u = 1.0 + e
        bv2 = (log_1m2 + jnp.minimum(fz2, 0.0)) - jnp.log2(u)
        lf2 = jnp.maximum(log_lb2, bv2) + jnp.log2(1.0 + jnp.exp2(_neg_abs(log_lb2 - bv2)))
        k = one_m * (jnp.where(fz >= 0.0, e, 1.0) / u)
        return q, k, v, lf2

    def store(ci, o):
        o_scr[ci * GLA_CHUNK:(ci + 1) * GLA_CHUNK, :] = o

    _gla_tile(tt // GLA_CHUNK, load, store, st_scr, heads=heads, dk=d // heads, dv=d // heads, rows=GLA_CHUNK)

    o = _rmsnorm(o_scr[...], onw_ref[...]) * _sigmoid(p_scr[:, 3 * d:4 * d])
    out = _dot(o.astype(BF16), wout_ref[...])
    o_ref[0] = x + g_ref[0] * out


def _hg_call(x, nw, sh, sc, g, w_in, lb_logits, onw, w_out, layer):
    b, s, d = x.shape
    heads = d // HG_EXPAND
    tt = min(MIX_ROWS, s)
    tok = pl.BlockSpec((1, tt, d), lambda i, t: (i, t, 0))
    vec = pl.BlockSpec((1, 1, d), lambda i, t: (i, 0, 0))
    return pl.pallas_call(
        functools.partial(_hg_kernel, tt=tt, d=d, layer=layer, heads=heads),
        grid=(b, s // tt),
        in_specs=[tok, _resident((1, d)), vec, vec, vec, _resident(w_in.shape), _resident(lb_logits.shape),
                  _resident((1, d)), _resident(w_out.shape)],
        out_specs=tok,
        out_shape=jax.ShapeDtypeStruct(x.shape, F32),
        scratch_shapes=[pltpu.VMEM((tt, 4 * d), F32), pltpu.VMEM((tt, d), F32),
                        pltpu.VMEM((heads, HG_EXPAND, HG_EXPAND), F32)],
        compiler_params=_params(),
        name="hgrn2_mixer",
    )(x, nw.reshape(1, d), sh, sc, g, w_in, lb_logits, onw.reshape(1, d), w_out)


def _gla_kernel(x_ref, nw_ref, sh_ref, sc_ref, g_ref, win_ref, gw2_ref, gb_ref, onw_ref, wout_ref, o_ref,
                p_scr, o_scr, st_scr, *, tt, kd, vd, heads):
    t = pl.program_id(1)

    @pl.when(t == 0)
    def _():
        st_scr[...] = jnp.zeros_like(st_scr)

    x = x_ref[0]
    h = _norm_mod(x, nw_ref[...], sh_ref[0], sc_ref[0]).astype(BF16)
    p_scr[...] = _dot(h, win_ref[...])
    dk = kd // heads
    dv = vd // heads
    scale = dk ** -0.5
    low0 = 2 * kd + 2 * vd

    def load(ci):
        rs = slice(ci * GLA_CHUNK, (ci + 1) * GLA_CHUNK)
        q = p_scr[rs, 0:kd] * scale
        k = p_scr[rs, kd:2 * kd]
        v = p_scr[rs, 2 * kd:2 * kd + vd]
        a_low = p_scr[rs, low0:low0 + LANES].astype(BF16)
        z2 = (_dot(a_low, gw2_ref[...]) + gb_ref[...]) * LOG2E
        lf2 = (jnp.minimum(z2, 0.0) - jnp.log2(1.0 + jnp.exp2(_neg_abs(z2)))) * (1.0 / GLA_LOGIT_NORM)
        return q, k, v, lf2

    def store(ci, o):
        o_scr[ci * GLA_CHUNK:(ci + 1) * GLA_CHUNK, :] = o

    _gla_tile(tt // GLA_CHUNK, load, store, st_scr, heads=heads, dk=dk, dv=dv, rows=GLA_CHUNK)

    gate = p_scr[:, 2 * kd + vd:2 * kd + 2 * vd]
    onw = onw_ref[...]
    parts = []
    for hd in range(heads):
        vs = slice(hd * dv, (hd + 1) * dv)
        gh = gate[:, vs]
        parts.append(_rmsnorm(o_scr[:, vs], onw) * (gh * _sigmoid(gh)))
    out = _dot(jnp.concatenate(parts, axis=1).astype(BF16), wout_ref[...])
    o_ref[0] = x + g_ref[0] * out


def _gla_call(x, nw, sh, sc, g, w_in, gate_w2, gate_b, onw, w_out):
    b, s, d = x.shape
    kd = gate_w2.shape[1]
    vd = w_out.shape[0]
    heads = GLA_HEADS
    dv = vd // heads
    tt = min(MIX_ROWS, s)
    cols = 2 * kd + 2 * vd + LANES
    w_in_p = jnp.zeros((d, cols), BF16).at[:, :w_in.shape[1]].set(w_in)
    gw2_p = jnp.zeros((LANES, kd), BF16).at[:GLA_RANK, :].set(gate_w2)
    tok = pl.BlockSpec((1, tt, d), lambda i, t: (i, t, 0))
    vec = pl.BlockSpec((1, 1, d), lambda i, t: (i, 0, 0))
    return pl.pallas_call(
        functools.partial(_gla_kernel, tt=tt, kd=kd, vd=vd, heads=heads),
        grid=(b, s // tt),
        in_specs=[tok, _resident((1, d)), vec, vec, vec, _resident(w_in_p.shape), _resident(gw2_p.shape),
                  _resident((1, kd)), _resident((1, dv)), _resident(w_out.shape)],
        out_specs=tok,
        out_shape=jax.ShapeDtypeStruct(x.shape, F32),
        scratch_shapes=[pltpu.VMEM((tt, cols), F32), pltpu.VMEM((tt, vd), F32),
                        pltpu.VMEM((heads, dv, kd // heads), F32)],
        compiler_params=_params(),
        name="gla_mixer",
    )(x, nw.reshape(1, d), sh, sc, g, w_in_p, gw2_p, gate_b.reshape(1, kd), onw.reshape(1, dv), w_out)


def kernel(x, c, ada_w, ada_b, norm_w, final_norm_w, ffn_w13, ffn_w2, rg_w_in, rg_conv_w, rg_conv_b, rg_gate_w,
           rg_gate_b, rg_lambda, rg_w_out, hg_w_in, hg_lb_logits, hg_norm_w, hg_w_out, gla_w_in, gla_gate_w2,
           gla_gate_b, gla_norm_w, gla_w_out):
    depth = ada_w.shape[0]
    b, s, d = x.shape
    assert s % min(FFN_ROWS, s) == 0 and s % min(MIX_ROWS, s) == 0 and min(MIX_ROWS, s) % GLA_CHUNK == 0
    mods = _mod_call(c, ada_w, ada_b).reshape(depth * N_MOD, b, 1, d)
    bf = lambda w: w.astype(BF16)
    w13b, w2b = bf(ffn_w13), bf(ffn_w2)
    i_rg = i_hg = i_gla = 0
    for l in range(depth):
        sh1, sc1, g1, sh2, sc2, g2, sh3, sc3, g3 = [mods[l * N_MOD + j] for j in range(N_MOD)]
        x = _ffn_call(x, norm_w[l, 0], sh1, sc1, g1, w13b, w2b, l, 0)
        m = l % N_MIXERS
        if m == 0:
            x = _rg_call(x, norm_w[l, 1], sh2, sc2, g2, bf(rg_w_in[i_rg]), rg_conv_w[i_rg], rg_conv_b[i_rg],
                         bf(rg_gate_w[i_rg]), rg_gate_b[i_rg], rg_lambda[i_rg], bf(rg_w_out[i_rg]))
            i_rg += 1
        elif m == 1:
            x = _hg_call(x, norm_w[l, 1], sh2, sc2, g2, bf(hg_w_in[i_hg]), hg_lb_logits, hg_norm_w[i_hg],
                         bf(hg_w_out[i_hg]), l)
            i_hg += 1
        else:
            x = _gla_call(x, norm_w[l, 1], sh2, sc2, g2, bf(gla_w_in[i_gla]), bf(gla_gate_w2[i_gla]),
                          gla_gate_b[i_gla], gla_norm_w[i_gla], bf(gla_w_out[i_gla]))
            i_gla += 1
        x = _ffn_call(x, norm_w[l, 2], sh3, sc3, g3, w13b, w2b, l, 1,
                      final_w=final_norm_w if l == depth - 1 else None)
    return x
```

```python
import functools

import jax
import jax.numpy as jnp
from jax import lax
from jax.experimental import pallas as pl
from jax.experimental.pallas import tpu as pltpu

F32 = jnp.float32
BF16 = jnp.bfloat16

N_MIXERS = 3
N_MOD = 9
EPS = 1e-6
RG_HEADS = 8
CONV_WIDTH = 4
RG_C = 8.0
HG_EXPAND = 128
GLA_HEADS = 4
GLA_RANK = 16
GLA_LOGIT_NORM = 16.0
LOG2E = 1.4426950408889634

LANES = 128
SUBLANES = 8
MXU_COLS = 256
VMEM_LIMIT_BYTES = 56 * 1024 * 1024

FFN_ROWS = 1024
FFN_SUB = 512
FFN_COL_CHUNKS = 2
MIX_ROWS = 512
RG_STEPS = 128
RG_SUB = 32
GLA_CHUNK = 64


def _params():
    return pltpu.CompilerParams(dimension_semantics=("arbitrary", "arbitrary"),
                                vmem_limit_bytes=VMEM_LIMIT_BYTES)


def _resident(shape):
    zeros = (0,) * len(shape)
    return pl.BlockSpec(shape, lambda *_: zeros, pipeline_mode=pl.Buffered(1))


def _mod_specs(d, base, seqs):
    return [pl.BlockSpec((None, seqs, 1, d), lambda i, t, row=base + j: (row, i, 0, 0)) for j in range(3)]


def _dot(a, b):
    return jnp.dot(a, b, preferred_element_type=F32)


def _dot_nt(a, b):
    return lax.dot_general(a, b, (((1,), (1,)), ((), ())), preferred_element_type=F32)


def _dot_tn(a, b):
    return lax.dot_general(a, b, (((0,), (0,)), ((), ())), preferred_element_type=F32)


def _neg_abs(x):
    bits = lax.bitcast_convert_type(x, jnp.uint32) | jnp.uint32(0x80000000)
    return lax.bitcast_convert_type(bits, F32)


def _sigmoid(x):
    return 1.0 / (1.0 + jnp.exp(-x))


def _softplus(x):
    return jnp.maximum(x, 0.0) + jnp.log1p(jnp.exp(-jnp.abs(x)))


def _rmsnorm(x, w):
    ms = jnp.mean(x * x, axis=-1, keepdims=True)
    return (x * lax.rsqrt(ms + EPS)) * w


def _norm_mod(x, nw, shift, scale):
    return _rmsnorm(x, nw) * (1.0 + scale) + shift


def _mod_kernel(c_ref, w_ref, b_ref, o_ref):
    c = c_ref[...]
    ca = c * _sigmoid(c)
    w = w_ref[0]
    c_hi = ca.astype(BF16)
    c_lo = (ca - c_hi.astype(F32)).astype(BF16)
    w_hi = w.astype(BF16)
    w_lo = (w - w_hi.astype(F32)).astype(BF16)
    acc = _dot(c_hi, w_hi) + (_dot(c_hi, w_lo) + _dot(c_lo, w_hi))
    o_ref[0] = acc + b_ref[0]


def _mod_call(c, ada_w, ada_b):
    depth, d, _ = ada_w.shape
    b = c.shape[0]
    return pl.pallas_call(
        _mod_kernel,
        grid=(depth, N_MOD),
        in_specs=[
            pl.BlockSpec((b, d), lambda l, j: (0, 0)),
            pl.BlockSpec((1, d, d), lambda l, j: (l, 0, j)),
            pl.BlockSpec((1, 1, d), lambda l, j: (l, 0, j)),
        ],
        out_specs=pl.BlockSpec((1, b, d), lambda l, j: (l * N_MOD + j, 0, 0)),
        out_shape=jax.ShapeDtypeStruct((depth * N_MOD, b, d), F32),
        compiler_params=_params(),
        name="adaln_table",
    )(c, ada_w, ada_b.reshape(depth, 1, N_MOD * d))


def _ffn_kernel(x_ref, nw_ref, sh_ref, sc_ref, g_ref, w13_ref, w2_ref, *rest, d_ff, tm, sub, final):
    if final:
        fw_ref, o_ref = rest
    else:
        (o_ref,) = rest
    nw, sh, sc = nw_ref[...], sh_ref[0], sc_ref[0]
    half_g = 0.5 * g_ref[0]
    tiles = -(-d_ff // MXU_COLS)
    per = -(-tiles // FFN_COL_CHUNKS)
    bounds = [min(c * per * MXU_COLS, d_ff) for c in range(FFN_COL_CHUNKS + 1)]
    for r0 in range(0, tm, sub):
        x = x_ref[0, r0:r0 + sub, :]
        h = _norm_mod(x, nw, sh, sc).astype(BF16)
        y = None
        for c0, c1 in zip(bounds[:-1], bounds[1:]):
            gate = _dot(h, w13_ref[:, c0:c1])
            up = _dot(h, w13_ref[:, d_ff + c0:d_ff + c1])
            act = (gate * _sigmoid(gate) * up).astype(BF16)
            part = _dot(act, w2_ref[c0:c1, :])
            y = part if y is None else y + part
        out = x + half_g * y
        if final:
            out = _rmsnorm(out, fw_ref[...])
        o_ref[0, r0:r0 + sub, :] = out


def _ffn_call(x, nw, mods, base, w13, w2, layer, which, final_w=None):
    b, s, d = x.shape
    d_ff = w2.shape[2]
    tm = min(FFN_ROWS, s)
    sub = min(FFN_SUB, tm)
    assert tm % sub == 0 and d_ff % LANES == 0
    tok = pl.BlockSpec((1, tm, d), lambda i, t: (i, t, 0))
    vecs = _mod_specs(d, base, 1)
    pick = lambda i, t: (layer, which, 0, 0)
    w13_spec = pl.BlockSpec((None, None, d, 2 * d_ff), pick, pipeline_mode=pl.Buffered(1))
    w2_spec = pl.BlockSpec((None, None, d_ff, d), pick, pipeline_mode=pl.Buffered(1))
    in_specs = [tok, _resident((1, d)), *vecs, w13_spec, w2_spec]
    args = [x, nw.reshape(1, d), mods, mods, mods, w13, w2]
    if final_w is not None:
        in_specs.append(_resident((1, d)))
        args.append(final_w.reshape(1, d))
    return pl.pallas_call(
        functools.partial(_ffn_kernel, d_ff=d_ff, tm=tm, sub=sub, final=final_w is not None),
        grid=(b, s // tm),
        in_specs=in_specs,
        out_specs=tok,
        out_shape=jax.ShapeDtypeStruct(x.shape, F32),
        compiler_params=_params(),
        name="swiglu_final" if final_w is not None else "swiglu",
    )(*args)


def _gelu_tanh(x):
    return x * (0.5 * (1.0 + jnp.tanh(0.7978845608028654 * (x + 0.044715 * (x * x * x)))))


def _rg_kernel(x_ref, nw_ref, sh_ref, sc_ref, g_ref, perm_ref, permt_ref, win_ref, cw_ref, cb_ref, gw_ref, gb_ref,
               lam_ref, wout_ref, o_ref, conv_scr, h_scr, *, nb, ts, sub, width):
    t = pl.program_id(1)
    rows = nb * sub
    d = x_ref.shape[2]
    blk = width // RG_HEADS
    tail_rows = (CONV_WIDTH - 1) * nb

    @pl.when(t == 0)
    def _():
        conv_scr[...] = jnp.zeros_like(conv_scr)
        h_scr[...] = jnp.zeros_like(h_scr)

    nw, sh, sc, g = nw_ref[...], sh_ref[...], sc_ref[...], g_ref[...]
    cw, cb = cw_ref[...], cb_ref[...]
    decay = -RG_C * _softplus(-lam_ref[...])
    first_rows = lax.broadcasted_iota(jnp.int32, (rows, 1), 0) < nb
    nsub = ts // sub
    st = {"tail": conv_scr[...], "h": h_scr[...]}
    xs, ys, xbs, xcs, gxs, gas = {}, {}, {}, {}, {}, {}

    def project(si):
        x = x_ref[:, si * sub:(si + 1) * sub, :]
        h = _norm_mod(x, nw, sh, sc).reshape(rows, d).astype(BF16)
        hp = _dot(perm_ref[...], h).astype(BF16)
        yx = _dot(hp, win_ref[...])
        xs[si] = x
        ys[si] = _gelu_tanh(yx[:, :width])
        xbs[si] = yx[:, width:]

    def conv_gates(si):
        xb = xbs.pop(si)
        ext = jnp.concatenate([st["tail"], xb], axis=0)
        xc = cb
        for j in range(CONV_WIDTH):
            xc = xc + ext[j * nb:j * nb + rows] * cw[j:j + 1]
        st["tail"] = xb[rows - tail_rows:rows]
        xcb = xc.astype(BF16)
        gx, ga = [], []
        for hd in range(RG_HEADS):
            gts = _dot(xcb[:, hd * blk:(hd + 1) * blk], gw_ref[hd]) + gb_ref[hd]
            gx.append(gts[:, :blk])
            ga.append(gts[:, blk:])
        xcs[si] = xc
        gxs[si] = jnp.concatenate(gx, axis=1)
        gas[si] = jnp.concatenate(ga, axis=1)

    def recur_out(si):
        log_a = _sigmoid(gas.pop(si)) * decay
        a = jnp.exp(log_a)
        m2 = -jnp.tanh(log_a) * (a * a + 1.0)
        mult = jnp.where(m2 > 0.0, m2 * lax.rsqrt(m2), 0.0)
        if si == 0:
            mult = jnp.where(jnp.logical_and(first_rows, t == 0), 1.0, mult)
        u = _sigmoid(gxs.pop(si)) * xcs.pop(si) * mult
        hcar = st["h"]
        hs = []
        for ti in range(sub):
            hcar = a[ti * nb:(ti + 1) * nb] * hcar + u[ti * nb:(ti + 1) * nb]
            hs.append(hcar)
        st["h"] = hcar
        z = (jnp.concatenate(hs, axis=0) * ys.pop(si)).astype(BF16)
        zn = _dot(permt_ref[...], z).astype(BF16)
        out = _dot(zn, wout_ref[...]).reshape(nb, sub, d)
        o_ref[:, si * sub:(si + 1) * sub, :] = xs.pop(si) + g * out

    for step in range(nsub + 2):
        if step < nsub:
            project(step)
        if 0 <= step - 1 < nsub:
            conv_gates(step - 1)
        if 0 <= step - 2 < nsub:
            recur_out(step - 2)
    conv_scr[...] = st["tail"]
    h_scr[...] = st["h"]


def _rg_call(x, nw, mods, base, w_in, conv_w, conv_b, gate_w, gate_b, lam, w_out):
    b, s, d = x.shape
    width = w_out.shape[0]
    blk = width // RG_HEADS
    nb = SUBLANES
    ts = min(RG_STEPS, s)
    sub = min(RG_SUB, ts)
    rows = nb * sub
    assert b % nb == 0 and s % ts == 0 and ts % sub == 0 and sub % SUBLANES == 0
    r = jnp.arange(rows, dtype=jnp.int32)
    perm = ((r[:, None] % nb) * sub + r[:, None] // nb == r[None, :]).astype(BF16)
    tok = pl.BlockSpec((nb, ts, d), lambda i, t: (i, t, 0))
    vecs = _mod_specs(d, base, nb)
    return pl.pallas_call(
        functools.partial(_rg_kernel, nb=nb, ts=ts, sub=sub, width=width),
        grid=(b // nb, s // ts),
        in_specs=[tok, _resident((1, d)), *vecs, _resident((rows, rows)), _resident((rows, rows)),
                  _resident(w_in.shape), _resident((CONV_WIDTH, width)), _resident((1, width)),
                  _resident(gate_w.shape), _resident((RG_HEADS, 1, 2 * blk)), _resident((1, width)),
                  _resident(w_out.shape)],
        out_specs=tok,
        out_shape=jax.ShapeDtypeStruct(x.shape, F32),
        scratch_shapes=[pltpu.VMEM(((CONV_WIDTH - 1) * nb, width), F32), pltpu.VMEM((nb, width), F32)],
        compiler_params=_params(),
        name="rglru_mixer",
    )(x, nw.reshape(1, d), mods, mods, mods, perm, perm.T, w_in, conv_w, conv_b.reshape(1, width), gate_w,
      gate_b.reshape(RG_HEADS, 1, 2 * blk), lam.reshape(1, width), w_out)


def _cumsum_rows(v, rows):
    n = v.shape[1]
    groups = rows // SUBLANES
    v3 = v.reshape(groups, SUBLANES, n)
    srow = lax.broadcasted_iota(jnp.int32, v3.shape, 1)
    for k in (1, 2, 4):
        v3 = v3 + jnp.where(srow >= k, pltpu.roll(v3, k, 1), 0.0)
    out, carry = [], None
    for gi in range(groups):
        blk = v3[gi] if carry is None else v3[gi] + carry
        out.append(blk)
        carry = blk[SUBLANES - 1:SUBLANES, :]
    return jnp.concatenate(out, axis=0)


def _anchor(b, half, rows):
    n = b.shape[1]
    if 2 * half >= SUBLANES:
        v = b.reshape(rows // (2 * half), 2 * half, n)
        return jnp.broadcast_to(v[:, half - 1:half, :], v.shape).reshape(rows, n)
    v = b.reshape(rows // SUBLANES, SUBLANES, n)
    srow = lax.broadcasted_iota(jnp.int32, v.shape, 1)
    if half == 2:
        lo = jnp.broadcast_to(v[:, 1:2, :], v.shape)
        hi = jnp.broadcast_to(v[:, 5:6, :], v.shape)
        return jnp.where(srow < 4, lo, hi).reshape(rows, n)
    assert half == 1
    return jnp.where((srow & 1) == 1, pltpu.roll(v, 1, 1), v).reshape(rows, n)


def _block_diag(a, b):
    top = jnp.concatenate([a, jnp.zeros_like(b)], axis=1)
    bot = jnp.concatenate([jnp.zeros_like(a), b], axis=1)
    return jnp.concatenate([top, bot], axis=0)


def _gla_masks(rows):
    qi = lax.broadcasted_iota(jnp.int32, (rows, 2 * rows), 0)
    kj = lax.broadcasted_iota(jnp.int32, (rows, 2 * rows), 1) & (rows - 1)
    masks = []
    half, lvl = 1, 0
    while half < rows:
        same = (qi >> (lvl + 1)) == (kj >> (lvl + 1))
        masks.append(same & ((qi & half) != 0) & ((kj & half) == 0))
        half, lvl = half * 2, lvl + 1
    return qi == kj, masks


def _gla_prep(q, k, v, lf2, *, rows):
    b = _cumsum_rows(lf2, rows)
    b_last = b[rows - 1:rows, :]
    qe = (q * jnp.exp2(b)).astype(BF16)
    kd = (k * jnp.exp2(b_last - b)).astype(BF16)
    eb = jnp.exp2(b_last)
    vb = v.astype(BF16)
    rown = lax.broadcasted_iota(jnp.int32, b.shape, 0)
    mixed = []
    half = 1
    while half < rows:
        if half >= SUBLANES:
            parts = []
            for r0 in range(0, rows, half):
                if r0 & half:
                    parts.append(q[r0:r0 + half] * jnp.exp2(b[r0:r0 + half] - b[r0 - 1:r0]))
                else:
                    parts.append(k[r0:r0 + half] * jnp.exp2(b[r0 + half - 1:r0 + half] - b[r0:r0 + half]))
            mixed.append(jnp.concatenate(parts, axis=0).astype(BF16))
        else:
            e = jnp.exp2(_neg_abs(b - _anchor(b, half, rows)))
            mixed.append((jnp.where((rown & half) != 0, q, k) * e).astype(BF16))
        half *= 2
    return dict(qe=qe, kd=kd, eb=eb, vb=vb, mixed=mixed, qk=q * k)


def _gla_scores(ctx, st_ref, eye, masks, *, heads, dk, dv, rows):
    qe, kd, eb, vb, mixed, qk = (ctx[n] for n in ("qe", "kd", "eb", "vb", "mixed", "qk"))
    ks = [slice(hd * dk, (hd + 1) * dk) for hd in range(heads)]
    vs = [slice(hd * dv, (hd + 1) * dv) for hd in range(heads)]
    inter = []
    for hd in range(heads):
        st = st_ref[hd]
        inter.append(_dot_nt(qe[:, ks[hd]], st.astype(BF16)))
        st_ref[hd] = st * eb[:, ks[hd]] + _dot_tn(vb[:, vs[hd]], kd[:, ks[hd]])
    left = lax.broadcasted_iota(jnp.int32, (rows, 2 * rows), 1) < rows
    scores = []
    for h0 in range(0, heads, 2):
        kc = slice(h0 * dk, (h0 + 2) * dk)
        diag = jnp.where(left, jnp.sum(qk[:, ks[h0]], axis=-1, keepdims=True),
                         jnp.sum(qk[:, ks[h0 + 1]], axis=-1, keepdims=True))
        sc = jnp.where(eye, diag, 0.0)
        for m, mk in zip(mixed, masks):
            sc = jnp.where(mk, _dot_nt(m[:, kc], _block_diag(m[:, ks[h0]], m[:, ks[h0 + 1]])), sc)
        scores.append(sc.astype(BF16))
    ctx["inter"] = inter
    ctx["scores"] = scores


def _gla_out(ctx, *, heads, dv):
    vb, inter, scores = ctx["vb"], ctx["inter"], ctx["scores"]
    vs = [slice(hd * dv, (hd + 1) * dv) for hd in range(heads)]
    outs = []
    for h0 in range(0, heads, 2):
        o2 = _dot(scores[h0 // 2], _block_diag(vb[:, vs[h0]], vb[:, vs[h0 + 1]]))
        outs.append(o2 + jnp.concatenate([inter[h0], inter[h0 + 1]], axis=1))
    return jnp.concatenate(outs, axis=1)


def _gla_tile(n_chunks, load, store, st_ref, *, heads, dk, dv, rows):
    eye, masks = _gla_masks(rows)
    ctxs = {}
    for step in range(n_chunks + 2):
        if step < n_chunks:
            ctxs[step] = _gla_prep(*load(step), rows=rows)
        if 0 <= step - 1 < n_chunks:
            _gla_scores(ctxs[step - 1], st_ref, eye, masks, heads=heads, dk=dk, dv=dv, rows=rows)
        if 0 <= step - 2 < n_chunks:
            store(step - 2, _gla_out(ctxs.pop(step - 2), heads=heads, dv=dv))


def _hg_kernel(x_ref, nw_ref, sh_ref, sc_ref, g_ref, win_ref, lbl_ref, onw_ref, wout_ref, o_ref,
               p_scr, lf_scr, o_scr, st_scr, *, tt, d, layer, heads):
    t = pl.program_id(1)

    @pl.when(t == 0)
    def _():
        st_scr[...] = jnp.zeros_like(st_scr)

    x = x_ref[0]
    h = _norm_mod(x, nw_ref[...], sh_ref[0], sc_ref[0]).astype(BF16)
    p_scr[:, d:2 * d] = _dot(h, win_ref[:, d:2 * d])

    lg = lbl_ref[...]
    rows_l = [lg[i:i + 1, :] for i in range(lg.shape[0])]
    mx = functools.reduce(jnp.maximum, rows_l)
    ex = [jnp.exp(r - mx) for r in rows_l]
    den = functools.reduce(lambda u, w: u + w, ex)
    lb = jnp.zeros_like(den)
    for i in range(1, layer + 1):
        lb = lb + ex[i] / den
    log_lb = jnp.log(lb)
    log_1m = jnp.log1p(-lb)
    one_m = 1.0 - lb

    log_lb2 = log_lb * LOG2E
    log_1m2 = log_1m * LOG2E

    fz = p_scr[:, d:2 * d]
    fz2 = fz * LOG2E
    e = jnp.exp2(_neg_abs(fz2))
    u = 1.0 + e
    bv2 = (log_1m2 + jnp.minimum(fz2, 0.0)) - jnp.log2(u)
    lf_scr[...] = jnp.maximum(log_lb2, bv2) + jnp.log2(1.0 + jnp.exp2(_neg_abs(log_lb2 - bv2)))
    p_scr[:, d:2 * d] = one_m * (jnp.where(fz >= 0.0, e, 1.0) / u)
    p_scr[:, 0:d] = _dot(h, win_ref[:, 0:d])
    p_scr[:, 2 * d:4 * d] = _dot(h, win_ref[:, 2 * d:4 * d])

    def load(ci):
        rs = slice(ci * GLA_CHUNK, (ci + 1) * GLA_CHUNK)
        return p_scr[rs, 0:d], p_scr[rs, d:2 * d], p_scr[rs, 2 * d:3 * d], lf_scr[rs, :]

    def store(ci, o):
        o_scr[ci * GLA_CHUNK:(ci + 1) * GLA_CHUNK, :] = o

    _gla_tile(tt // GLA_CHUNK, load, store, st_scr, heads=heads, dk=d // heads, dv=d // heads, rows=GLA_CHUNK)

    o = _rmsnorm(o_scr[...], onw_ref[...]) * _sigmoid(p_scr[:, 3 * d:4 * d])
    out = _dot(o.astype(BF16), wout_ref[...])
    o_ref[0] = x + g_ref[0] * out


def _hg_call(x, nw, mods, base, w_in, lb_logits, onw, w_out, layer):
    b, s, d = x.shape
    heads = d // HG_EXPAND
    tt = min(MIX_ROWS, s)
    tok = pl.BlockSpec((1, tt, d), lambda i, t: (i, t, 0))
    vecs = _mod_specs(d, base, 1)
    return pl.pallas_call(
        functools.partial(_hg_kernel, tt=tt, d=d, layer=layer, heads=heads),
        grid=(b, s // tt),
        in_specs=[tok, _resident((1, d)), *vecs, _resident(w_in.shape), _resident(lb_logits.shape),
                  _resident((1, d)), _resident(w_out.shape)],
        out_specs=tok,
        out_shape=jax.ShapeDtypeStruct(x.shape, F32),
        scratch_shapes=[pltpu.VMEM((tt, 4 * d), F32), pltpu.VMEM((tt, d), F32), pltpu.VMEM((tt, d), F32),
                        pltpu.VMEM((heads, HG_EXPAND, HG_EXPAND), F32)],
        compiler_params=_params(),
        name="hgrn2_mixer",
    )(x, nw.reshape(1, d), mods, mods, mods, w_in, lb_logits, onw.reshape(1, d), w_out)


def _gla_kernel(x_ref, nw_ref, sh_ref, sc_ref, g_ref, win_ref, gw2_ref, gb_ref, onw_ref, wout_ref, o_ref,
                p_scr, o_scr, st_scr, *, tt, kd, vd, heads):
    t = pl.program_id(1)

    @pl.when(t == 0)
    def _():
        st_scr[...] = jnp.zeros_like(st_scr)

    x = x_ref[0]
    h = _norm_mod(x, nw_ref[...], sh_ref[0], sc_ref[0]).astype(BF16)
    p_scr[...] = _dot(h, win_ref[...])
    dk = kd // heads
    dv = vd // heads
    scale = dk ** -0.5
    low0 = 2 * kd + 2 * vd

    def load(ci):
        rs = slice(ci * GLA_CHUNK, (ci + 1) * GLA_CHUNK)
        q = p_scr[rs, 0:kd] * scale
        k = p_scr[rs, kd:2 * kd]
        v = p_scr[rs, 2 * kd:2 * kd + vd]
        a_low = p_scr[rs, low0:low0 + LANES].astype(BF16)
        z2 = (_dot(a_low, gw2_ref[...]) + gb_ref[...]) * LOG2E
        lf2 = (jnp.minimum(z2, 0.0) - jnp.log2(1.0 + jnp.exp2(_neg_abs(z2)))) * (1.0 / GLA_LOGIT_NORM)
        return q, k, v, lf2

    def store(ci, o):
        o_scr[ci * GLA_CHUNK:(ci + 1) * GLA_CHUNK, :] = o

    _gla_tile(tt // GLA_CHUNK, load, store, st_scr, heads=heads, dk=dk, dv=dv, rows=GLA_CHUNK)

    gate = p_scr[:, 2 * kd + vd:2 * kd + 2 * vd]
    onw = onw_ref[...]
    parts = []
    for hd in range(heads):
        vs = slice(hd * dv, (hd + 1) * dv)
        gh = gate[:, vs]
        parts.append(_rmsnorm(o_scr[:, vs], onw) * (gh * _sigmoid(gh)))
    out = _dot(jnp.concatenate(parts, axis=1).astype(BF16), wout_ref[...])
    o_ref[0] = x + g_ref[0] * out


def _gla_call(x, nw, mods, base, w_in, gate_w2, gate_b, onw, w_out):
    b, s, d = x.shape
    kd = gate_w2.shape[1]
    vd = w_out.shape[0]
    heads = GLA_HEADS
    dv = vd // heads
    tt = min(MIX_ROWS, s)
    cols = 2 * kd + 2 * vd + LANES
    w_in_p = jnp.zeros((d, cols), BF16).at[:, :w_in.shape[1]].set(w_in)
    gw2_p = jnp.zeros((LANES, kd), BF16).at[:GLA_RANK, :].set(gate_w2)
    tok = pl.BlockSpec((1, tt, d), lambda i, t: (i, t, 0))
    vecs = _mod_specs(d, base, 1)
    return pl.pallas_call(
        functools.partial(_gla_kernel, tt=tt, kd=kd, vd=vd, heads=heads),
        grid=(b, s // tt),
        in_specs=[tok, _resident((1, d)), *vecs, _resident(w_in_p.shape), _resident(gw2_p.shape),
                  _resident((1, kd)), _resident((1, dv)), _resident(w_out.shape)],
        out_specs=tok,
        out_shape=jax.ShapeDtypeStruct(x.shape, F32),
        scratch_shapes=[pltpu.VMEM((tt, cols), F32), pltpu.VMEM((tt, vd), F32),
                        pltpu.VMEM((heads, dv, kd // heads), F32)],
        compiler_params=_params(),
        name="gla_mixer",
    )(x, nw.reshape(1, d), mods, mods, mods, w_in_p, gw2_p, gate_b.reshape(1, kd), onw.reshape(1, dv), w_out)


def kernel(x, c, ada_w, ada_b, norm_w, final_norm_w, ffn_w13, ffn_w2, rg_w_in, rg_conv_w, rg_conv_b, rg_gate_w,
           rg_gate_b, rg_lambda, rg_w_out, hg_w_in, hg_lb_logits, hg_norm_w, hg_w_out, gla_w_in, gla_gate_w2,
           gla_gate_b, gla_norm_w, gla_w_out):
    depth = ada_w.shape[0]
    b, s, d = x.shape
    assert s % min(FFN_ROWS, s) == 0 and s % min(MIX_ROWS, s) == 0 and min(MIX_ROWS, s) % GLA_CHUNK == 0
    mods = _mod_call(c, ada_w, ada_b).reshape(depth * N_MOD, b, 1, d)
    bf = lambda w: w.astype(BF16)
    w13b, w2b = bf(ffn_w13), bf(ffn_w2)
    i_rg = i_hg = i_gla = 0
    for l in range(depth):
        base = l * N_MOD
        x = _ffn_call(x, norm_w[l, 0], mods, base, w13b, w2b, l, 0)
        m = l % N_MIXERS
        if m == 0:
            x = _rg_call(x, norm_w[l, 1], mods, base + 3, bf(rg_w_in[i_rg]), rg_conv_w[i_rg], rg_conv_b[i_rg],
                         bf(rg_gate_w[i_rg]), rg_gate_b[i_rg], rg_lambda[i_rg], bf(rg_w_out[i_rg]))
            i_rg += 1
        elif m == 1:
            x = _hg_call(x, norm_w[l, 1], mods, base + 3, bf(hg_w_in[i_hg]), hg_lb_logits, hg_norm_w[i_hg],
                         bf(hg_w_out[i_hg]), l)
            i_hg += 1
        else:
            x = _gla_call(x, norm_w[l, 1], mods, base + 3, bf(gla_w_in[i_gla]), bf(gla_gate_w2[i_gla]),
                          gla_gate_b[i_gla], gla_norm_w[i_gla], bf(gla_w_out[i_gla]))
            i_gla += 1
        x = _ffn_call(x, norm_w[l, 2], mods, base + 6, w13b, w2b, l, 1,
                      final_w=final_norm_w if l == depth - 1 else None)
    return x
```

```python
import functools

import jax
import jax.numpy as jnp
from jax import lax
from jax.experimental import pallas as pl
from jax.experimental.pallas import tpu as pltpu

F32 = jnp.float32
BF16 = jnp.bfloat16

N_MIXERS = 3
N_MOD = 9
EPS = 1e-6
RG_HEADS = 8
CONV_WIDTH = 4
RG_C = 8.0
HG_EXPAND = 128
GLA_HEADS = 4
GLA_RANK = 16
GLA_LOGIT_NORM = 16.0
LOG2E = 1.4426950408889634

LANES = 128
SUBLANES = 8
MXU_COLS = 256
VMEM_LIMIT_BYTES = 56 * 1024 * 1024

FFN_ROWS = 2048
FFN_VMEM_LIMIT_BYTES = 62 * 1024 * 1024
FFN_SUB = 512
FFN_COL_CHUNKS = 2
MIX_ROWS = 512
RG_STEPS = 128
RG_SUB = 32
GLA_CHUNK = 64


def _params(vmem_limit_bytes=VMEM_LIMIT_BYTES):
    return pltpu.CompilerParams(dimension_semantics=("arbitrary", "arbitrary"),
                                vmem_limit_bytes=vmem_limit_bytes)


def _resident(shape):
    zeros = (0,) * len(shape)
    return pl.BlockSpec(shape, lambda *_: zeros, pipeline_mode=pl.Buffered(1))


def _mod_specs(d, base, seqs):
    return [pl.BlockSpec((None, seqs, 1, d), lambda i, t, row=base + j: (row, i, 0, 0)) for j in range(3)]


def _dot(a, b):
    return jnp.dot(a, b, preferred_element_type=F32)


def _dot_nt(a, b):
    return lax.dot_general(a, b, (((1,), (1,)), ((), ())), preferred_element_type=F32)


def _dot_tn(a, b):
    return lax.dot_general(a, b, (((0,), (0,)), ((), ())), preferred_element_type=F32)


def _neg_abs(x):
    bits = lax.bitcast_convert_type(x, jnp.uint32) | jnp.uint32(0x80000000)
    return lax.bitcast_convert_type(bits, F32)


def _sigmoid(x):
    return 1.0 / (1.0 + jnp.exp(-x))


def _softplus(x):
    return jnp.maximum(x, 0.0) + jnp.log1p(jnp.exp(-jnp.abs(x)))


def _rmsnorm(x, w):
    ms = jnp.mean(x * x, axis=-1, keepdims=True)
    return (x * lax.rsqrt(ms + EPS)) * w


def _norm_mod(x, nw, shift, scale):
    return _rmsnorm(x, nw) * (1.0 + scale) + shift


def _mod_kernel(c_ref, w_ref, b_ref, o_ref):
    c = c_ref[...]
    ca = c * _sigmoid(c)
    w = w_ref[0]
    c_hi = ca.astype(BF16)
    c_lo = (ca - c_hi.astype(F32)).astype(BF16)
    w_hi = w.astype(BF16)
    w_lo = (w - w_hi.astype(F32)).astype(BF16)
    acc = _dot(c_hi, w_hi) + (_dot(c_hi, w_lo) + _dot(c_lo, w_hi))
    o_ref[0] = acc + b_ref[0]


def _mod_call(c, ada_w, ada_b):
    depth, d, _ = ada_w.shape
    b = c.shape[0]
    return pl.pallas_call(
        _mod_kernel,
        grid=(depth, N_MOD),
        in_specs=[
            pl.BlockSpec((b, d), lambda l, j: (0, 0)),
            pl.BlockSpec((1, d, d), lambda l, j: (l, 0, j)),
            pl.BlockSpec((1, 1, d), lambda l, j: (l, 0, j)),
        ],
        out_specs=pl.BlockSpec((1, b, d), lambda l, j: (l * N_MOD + j, 0, 0)),
        out_shape=jax.ShapeDtypeStruct((depth * N_MOD, b, d), F32),
        compiler_params=_params(),
        name="adaln_table",
    )(c, ada_w, ada_b.reshape(depth, 1, N_MOD * d))


def _ffn_kernel(x_ref, nw_ref, sh_ref, sc_ref, g_ref, w13_ref, w2_ref, *rest, d_ff, tm, sub, final):
    if final:
        fw_ref, o_ref = rest
    else:
        (o_ref,) = rest
    nw, sh, sc = nw_ref[...], sh_ref[0], sc_ref[0]
    half_g = 0.5 * g_ref[0]
    tiles = -(-d_ff // MXU_COLS)
    per = -(-tiles // FFN_COL_CHUNKS)
    bounds = [min(c * per * MXU_COLS, d_ff) for c in range(FFN_COL_CHUNKS + 1)]
    for r0 in range(0, tm, sub):
        x = x_ref[0, r0:r0 + sub, :]
        h = _norm_mod(x, nw, sh, sc).astype(BF16)
        y = None
        for c0, c1 in zip(bounds[:-1], bounds[1:]):
            gate = _dot(h, w13_ref[:, c0:c1])
            up = _dot(h, w13_ref[:, d_ff + c0:d_ff + c1])
            act = (gate * _sigmoid(gate) * up).astype(BF16)
            part = _dot(act, w2_ref[c0:c1, :])
            y = part if y is None else y + part
        out = x + half_g * y
        if final:
            out = _rmsnorm(out, fw_ref[...])
        o_ref[0, r0:r0 + sub, :] = out


def _ffn_call(x, nw, mods, base, w13, w2, layer, which, final_w=None):
    b, s, d = x.shape
    d_ff = w2.shape[2]
    tm = min(FFN_ROWS, s)
    sub = min(FFN_SUB, tm)
    assert tm % sub == 0 and d_ff % LANES == 0
    tok = pl.BlockSpec((1, tm, d), lambda i, t: (i, t, 0))
    vecs = _mod_specs(d, base, 1)
    pick = lambda i, t: (layer, which, 0, 0)
    w13_spec = pl.BlockSpec((None, None, d, 2 * d_ff), pick, pipeline_mode=pl.Buffered(1))
    w2_spec = pl.BlockSpec((None, None, d_ff, d), pick, pipeline_mode=pl.Buffered(1))
    in_specs = [tok, _resident((1, d)), *vecs, w13_spec, w2_spec]
    args = [x, nw.reshape(1, d), mods, mods, mods, w13, w2]
    if final_w is not None:
        in_specs.append(_resident((1, d)))
        args.append(final_w.reshape(1, d))
    return pl.pallas_call(
        functools.partial(_ffn_kernel, d_ff=d_ff, tm=tm, sub=sub, final=final_w is not None),
        grid=(b, s // tm),
        in_specs=in_specs,
        out_specs=tok,
        out_shape=jax.ShapeDtypeStruct(x.shape, F32),
        compiler_params=_params(FFN_VMEM_LIMIT_BYTES),
        name="swiglu_final" if final_w is not None else "swiglu",
    )(*args)


def _gelu_tanh(x):
    return x * (0.5 * (1.0 + jnp.tanh(0.7978845608028654 * (x + 0.044715 * (x * x * x)))))


def _rg_kernel(x_ref, nw_ref, sh_ref, sc_ref, g_ref, perm_ref, permt_ref, win_ref, cw_ref, cb_ref, gw_ref, gb_ref,
               lam_ref, wout_ref, o_ref, conv_scr, h_scr, *, nb, ts, sub, width):
    t = pl.program_id(1)
    rows = nb * sub
    d = x_ref.shape[2]
    blk = width // RG_HEADS
    tail_rows = (CONV_WIDTH - 1) * nb

    @pl.when(t == 0)
    def _():
        conv_scr[...] = jnp.zeros_like(conv_scr)
        h_scr[...] = jnp.zeros_like(h_scr)

    nw, sh, sc, g = nw_ref[...], sh_ref[...], sc_ref[...], g_ref[...]
    cw, cb = cw_ref[...], cb_ref[...]
    decay = -RG_C * _softplus(-lam_ref[...])
    first_rows = lax.broadcasted_iota(jnp.int32, (rows, 1), 0) < nb
    nsub = ts // sub
    st = {"tail": conv_scr[...], "h": h_scr[...]}
    xs, ys, xbs, xcs, gxs, gas = {}, {}, {}, {}, {}, {}

    def project(si):
        x = x_ref[:, si * sub:(si + 1) * sub, :]
        h = _norm_mod(x, nw, sh, sc).reshape(rows, d).astype(BF16)
        hp = _dot(perm_ref[...], h).astype(BF16)
        yx = _dot(hp, win_ref[...])
        xs[si] = x
        ys[si] = _gelu_tanh(yx[:, :width])
        xbs[si] = yx[:, width:]

    def conv_gates(si):
        xb = xbs.pop(si)
        ext = jnp.concatenate([st["tail"], xb], axis=0)
        xc = cb
        for j in range(CONV_WIDTH):
            xc = xc + ext[j * nb:j * nb + rows] * cw[j:j + 1]
        st["tail"] = xb[rows - tail_rows:rows]
        xcb = xc.astype(BF16)
        gx, ga = [], []
        for hd in range(RG_HEADS):
            gts = _dot(xcb[:, hd * blk:(hd + 1) * blk], gw_ref[hd]) + gb_ref[hd]
            gx.append(gts[:, :blk])
            ga.append(gts[:, blk:])
        xcs[si] = xc
        gxs[si] = jnp.concatenate(gx, axis=1)
        gas[si] = jnp.concatenate(ga, axis=1)

    def recur_out(si):
        log_a = _sigmoid(gas.pop(si)) * decay
        a = jnp.exp(log_a)
        m2 = -jnp.tanh(log_a) * (a * a + 1.0)
        mult = jnp.where(m2 > 0.0, m2 * lax.rsqrt(m2), 0.0)
        if si == 0:
            mult = jnp.where(jnp.logical_and(first_rows, t == 0), 1.0, mult)
        u = _sigmoid(gxs.pop(si)) * xcs.pop(si) * mult
        hcar = st["h"]
        hs = []
        for ti in range(sub):
            hcar = a[ti * nb:(ti + 1) * nb] * hcar + u[ti * nb:(ti + 1) * nb]
            hs.append(hcar)
        st["h"] = hcar
        z = (jnp.concatenate(hs, axis=0) * ys.pop(si)).astype(BF16)
        zn = _dot(permt_ref[...], z).astype(BF16)
        out = _dot(zn, wout_ref[...]).reshape(nb, sub, d)
        o_ref[:, si * sub:(si + 1) * sub, :] = xs.pop(si) + g * out

    for step in range(nsub + 2):
        if step < nsub:
            project(step)
        if 0 <= step - 1 < nsub:
            conv_gates(step - 1)
        if 0 <= step - 2 < nsub:
            recur_out(step - 2)
    conv_scr[...] = st["tail"]
    h_scr[...] = st["h"]


def _rg_call(x, nw, mods, base, w_in, conv_w, conv_b, gate_w, gate_b, lam, w_out):
    b, s, d = x.shape
    width = w_out.shape[0]
    blk = width // RG_HEADS
    nb = SUBLANES
    ts = min(RG_STEPS, s)
    sub = min(RG_SUB, ts)
    rows = nb * sub
    assert b % nb == 0 and s % ts == 0 and ts % sub == 0 and sub % SUBLANES == 0
    r = jnp.arange(rows, dtype=jnp.int32)
    perm = ((r[:, None] % nb) * sub + r[:, None] // nb == r[None, :]).astype(BF16)
    tok = pl.BlockSpec((nb, ts, d), lambda i, t: (i, t, 0))
    vecs = _mod_specs(d, base, nb)
    return pl.pallas_call(
        functools.partial(_rg_kernel, nb=nb, ts=ts, sub=sub, width=width),
        grid=(b // nb, s // ts),
        in_specs=[tok, _resident((1, d)), *vecs, _resident((rows, rows)), _resident((rows, rows)),
                  _resident(w_in.shape), _resident((CONV_WIDTH, width)), _resident((1, width)),
                  _resident(gate_w.shape), _resident((RG_HEADS, 1, 2 * blk)), _resident((1, width)),
                  _resident(w_out.shape)],
        out_specs=tok,
        out_shape=jax.ShapeDtypeStruct(x.shape, F32),
        scratch_shapes=[pltpu.VMEM(((CONV_WIDTH - 1) * nb, width), F32), pltpu.VMEM((nb, width), F32)],
        compiler_params=_params(),
        name="rglru_mixer",
    )(x, nw.reshape(1, d), mods, mods, mods, perm, perm.T, w_in, conv_w, conv_b.reshape(1, width), gate_w,
      gate_b.reshape(RG_HEADS, 1, 2 * blk), lam.reshape(1, width), w_out)


def _cumsum_rows(v, rows):
    n = v.shape[1]
    groups = rows // SUBLANES
    v3 = v.reshape(groups, SUBLANES, n)
    srow = lax.broadcasted_iota(jnp.int32, v3.shape, 1)
    for k in (1, 2, 4):
        v3 = v3 + jnp.where(srow >= k, pltpu.roll(v3, k, 1), 0.0)
    out, carry = [], None
    for gi in range(groups):
        blk = v3[gi] if carry is None else v3[gi] + carry
        out.append(blk)
        carry = blk[SUBLANES - 1:SUBLANES, :]
    return jnp.concatenate(out, axis=0)


def _anchor(b, half, rows):
    n = b.shape[1]
    if 2 * half >= SUBLANES:
        v = b.reshape(rows // (2 * half), 2 * half, n)
        return jnp.broadcast_to(v[:, half - 1:half, :], v.shape).reshape(rows, n)
    v = b.reshape(rows // SUBLANES, SUBLANES, n)
    srow = lax.broadcasted_iota(jnp.int32, v.shape, 1)
    if half == 2:
        lo = jnp.broadcast_to(v[:, 1:2, :], v.shape)
        hi = jnp.broadcast_to(v[:, 5:6, :], v.shape)
        return jnp.where(srow < 4, lo, hi).reshape(rows, n)
    assert half == 1
    return jnp.where((srow & 1) == 1, pltpu.roll(v, 1, 1), v).reshape(rows, n)


def _block_diag(a, b):
    top = jnp.concatenate([a, jnp.zeros_like(b)], axis=1)
    bot = jnp.concatenate([jnp.zeros_like(a), b], axis=1)
    return jnp.concatenate([top, bot], axis=0)


def _gla_masks(rows):
    qi = lax.broadcasted_iota(jnp.int32, (rows, 2 * rows), 0)
    kj = lax.broadcasted_iota(jnp.int32, (rows, 2 * rows), 1) & (rows - 1)
    masks = []
    half, lvl = 1, 0
    while half < rows:
        same = (qi >> (lvl + 1)) == (kj >> (lvl + 1))
        masks.append(same & ((qi & half) != 0) & ((kj & half) == 0))
        half, lvl = half * 2, lvl + 1
    return qi == kj, masks


def _gla_prep(q, k, v, lf2, *, rows):
    b = _cumsum_rows(lf2, rows)
    b_last = b[rows - 1:rows, :]
    qe = (q * jnp.exp2(b)).astype(BF16)
    kd = (k * jnp.exp2(b_last - b)).astype(BF16)
    eb = jnp.exp2(b_last)
    vb = v.astype(BF16)
    rown = lax.broadcasted_iota(jnp.int32, b.shape, 0)
    mixed = []
    half = 1
    while half < rows:
        if half >= SUBLANES:
            parts = []
            for r0 in range(0, rows, half):
                if r0 & half:
                    parts.append(q[r0:r0 + half] * jnp.exp2(b[r0:r0 + half] - b[r0 - 1:r0]))
                else:
                    parts.append(k[r0:r0 + half] * jnp.exp2(b[r0 + half - 1:r0 + half] - b[r0:r0 + half]))
            mixed.append(jnp.concatenate(parts, axis=0).astype(BF16))
        else:
            e = jnp.exp2(_neg_abs(b - _anchor(b, half, rows)))
            mixed.append((jnp.where((rown & half) != 0, q, k) * e).astype(BF16))
        half *= 2
    return dict(qe=qe, kd=kd, eb=eb, vb=vb, mixed=mixed, qk=q * k)


def _gla_scores(ctx, st_ref, eye, masks, *, heads, dk, dv, rows):
    qe, kd, eb, vb, mixed, qk = (ctx[n] for n in ("qe", "kd", "eb", "vb", "mixed", "qk"))
    ks = [slice(hd * dk, (hd + 1) * dk) for hd in range(heads)]
    vs = [slice(hd * dv, (hd + 1) * dv) for hd in range(heads)]
    inter = []
    for hd in range(heads):
        st = st_ref[hd]
        inter.append(_dot_nt(qe[:, ks[hd]], st.astype(BF16)))
        st_ref[hd] = st * eb[:, ks[hd]] + _dot_tn(vb[:, vs[hd]], kd[:, ks[hd]])
    left = lax.broadcasted_iota(jnp.int32, (rows, 2 * rows), 1) < rows
    scores = []
    for h0 in range(0, heads, 2):
        kc = slice(h0 * dk, (h0 + 2) * dk)
        diag = jnp.where(left, jnp.sum(qk[:, ks[h0]], axis=-1, keepdims=True),
                         jnp.sum(qk[:, ks[h0 + 1]], axis=-1, keepdims=True))
        sc = jnp.where(eye, diag, 0.0)
        for m, mk in zip(mixed, masks):
            sc = jnp.where(mk, _dot_nt(m[:, kc], _block_diag(m[:, ks[h0]], m[:, ks[h0 + 1]])), sc)
        scores.append(sc.astype(BF16))
    ctx["inter"] = inter
    ctx["scores"] = scores


def _gla_out(ctx, *, heads, dv):
    vb, inter, scores = ctx["vb"], ctx["inter"], ctx["scores"]
    vs = [slice(hd * dv, (hd + 1) * dv) for hd in range(heads)]
    outs = []
    for h0 in range(0, heads, 2):
        o2 = _dot(scores[h0 // 2], _block_diag(vb[:, vs[h0]], vb[:, vs[h0 + 1]]))
        outs.append(o2 + jnp.concatenate([inter[h0], inter[h0 + 1]], axis=1))
    return jnp.concatenate(outs, axis=1)


def _gla_tile(n_chunks, load, store, st_ref, *, heads, dk, dv, rows):
    eye, masks = _gla_masks(rows)
    ctxs = {}
    for step in range(n_chunks + 2):
        if step < n_chunks:
            ctxs[step] = _gla_prep(*load(step), rows=rows)
        if 0 <= step - 1 < n_chunks:
            _gla_scores(ctxs[step - 1], st_ref, eye, masks, heads=heads, dk=dk, dv=dv, rows=rows)
        if 0 <= step - 2 < n_chunks:
            store(step - 2, _gla_out(ctxs.pop(step - 2), heads=heads, dv=dv))


def _hg_kernel(x_ref, nw_ref, sh_ref, sc_ref, g_ref, win_ref, lbl_ref, onw_ref, wout_ref, o_ref,
               p_scr, lf_scr, o_scr, st_scr, *, tt, d, layer, heads):
    t = pl.program_id(1)

    @pl.when(t == 0)
    def _():
        st_scr[...] = jnp.zeros_like(st_scr)

    x = x_ref[0]
    h = _norm_mod(x, nw_ref[...], sh_ref[0], sc_ref[0]).astype(BF16)
    p_scr[:, d:2 * d] = _dot(h, win_ref[:, d:2 * d])

    lg = lbl_ref[...]
    rows_l = [lg[i:i + 1, :] for i in range(lg.shape[0])]
    mx = functools.reduce(jnp.maximum, rows_l)
    ex = [jnp.exp(r - mx) for r in rows_l]
    den = functools.reduce(lambda u, w: u + w, ex)
    lb = jnp.zeros_like(den)
    for i in range(1, layer + 1):
        lb = lb + ex[i] / den
    log_lb = jnp.log(lb)
    log_1m = jnp.log1p(-lb)
    one_m = 1.0 - lb

    log_lb2 = log_lb * LOG2E
    log_1m2 = log_1m * LOG2E

    fz = p_scr[:, d:2 * d]
    fz2 = fz * LOG2E
    e = jnp.exp2(_neg_abs(fz2))
    u = 1.0 + e
    bv2 = (log_1m2 + jnp.minimum(fz2, 0.0)) - jnp.log2(u)
    lf_scr[...] = jnp.maximum(log_lb2, bv2) + jnp.log2(1.0 + jnp.exp2(_neg_abs(log_lb2 - bv2)))
    p_scr[:, d:2 * d] = one_m * (jnp.where(fz >= 0.0, e, 1.0) / u)
    p_scr[:, 0:d] = _dot(h, win_ref[:, 0:d])
    p_scr[:, 2 * d:4 * d] = _dot(h, win_ref[:, 2 * d:4 * d])

    def load(ci):
        rs = slice(ci * GLA_CHUNK, (ci + 1) * GLA_CHUNK)
        return p_scr[rs, 0:d], p_scr[rs, d:2 * d], p_scr[rs, 2 * d:3 * d], lf_scr[rs, :]

    def store(ci, o):
        o_scr[ci * GLA_CHUNK:(ci + 1) * GLA_CHUNK, :] = o

    _gla_tile(tt // GLA_CHUNK, load, store, st_scr, heads=heads, dk=d // heads, dv=d // heads, rows=GLA_CHUNK)

    o = _rmsnorm(o_scr[...], onw_ref[...]) * _sigmoid(p_scr[:, 3 * d:4 * d])
    out = _dot(o.astype(BF16), wout_ref[...])
    o_ref[0] = x + g_ref[0] * out


def _hg_call(x, nw, mods, base, w_in, lb_logits, onw, w_out, layer):
    b, s, d = x.shape
    heads = d // HG_EXPAND
    tt = min(MIX_ROWS, s)
    tok = pl.BlockSpec((1, tt, d), lambda i, t: (i, t, 0))
    vecs = _mod_specs(d, base, 1)
    return pl.pallas_call(
        functools.partial(_hg_kernel, tt=tt, d=d, layer=layer, heads=heads),
        grid=(b, s // tt),
        in_specs=[tok, _resident((1, d)), *vecs, _resident(w_in.shape), _resident(lb_logits.shape),
                  _resident((1, d)), _resident(w_out.shape)],
        out_specs=tok,
        out_shape=jax.ShapeDtypeStruct(x.shape, F32),
        scratch_shapes=[pltpu.VMEM((tt, 4 * d), F32), pltpu.VMEM((tt, d), F32), pltpu.VMEM((tt, d), F32),
                        pltpu.VMEM((heads, HG_EXPAND, HG_EXPAND), F32)],
        compiler_params=_params(),
        name="hgrn2_mixer",
    )(x, nw.reshape(1, d), mods, mods, mods, w_in, lb_logits, onw.reshape(1, d), w_out)


def _gla_kernel(x_ref, nw_ref, sh_ref, sc_ref, g_ref, win_ref, gw2_ref, gb_ref, onw_ref, wout_ref, o_ref,
                p_scr, o_scr, st_scr, *, tt, kd, vd, heads):
    t = pl.program_id(1)

    @pl.when(t == 0)
    def _():
        st_scr[...] = jnp.zeros_like(st_scr)

    x = x_ref[0]
    h = _norm_mod(x, nw_ref[...], sh_ref[0], sc_ref[0]).astype(BF16)
    p_scr[...] = _dot(h, win_ref[...])
    dk = kd // heads
    dv = vd // heads
    scale = dk ** -0.5
    low0 = 2 * kd + 2 * vd

    def load(ci):
        rs = slice(ci * GLA_CHUNK, (ci + 1) * GLA_CHUNK)
        q = p_scr[rs, 0:kd] * scale
        k = p_scr[rs, kd:2 * kd]
        v = p_scr[rs, 2 * kd:2 * kd + vd]
        a_low = p_scr[rs, low0:low0 + LANES].astype(BF16)
        z2 = (_dot(a_low, gw2_ref[...]) + gb_ref[...]) * LOG2E
        lf2 = (jnp.minimum(z2, 0.0) - jnp.log2(1.0 + jnp.exp2(_neg_abs(z2)))) * (1.0 / GLA_LOGIT_NORM)
        return q, k, v, lf2

    def store(ci, o):
        o_scr[ci * GLA_CHUNK:(ci + 1) * GLA_CHUNK, :] = o

    _gla_tile(tt // GLA_CHUNK, load, store, st_scr, heads=heads, dk=dk, dv=dv, rows=GLA_CHUNK)

    gate = p_scr[:, 2 * kd + vd:2 * kd + 2 * vd]
    onw = onw_ref[...]
    parts = []
    for hd in range(heads):
        vs = slice(hd * dv, (hd + 1) * dv)
        gh = gate[:, vs]
        parts.append(_rmsnorm(o_scr[:, vs], onw) * (gh * _sigmoid(gh)))
    out = _dot(jnp.concatenate(parts, axis=1).astype(BF16), wout_ref[...])
    o_ref[0] = x + g_ref[0] * out


def _gla_call(x, nw, mods, base, w_in, gate_w2, gate_b, onw, w_out):
    b, s, d = x.shape
    kd = gate_w2.shape[1]
    vd = w_out.shape[0]
    heads = GLA_HEADS
    dv = vd // heads
    tt = min(MIX_ROWS, s)
    cols = 2 * kd + 2 * vd + LANES
    w_in_p = jnp.zeros((d, cols), BF16).at[:, :w_in.shape[1]].set(w_in)
    gw2_p = jnp.zeros((LANES, kd), BF16).at[:GLA_RANK, :].set(gate_w2)
    tok = pl.BlockSpec((1, tt, d), lambda i, t: (i, t, 0))
    vecs = _mod_specs(d, base, 1)
    return pl.pallas_call(
        functools.partial(_gla_kernel, tt=tt, kd=kd, vd=vd, heads=heads),
        grid=(b, s // tt),
        in_specs=[tok, _resident((1, d)), *vecs, _resident(w_in_p.shape), _resident(gw2_p.shape),
                  _resident((1, kd)), _resident((1, dv)), _resident(w_out.shape)],
        out_specs=tok,
        out_shape=jax.ShapeDtypeStruct(x.shape, F32),
        scratch_shapes=[pltpu.VMEM((tt, cols), F32), pltpu.VMEM((tt, vd), F32),
                        pltpu.VMEM((heads, dv, kd // heads), F32)],
        compiler_params=_params(),
        name="gla_mixer",
    )(x, nw.reshape(1, d), mods, mods, mods, w_in_p, gw2_p, gate_b.reshape(1, kd), onw.reshape(1, dv), w_out)


def kernel(x, c, ada_w, ada_b, norm_w, final_norm_w, ffn_w13, ffn_w2, rg_w_in, rg_conv_w, rg_conv_b, rg_gate_w,
           rg_gate_b, rg_lambda, rg_w_out, hg_w_in, hg_lb_logits, hg_norm_w, hg_w_out, gla_w_in, gla_gate_w2,
           gla_gate_b, gla_norm_w, gla_w_out):
    depth = ada_w.shape[0]
    b, s, d = x.shape
    assert s % min(FFN_ROWS, s) == 0 and s % min(MIX_ROWS, s) == 0 and min(MIX_ROWS, s) % GLA_CHUNK == 0
    mods = _mod_call(c, ada_w, ada_b).reshape(depth * N_MOD, b, 1, d)
    bf = lambda w: w.astype(BF16)
    w13b, w2b = bf(ffn_w13), bf(ffn_w2)
    i_rg = i_hg = i_gla = 0
    for l in range(depth):
        base = l * N_MOD
        x = _ffn_call(x, norm_w[l, 0], mods, base, w13b, w2b, l, 0)
        m = l % N_MIXERS
        if m == 0:
            x = _rg_call(x, norm_w[l, 1], mods, base + 3, bf(rg_w_in[i_rg]), rg_conv_w[i_rg], rg_conv_b[i_rg],
                         bf(rg_gate_w[i_rg]), rg_gate_b[i_rg], rg_lambda[i_rg], bf(rg_w_out[i_rg]))
            i_rg += 1
        elif m == 1:
            x = _hg_call(x, norm_w[l, 1], mods, base + 3, bf(hg_w_in[i_hg]), hg_lb_logits, hg_norm_w[i_hg],
                         bf(hg_w_out[i_hg]), l)
            i_hg += 1
        else:
            x = _gla_call(x, norm_w[l, 1], mods, base + 3, bf(gla_w_in[i_gla]), bf(gla_gate_w2[i_gla]),
                          gla_gate_b[i_gla], gla_norm_w[i_gla], bf(gla_w_out[i_gla]))
            i_gla += 1
        x = _ffn_call(x, norm_w[l, 2], mods, base + 6, w13b, w2b, l, 1,
                      final_w=final_norm_w if l == depth - 1 else None)
    return x
```

```python
import functools

import jax
import jax.numpy as jnp
from jax import lax
from jax.experimental import pallas as pl
from jax.experimental.pallas import tpu as pltpu

F32 = jnp.float32
BF16 = jnp.bfloat16

N_MIXERS = 3
N_MOD = 9
EPS = 1e-6
RG_HEADS = 8
CONV_WIDTH = 4
RG_C = 8.0
HG_EXPAND = 128
GLA_HEADS = 4
GLA_RANK = 16
GLA_LOGIT_NORM = 16.0
LOG2E = 1.4426950408889634

LANES = 128
SUBLANES = 8
MXU_COLS = 256
VMEM_LIMIT_BYTES = 56 * 1024 * 1024

FFN_ROWS = 1024
FFN_SUB = 512
FFN_COL_CHUNKS = 2
MIX_ROWS = 512
RG_STEPS = 128
RG_SUB = 32
GLA_CHUNK = 64


def _params():
    return pltpu.CompilerParams(dimension_semantics=("arbitrary", "arbitrary"),
                                vmem_limit_bytes=VMEM_LIMIT_BYTES)


def _resident(shape):
    zeros = (0,) * len(shape)
    return pl.BlockSpec(shape, lambda *_: zeros, pipeline_mode=pl.Buffered(1))


def _mod_specs(d, base, seqs):
    return [pl.BlockSpec((None, seqs, 1, d), lambda i, t, row=base + j: (row, i, 0, 0)) for j in range(3)]


def _dot(a, b):
    return jnp.dot(a, b, preferred_element_type=F32)


def _dot_nt(a, b):
    return lax.dot_general(a, b, (((1,), (1,)), ((), ())), preferred_element_type=F32)


def _dot_tn(a, b):
    return lax.dot_general(a, b, (((0,), (0,)), ((), ())), preferred_element_type=F32)


def _neg_abs(x):
    bits = lax.bitcast_convert_type(x, jnp.uint32) | jnp.uint32(0x80000000)
    return lax.bitcast_convert_type(bits, F32)


def _sigmoid(x):
    return 1.0 / (1.0 + jnp.exp(-x))


def _softplus(x):
    return jnp.maximum(x, 0.0) + jnp.log1p(jnp.exp(-jnp.abs(x)))


def _rmsnorm(x, w):
    ms = jnp.mean(x * x, axis=-1, keepdims=True)
    return (x * lax.rsqrt(ms + EPS)) * w


def _norm_mod(x, nw, shift, scale):
    return _rmsnorm(x, nw) * (1.0 + scale) + shift


def _mod_kernel(c_ref, w_ref, b_ref, o_ref):
    c = c_ref[...]
    ca = c * _sigmoid(c)
    w = w_ref[0]
    c_hi = ca.astype(BF16)
    c_lo = (ca - c_hi.astype(F32)).astype(BF16)
    w_hi = w.astype(BF16)
    w_lo = (w - w_hi.astype(F32)).astype(BF16)
    acc = _dot(c_hi, w_hi) + (_dot(c_hi, w_lo) + _dot(c_lo, w_hi))
    o_ref[0] = acc + b_ref[0]


def _mod_call(c, ada_w, ada_b):
    depth, d, _ = ada_w.shape
    b = c.shape[0]
    return pl.pallas_call(
        _mod_kernel,
        grid=(depth, N_MOD),
        in_specs=[
            pl.BlockSpec((b, d), lambda l, j: (0, 0)),
            pl.BlockSpec((1, d, d), lambda l, j: (l, 0, j)),
            pl.BlockSpec((1, 1, d), lambda l, j: (l, 0, j)),
        ],
        out_specs=pl.BlockSpec((1, b, d), lambda l, j: (l * N_MOD + j, 0, 0)),
        out_shape=jax.ShapeDtypeStruct((depth * N_MOD, b, d), F32),
        compiler_params=_params(),
        name="adaln_table",
    )(c, ada_w, ada_b.reshape(depth, 1, N_MOD * d))


def _ffn_kernel(x_ref, nw_ref, sh_ref, sc_ref, g_ref, w13_ref, w2_ref, *rest, d_ff, tm, sub, final):
    if final:
        fw_ref, o_ref = rest
    else:
        (o_ref,) = rest
    nw, sh, sc = nw_ref[...], sh_ref[0], sc_ref[0]
    half_g = 0.5 * g_ref[0]
    tiles = -(-d_ff // MXU_COLS)
    per = -(-tiles // FFN_COL_CHUNKS)
    bounds = [min(c * per * MXU_COLS, d_ff) for c in range(FFN_COL_CHUNKS + 1)]
    for r0 in range(0, tm, sub):
        x = x_ref[0, r0:r0 + sub, :]
        h = _norm_mod(x, nw, sh, sc).astype(BF16)
        y = None
        for c0, c1 in zip(bounds[:-1], bounds[1:]):
            gate = _dot(h, w13_ref[:, c0:c1])
            up = _dot(h, w13_ref[:, d_ff + c0:d_ff + c1])
            act = (gate * _sigmoid(gate) * up).astype(BF16)
            part = _dot(act, w2_ref[c0:c1, :])
            y = part if y is None else y + part
        out = x + half_g * y
        if final:
            out = _rmsnorm(out, fw_ref[...])
        o_ref[0, r0:r0 + sub, :] = out


def _ffn_call(x, nw, mods, base, w13, w2, layer, which, final_w=None):
    b, s, d = x.shape
    d_ff = w2.shape[2]
    tm = min(FFN_ROWS, s)
    sub = min(FFN_SUB, tm)
    assert tm % sub == 0 and d_ff % LANES == 0
    tok = pl.BlockSpec((1, tm, d), lambda i, t: (i, t, 0))
    vecs = _mod_specs(d, base, 1)
    pick = lambda i, t: (layer, which, 0, 0)
    w13_spec = pl.BlockSpec((None, None, d, 2 * d_ff), pick, pipeline_mode=pl.Buffered(1))
    w2_spec = pl.BlockSpec((None, None, d_ff, d), pick, pipeline_mode=pl.Buffered(1))
    in_specs = [tok, _resident((1, d)), *vecs, w13_spec, w2_spec]
    args = [x, nw.reshape(1, d), mods, mods, mods, w13, w2]
    if final_w is not None:
        in_specs.append(_resident((1, d)))
        args.append(final_w.reshape(1, d))
    return pl.pallas_call(
        functools.partial(_ffn_kernel, d_ff=d_ff, tm=tm, sub=sub, final=final_w is not None),
        grid=(b, s // tm),
        in_specs=in_specs,
        out_specs=tok,
        out_shape=jax.ShapeDtypeStruct(x.shape, F32),
        compiler_params=_params(),
        name="swiglu_final" if final_w is not None else "swiglu",
    )(*args)


def _gelu_tanh(x):
    return x * (0.5 * (1.0 + jnp.tanh(0.7978845608028654 * (x + 0.044715 * (x * x * x)))))


def _rg_kernel(x_ref, nw_ref, sh_ref, sc_ref, g_ref, perm_ref, permt_ref, win_ref, cw_ref, cb_ref, gw_ref, gb_ref,
               lam_ref, wout_ref, o_ref, conv_scr, h_scr, *, nb, ts, sub, width):
    t = pl.program_id(1)
    rows = nb * sub
    d = x_ref.shape[2]
    blk = width // RG_HEADS
    tail_rows = (CONV_WIDTH - 1) * nb

    @pl.when(t == 0)
    def _():
        conv_scr[...] = jnp.zeros_like(conv_scr)
        h_scr[...] = jnp.zeros_like(h_scr)

    nw, sh, sc, g = nw_ref[...], sh_ref[...], sc_ref[...], g_ref[...]
    cw, cb = cw_ref[...], cb_ref[...]
    decay = -RG_C * _softplus(-lam_ref[...])
    first_rows = lax.broadcasted_iota(jnp.int32, (rows, 1), 0) < nb
    nsub = ts // sub
    st = {"tail": conv_scr[...], "h": h_scr[...]}
    xs, ys, xbs, xcs, gxs, gas = {}, {}, {}, {}, {}, {}

    def project(si):
        x = x_ref[:, si * sub:(si + 1) * sub, :]
        h = _norm_mod(x, nw, sh, sc).reshape(rows, d).astype(BF16)
        hp = _dot(perm_ref[...], h).astype(BF16)
        yx = _dot(hp, win_ref[...])
        xs[si] = x
        ys[si] = _gelu_tanh(yx[:, :width])
        xbs[si] = yx[:, width:]

    def conv_gates(si):
        xb = xbs.pop(si)
        ext = jnp.concatenate([st["tail"], xb], axis=0)
        xc = cb
        for j in range(CONV_WIDTH):
            xc = xc + ext[j * nb:j * nb + rows] * cw[j:j + 1]
        st["tail"] = xb[rows - tail_rows:rows]
        xcb = xc.astype(BF16)
        gx, ga = [], []
        for hd in range(RG_HEADS):
            gts = _dot(xcb[:, hd * blk:(hd + 1) * blk], gw_ref[hd]) + gb_ref[hd]
            gx.append(gts[:, :blk])
            ga.append(gts[:, blk:])
        xcs[si] = xc
        gxs[si] = jnp.concatenate(gx, axis=1)
        gas[si] = jnp.concatenate(ga, axis=1)

    def recur_out(si):
        log_a = _sigmoid(gas.pop(si)) * decay
        a = jnp.exp(log_a)
        m2 = -jnp.tanh(log_a) * (a * a + 1.0)
        mult = jnp.where(m2 > 0.0, m2 * lax.rsqrt(m2), 0.0)
        if si == 0:
            mult = jnp.where(jnp.logical_and(first_rows, t == 0), 1.0, mult)
        u = _sigmoid(gxs.pop(si)) * xcs.pop(si) * mult
        hcar = st["h"]
        hs = []
        for ti in range(sub):
            hcar = a[ti * nb:(ti + 1) * nb] * hcar + u[ti * nb:(ti + 1) * nb]
            hs.append(hcar)
        st["h"] = hcar
        z = (jnp.concatenate(hs, axis=0) * ys.pop(si)).astype(BF16)
        zn = _dot(permt_ref[...], z).astype(BF16)
        out = _dot(zn, wout_ref[...]).reshape(nb, sub, d)
        o_ref[:, si * sub:(si + 1) * sub, :] = xs.pop(si) + g * out

    for step in range(nsub + 2):
        if step < nsub:
            project(step)
        if 0 <= step - 1 < nsub:
            conv_gates(step - 1)
        if 0 <= step - 2 < nsub:
            recur_out(step - 2)
    conv_scr[...] = st["tail"]
    h_scr[...] = st["h"]


def _rg_call(x, nw, mods, base, w_in, conv_w, conv_b, gate_w, gate_b, lam, w_out):
    b, s, d = x.shape
    width = w_out.shape[0]
    blk = width // RG_HEADS
    nb = SUBLANES
    ts = min(RG_STEPS, s)
    sub = min(RG_SUB, ts)
    rows = nb * sub
    assert b % nb == 0 and s % ts == 0 and ts % sub == 0 and sub % SUBLANES == 0
    r = jnp.arange(rows, dtype=jnp.int32)
    perm = ((r[:, None] % nb) * sub + r[:, None] // nb == r[None, :]).astype(BF16)
    tok = pl.BlockSpec((nb, ts, d), lambda i, t: (i, t, 0))
    vecs = _mod_specs(d, base, nb)
    return pl.pallas_call(
        functools.partial(_rg_kernel, nb=nb, ts=ts, sub=sub, width=width),
        grid=(b // nb, s // ts),
        in_specs=[tok, _resident((1, d)), *vecs, _resident((rows, rows)), _resident((rows, rows)),
                  _resident(w_in.shape), _resident((CONV_WIDTH, width)), _resident((1, width)),
                  _resident(gate_w.shape), _resident((RG_HEADS, 1, 2 * blk)), _resident((1, width)),
                  _resident(w_out.shape)],
        out_specs=tok,
        out_shape=jax.ShapeDtypeStruct(x.shape, F32),
        scratch_shapes=[pltpu.VMEM(((CONV_WIDTH - 1) * nb, width), F32), pltpu.VMEM((nb, width), F32)],
        compiler_params=_params(),
        name="rglru_mixer",
    )(x, nw.reshape(1, d), mods, mods, mods, perm, perm.T, w_in, conv_w, conv_b.reshape(1, width), gate_w,
      gate_b.reshape(RG_HEADS, 1, 2 * blk), lam.reshape(1, width), w_out)


def _cumsum_rows(v, rows):
    n = v.shape[1]
    groups = rows // SUBLANES
    v3 = v.reshape(groups, SUBLANES, n)
    srow = lax.broadcasted_iota(jnp.int32, v3.shape, 1)
    for k in (1, 2, 4):
        v3 = v3 + jnp.where(srow >= k, pltpu.roll(v3, k, 1), 0.0)
    out, carry = [], None
    for gi in range(groups):
        blk = v3[gi] if carry is None else v3[gi] + carry
        out.append(blk)
        carry = blk[SUBLANES - 1:SUBLANES, :]
    return jnp.concatenate(out, axis=0)


def _anchor(b, half, rows):
    n = b.shape[1]
    if 2 * half >= SUBLANES:
        v = b.reshape(rows // (2 * half), 2 * half, n)
        return jnp.broadcast_to(v[:, half - 1:half, :], v.shape).reshape(rows, n)
    v = b.reshape(rows // SUBLANES, SUBLANES, n)
    srow = lax.broadcasted_iota(jnp.int32, v.shape, 1)
    if half == 2:
        lo = jnp.broadcast_to(v[:, 1:2, :], v.shape)
        hi = jnp.broadcast_to(v[:, 5:6, :], v.shape)
        return jnp.where(srow < 4, lo, hi).reshape(rows, n)
    assert half == 1
    return jnp.where((srow & 1) == 1, pltpu.roll(v, 1, 1), v).reshape(rows, n)


def _block_diag(a, b):
    top = jnp.concatenate([a, jnp.zeros_like(b)], axis=1)
    bot = jnp.concatenate([jnp.zeros_like(a), b], axis=1)
    return jnp.concatenate([top, bot], axis=0)


def _gla_masks(rows):
    qi = lax.broadcasted_iota(jnp.int32, (rows, 2 * rows), 0)
    kj = lax.broadcasted_iota(jnp.int32, (rows, 2 * rows), 1) & (rows - 1)
    masks = []
    half, lvl = 1, 0
    while half < rows:
        same = (qi >> (lvl + 1)) == (kj >> (lvl + 1))
        masks.append(same & ((qi & half) != 0) & ((kj & half) == 0))
        half, lvl = half * 2, lvl + 1
    return qi == kj, masks


def _gla_prep(q, k, v, lf2, *, rows):
    b = _cumsum_rows(lf2, rows)
    b_last = b[rows - 1:rows, :]
    qe = (q * jnp.exp2(b)).astype(BF16)
    kd = (k * jnp.exp2(b_last - b)).astype(BF16)
    eb = jnp.exp2(b_last)
    vb = v.astype(BF16)
    rown = lax.broadcasted_iota(jnp.int32, b.shape, 0)
    mixed = []
    half = 1
    while half < rows:
        if half >= SUBLANES:
            parts = []
            for r0 in range(0, rows, half):
                if r0 & half:
                    parts.append(q[r0:r0 + half] * jnp.exp2(b[r0:r0 + half] - b[r0 - 1:r0]))
                else:
                    parts.append(k[r0:r0 + half] * jnp.exp2(b[r0 + half - 1:r0 + half] - b[r0:r0 + half]))
            mixed.append(jnp.concatenate(parts, axis=0).astype(BF16))
        else:
            e = jnp.exp2(_neg_abs(b - _anchor(b, half, rows)))
            mixed.append((jnp.where((rown & half) != 0, q, k) * e).astype(BF16))
        half *= 2
    return dict(qe=qe, kd=kd, eb=eb, vb=vb, mixed=mixed, qk=q * k)


def _gla_scores(ctx, st_ref, eye, masks, *, heads, dk, dv, rows):
    qe, kd, eb, vb, mixed, qk = (ctx[n] for n in ("qe", "kd", "eb", "vb", "mixed", "qk"))
    ks = [slice(hd * dk, (hd + 1) * dk) for hd in range(heads)]
    vs = [slice(hd * dv, (hd + 1) * dv) for hd in range(heads)]
    inter = []
    for hd in range(heads):
        st = st_ref[hd]
        inter.append(_dot_nt(qe[:, ks[hd]], st.astype(BF16)))
        st_ref[hd] = st * eb[:, ks[hd]] + _dot_tn(vb[:, vs[hd]], kd[:, ks[hd]])
    left = lax.broadcasted_iota(jnp.int32, (rows, 2 * rows), 1) < rows
    scores = []
    for h0 in range(0, heads, 2):
        kc = slice(h0 * dk, (h0 + 2) * dk)
        diag = jnp.where(left, jnp.sum(qk[:, ks[h0]], axis=-1, keepdims=True),
                         jnp.sum(qk[:, ks[h0 + 1]], axis=-1, keepdims=True))
        sc = jnp.where(eye, diag, 0.0)
        for m, mk in zip(mixed, masks):
            sc = jnp.where(mk, _dot_nt(m[:, kc], _block_diag(m[:, ks[h0]], m[:, ks[h0 + 1]])), sc)
        scores.append(sc.astype(BF16))
    ctx["inter"] = inter
    ctx["scores"] = scores


def _gla_out(ctx, *, heads, dv):
    vb, inter, scores = ctx["vb"], ctx["inter"], ctx["scores"]
    vs = [slice(hd * dv, (hd + 1) * dv) for hd in range(heads)]
    outs = []
    for h0 in range(0, heads, 2):
        o2 = _dot(scores[h0 // 2], _block_diag(vb[:, vs[h0]], vb[:, vs[h0 + 1]]))
        outs.append(o2 + jnp.concatenate([inter[h0], inter[h0 + 1]], axis=1))
    return jnp.concatenate(outs, axis=1)


def _gla_tile(n_chunks, load, store, st_ref, *, heads, dk, dv, rows):
    eye, masks = _gla_masks(rows)
    ctxs = {}
    for step in range(n_chunks + 2):
        if step < n_chunks:
            ctxs[step] = _gla_prep(*load(step), rows=rows)
        if 0 <= step - 1 < n_chunks:
            _gla_scores(ctxs[step - 1], st_ref, eye, masks, heads=heads, dk=dk, dv=dv, rows=rows)
        if 0 <= step - 2 < n_chunks:
            store(step - 2, _gla_out(ctxs.pop(step - 2), heads=heads, dv=dv))


def _hg_kernel(x_ref, nw_ref, sh_ref, sc_ref, g_ref, win_ref, lbl_ref, onw_ref, wout_ref, o_ref,
               p_scr, lf_scr, o_scr, st_scr, *, tt, d, layer, heads):
    t = pl.program_id(1)

    @pl.when(t == 0)
    def _():
        st_scr[...] = jnp.zeros_like(st_scr)

    x = x_ref[0]
    h = _norm_mod(x, nw_ref[...], sh_ref[0], sc_ref[0]).astype(BF16)
    p_scr[:, d:2 * d] = _dot(h, win_ref[:, d:2 * d])

    lg = lbl_ref[...]
    rows_l = [lg[i:i + 1, :] for i in range(lg.shape[0])]
    mx = functools.reduce(jnp.maximum, rows_l)
    ex = [jnp.exp(r - mx) for r in rows_l]
    den = functools.reduce(lambda u, w: u + w, ex)
    lb = jnp.zeros_like(den)
    for i in range(1, layer + 1):
        lb = lb + ex[i] / den
    log_lb = jnp.log(lb)
    log_1m = jnp.log1p(-lb)
    one_m = 1.0 - lb

    log_lb2 = log_lb * LOG2E
    log_1m2 = log_1m * LOG2E

    fz = p_scr[:, d:2 * d]
    fz2 = fz * LOG2E
    e = jnp.exp2(_neg_abs(fz2))
    u = 1.0 + e
    bv2 = (log_1m2 + jnp.minimum(fz2, 0.0)) - jnp.log2(u)
    lf_scr[...] = jnp.maximum(log_lb2, bv2) + jnp.log2(1.0 + jnp.exp2(_neg_abs(log_lb2 - bv2)))
    p_scr[:, d:2 * d] = one_m * (jnp.where(fz >= 0.0, e, 1.0) / u)
    p_scr[:, 0:d] = _dot(h, win_ref[:, 0:d])
    p_scr[:, 2 * d:4 * d] = _dot(h, win_ref[:, 2 * d:4 * d])

    def load(ci):
        rs = slice(ci * GLA_CHUNK, (ci + 1) * GLA_CHUNK)
        return p_scr[rs, 0:d], p_scr[rs, d:2 * d], p_scr[rs, 2 * d:3 * d], lf_scr[rs, :]

    def store(ci, o):
        o_scr[ci * GLA_CHUNK:(ci + 1) * GLA_CHUNK, :] = o

    _gla_tile(tt // GLA_CHUNK, load, store, st_scr, heads=heads, dk=d // heads, dv=d // heads, rows=GLA_CHUNK)

    o = _rmsnorm(o_scr[...], onw_ref[...]) * _sigmoid(p_scr[:, 3 * d:4 * d])
    out = _dot(o.astype(BF16), wout_ref[...])
    o_ref[0] = x + g_ref[0] * out


def _hg_call(x, nw, mods, base, w_in, lb_logits, onw, w_out, layer):
    b, s, d = x.shape
    heads = d // HG_EXPAND
    tt = min(MIX_ROWS, s)
    tok = pl.BlockSpec((1, tt, d), lambda i, t: (i, t, 0))
    vecs = _mod_specs(d, base, 1)
    return pl.pallas_call(
        functools.partial(_hg_kernel, tt=tt, d=d, layer=layer, heads=heads),
        grid=(b, s // tt),
        in_specs=[tok, _resident((1, d)), *vecs, _resident(w_in.shape), _resident(lb_logits.shape),
                  _resident((1, d)), _resident(w_out.shape)],
        out_specs=tok,
        out_shape=jax.ShapeDtypeStruct(x.shape, F32),
        scratch_shapes=[pltpu.VMEM((tt, 4 * d), F32), pltpu.VMEM((tt, d), F32), pltpu.VMEM((tt, d), F32),
                        pltpu.VMEM((heads, HG_EXPAND, HG_EXPAND), F32)],
        compiler_params=_params(),
        name="hgrn2_mixer",
    )(x, nw.reshape(1, d), mods, mods, mods, w_in, lb_logits, onw.reshape(1, d), w_out)


def _gla_kernel(x_ref, nw_ref, sh_ref, sc_ref, g_ref, win_ref, gw2_ref, gb_ref, onw_ref, wout_ref, o_ref,
                p_scr, lf_scr, o_scr, st_scr, *, tt, kd, vd, heads):
    t = pl.program_id(1)

    @pl.when(t == 0)
    def _():
        st_scr[...] = jnp.zeros_like(st_scr)

    x = x_ref[0]
    h = _norm_mod(x, nw_ref[...], sh_ref[0], sc_ref[0]).astype(BF16)
    c_v, c_low = 2 * kd, 2 * kd + 2 * vd
    a_low = _dot(h, win_ref[:, c_low:c_low + LANES]).astype(BF16)
    z2 = (_dot(a_low, gw2_ref[...]) + gb_ref[...]) * LOG2E
    lf_scr[...] = (jnp.minimum(z2, 0.0) - jnp.log2(1.0 + jnp.exp2(_neg_abs(z2)))) * (1.0 / GLA_LOGIT_NORM)
    p_scr[:, 0:c_v] = _dot(h, win_ref[:, 0:c_v])
    p_scr[:, c_v:c_low] = _dot(h, win_ref[:, c_v:c_low])
    dk = kd // heads
    dv = vd // heads
    scale = dk ** -0.5

    def load(ci):
        rs = slice(ci * GLA_CHUNK, (ci + 1) * GLA_CHUNK)
        q = p_scr[rs, 0:kd] * scale
        k = p_scr[rs, kd:2 * kd]
        v = p_scr[rs, 2 * kd:2 * kd + vd]
        return q, k, v, lf_scr[rs, :]

    def store(ci, o):
        o_scr[ci * GLA_CHUNK:(ci + 1) * GLA_CHUNK, :] = o

    _gla_tile(tt // GLA_CHUNK, load, store, st_scr, heads=heads, dk=dk, dv=dv, rows=GLA_CHUNK)

    gate = p_scr[:, 2 * kd + vd:2 * kd + 2 * vd]
    onw = onw_ref[...]
    parts = []
    for hd in range(heads):
        vs = slice(hd * dv, (hd + 1) * dv)
        gh = gate[:, vs]
        parts.append(_rmsnorm(o_scr[:, vs], onw) * (gh * _sigmoid(gh)))
    out = _dot(jnp.concatenate(parts, axis=1).astype(BF16), wout_ref[...])
    o_ref[0] = x + g_ref[0] * out


def _gla_call(x, nw, mods, base, w_in, gate_w2, gate_b, onw, w_out):
    b, s, d = x.shape
    kd = gate_w2.shape[1]
    vd = w_out.shape[0]
    heads = GLA_HEADS
    dv = vd // heads
    tt = min(MIX_ROWS, s)
    cols = 2 * kd + 2 * vd + LANES
    w_in_p = jnp.zeros((d, cols), BF16).at[:, :w_in.shape[1]].set(w_in)
    gw2_p = jnp.zeros((LANES, kd), BF16).at[:GLA_RANK, :].set(gate_w2)
    tok = pl.BlockSpec((1, tt, d), lambda i, t: (i, t, 0))
    vecs = _mod_specs(d, base, 1)
    return pl.pallas_call(
        functools.partial(_gla_kernel, tt=tt, kd=kd, vd=vd, heads=heads),
        grid=(b, s // tt),
        in_specs=[tok, _resident((1, d)), *vecs, _resident(w_in_p.shape), _resident(gw2_p.shape),
                  _resident((1, kd)), _resident((1, dv)), _resident(w_out.shape)],
        out_specs=tok,
        out_shape=jax.ShapeDtypeStruct(x.shape, F32),
        scratch_shapes=[pltpu.VMEM((tt, 2 * kd + 2 * vd), F32), pltpu.VMEM((tt, kd), F32), pltpu.VMEM((tt, vd), F32),
                        pltpu.VMEM((heads, dv, kd // heads), F32)],
        compiler_params=_params(),
        name="gla_mixer",
    )(x, nw.reshape(1, d), mods, mods, mods, w_in_p, gw2_p, gate_b.reshape(1, kd), onw.reshape(1, dv), w_out)


def kernel(x, c, ada_w, ada_b, norm_w, final_norm_w, ffn_w13, ffn_w2, rg_w_in, rg_conv_w, rg_conv_b, rg_gate_w,
           rg_gate_b, rg_lambda, rg_w_out, hg_w_in, hg_lb_logits, hg_norm_w, hg_w_out, gla_w_in, gla_gate_w2,
           gla_gate_b, gla_norm_w, gla_w_out):
    depth = ada_w.shape[0]
    b, s, d = x.shape
    assert s % min(FFN_ROWS, s) == 0 and s % min(MIX_ROWS, s) == 0 and min(MIX_ROWS, s) % GLA_CHUNK == 0
    mods = _mod_call(c, ada_w, ada_b).reshape(depth * N_MOD, b, 1, d)
    bf = lambda w: w.astype(BF16)
    w13b, w2b = bf(ffn_w13), bf(ffn_w2)
    i_rg = i_hg = i_gla = 0
    for l in range(depth):
        base = l * N_MOD
        x = _ffn_call(x, norm_w[l, 0], mods, base, w13b, w2b, l, 0)
        m = l % N_MIXERS
        if m == 0:
            x = _rg_call(x, norm_w[l, 1], mods, base + 3, bf(rg_w_in[i_rg]), rg_conv_w[i_rg], rg_conv_b[i_rg],
                         bf(rg_gate_w[i_rg]), rg_gate_b[i_rg], rg_lambda[i_rg], bf(rg_w_out[i_rg]))
            i_rg += 1
        elif m == 1:
            x = _hg_call(x, norm_w[l, 1], mods, base + 3, bf(hg_w_in[i_hg]), hg_lb_logits, hg_norm_w[i_hg],
                         bf(hg_w_out[i_hg]), l)
            i_hg += 1
        else:
            x = _gla_call(x, norm_w[l, 1], mods, base + 3, bf(gla_w_in[i_gla]), bf(gla_gate_w2[i_gla]),
                          gla_gate_b[i_gla], gla_norm_w[i_gla], bf(gla_w_out[i_gla]))
            i_gla += 1
        x = _ffn_call(x, norm_w[l, 2], mods, base + 6, w13b, w2b, l, 1,
                      final_w=final_norm_w if l == depth - 1 else None)
    return x
```

```python
import functools

import jax
import jax.numpy as jnp
from jax import lax
from jax.experimental import pallas as pl
from jax.experimental.pallas import tpu as pltpu

F32 = jnp.float32
BF16 = jnp.bfloat16

N_MIXERS = 3
N_MOD = 9
EPS = 1e-6
RG_HEADS = 8
CONV_WIDTH = 4
RG_C = 8.0
HG_EXPAND = 128
GLA_HEADS = 4
GLA_RANK = 16
GLA_LOGIT_NORM = 16.0
LOG2E = 1.4426950408889634

LANES = 128
SUBLANES = 8
MXU_COLS = 256
VMEM_LIMIT_BYTES = 56 * 1024 * 1024

MOD_PER_STEP = 3
FFN_ROWS = 1024
FFN_SUB = 512
FFN_COL_CHUNKS = 2
MIX_ROWS = 512
RG_STEPS = 128
RG_SUB = 32
GLA_CHUNK = 64


def _params():
    return pltpu.CompilerParams(dimension_semantics=("arbitrary", "arbitrary"),
                                vmem_limit_bytes=VMEM_LIMIT_BYTES)


def _resident(shape):
    zeros = (0,) * len(shape)
    return pl.BlockSpec(shape, lambda *_: zeros, pipeline_mode=pl.Buffered(1))


def _mod_specs(d, base, seqs):
    return [pl.BlockSpec((None, seqs, 1, d), lambda i, t, row=base + j: (row, i, 0, 0)) for j in range(3)]


def _dot(a, b):
    return jnp.dot(a, b, preferred_element_type=F32)


def _dot_nt(a, b):
    return lax.dot_general(a, b, (((1,), (1,)), ((), ())), preferred_element_type=F32)


def _dot_tn(a, b):
    return lax.dot_general(a, b, (((0,), (0,)), ((), ())), preferred_element_type=F32)


def _neg_abs(x):
    bits = lax.bitcast_convert_type(x, jnp.uint32) | jnp.uint32(0x80000000)
    return lax.bitcast_convert_type(bits, F32)


def _sigmoid(x):
    return 1.0 / (1.0 + jnp.exp(-x))


def _softplus(x):
    return jnp.maximum(x, 0.0) + jnp.log1p(jnp.exp(-jnp.abs(x)))


def _rmsnorm(x, w):
    ms = jnp.mean(x * x, axis=-1, keepdims=True)
    return (x * lax.rsqrt(ms + EPS)) * w


def _norm_mod(x, nw, shift, scale):
    return _rmsnorm(x, nw) * (1.0 + scale) + shift


def _mod_kernel(c_ref, w_ref, b_ref, o_ref, *, d):
    c = c_ref[...]
    ca = c * _sigmoid(c)
    w = w_ref[0]
    c_hi = ca.astype(BF16)
    c_lo = (ca - c_hi.astype(F32)).astype(BF16)
    w_hi = w.astype(BF16)
    w_lo = (w - w_hi.astype(F32)).astype(BF16)
    acc = _dot(c_hi, w_hi) + (_dot(c_hi, w_lo) + _dot(c_lo, w_hi)) + b_ref[0]
    for r in range(MOD_PER_STEP):
        o_ref[r] = acc[:, r * d:(r + 1) * d]


def _mod_call(c, ada_w, ada_b):
    depth, d, _ = ada_w.shape
    b = c.shape[0]
    steps = N_MOD // MOD_PER_STEP
    return pl.pallas_call(
        functools.partial(_mod_kernel, d=d),
        grid=(depth, steps),
        in_specs=[
            pl.BlockSpec((b, d), lambda l, j: (0, 0)),
            pl.BlockSpec((1, d, MOD_PER_STEP * d), lambda l, j: (l, 0, j)),
            pl.BlockSpec((1, 1, MOD_PER_STEP * d), lambda l, j: (l, 0, j)),
        ],
        out_specs=pl.BlockSpec((MOD_PER_STEP, b, d), lambda l, j: (l * steps + j, 0, 0)),
        out_shape=jax.ShapeDtypeStruct((depth * N_MOD, b, d), F32),
        compiler_params=_params(),
        name="adaln_table",
    )(c, ada_w, ada_b.reshape(depth, 1, N_MOD * d))


def _ffn_kernel(x_ref, nw_ref, sh_ref, sc_ref, g_ref, w13_ref, w2_ref, *rest, d_ff, tm, sub, final):
    if final:
        fw_ref, o_ref = rest
    else:
        (o_ref,) = rest
    nw, sh, sc = nw_ref[...], sh_ref[0], sc_ref[0]
    half_g = 0.5 * g_ref[0]
    tiles = -(-d_ff // MXU_COLS)
    per = -(-tiles // FFN_COL_CHUNKS)
    bounds = [min(c * per * MXU_COLS, d_ff) for c in range(FFN_COL_CHUNKS + 1)]
    for r0 in range(0, tm, sub):
        x = x_ref[0, r0:r0 + sub, :]
        h = _norm_mod(x, nw, sh, sc).astype(BF16)
        y = None
        for c0, c1 in zip(bounds[:-1], bounds[1:]):
            gate = _dot(h, w13_ref[:, c0:c1])
            up = _dot(h, w13_ref[:, d_ff + c0:d_ff + c1])
            act = (gate * _sigmoid(gate) * up).astype(BF16)
            part = _dot(act, w2_ref[c0:c1, :])
            y = part if y is None else y + part
        out = x + half_g * y
        if final:
            out = _rmsnorm(out, fw_ref[...])
        o_ref[0, r0:r0 + sub, :] = out


def _ffn_call(x, nw, mods, base, w13, w2, layer, which, final_w=None):
    b, s, d = x.shape
    d_ff = w2.shape[2]
    tm = min(FFN_ROWS, s)
    sub = min(FFN_SUB, tm)
    assert tm % sub == 0 and d_ff % LANES == 0
    tok = pl.BlockSpec((1, tm, d), lambda i, t: (i, t, 0))
    vecs = _mod_specs(d, base, 1)
    pick = lambda i, t: (layer, which, 0, 0)
    w13_spec = pl.BlockSpec((None, None, d, 2 * d_ff), pick, pipeline_mode=pl.Buffered(1))
    w2_spec = pl.BlockSpec((None, None, d_ff, d), pick, pipeline_mode=pl.Buffered(1))
    in_specs = [tok, _resident((1, d)), *vecs, w13_spec, w2_spec]
    args = [x, nw.reshape(1, d), mods, mods, mods, w13, w2]
    if final_w is not None:
        in_specs.append(_resident((1, d)))
        args.append(final_w.reshape(1, d))
    return pl.pallas_call(
        functools.partial(_ffn_kernel, d_ff=d_ff, tm=tm, sub=sub, final=final_w is not None),
        grid=(b, s // tm),
        in_specs=in_specs,
        out_specs=tok,
        out_shape=jax.ShapeDtypeStruct(x.shape, F32),
        compiler_params=_params(),
        name="swiglu_final" if final_w is not None else "swiglu",
    )(*args)


def _gelu_tanh(x):
    return x * (0.5 * (1.0 + jnp.tanh(0.7978845608028654 * (x + 0.044715 * (x * x * x)))))


def _rg_kernel(x_ref, nw_ref, sh_ref, sc_ref, g_ref, perm_ref, permt_ref, win_ref, cw_ref, cb_ref, gw_ref, gb_ref,
               lam_ref, wout_ref, o_ref, conv_scr, h_scr, *, nb, ts, sub, width):
    t = pl.program_id(1)
    rows = nb * sub
    d = x_ref.shape[2]
    blk = width // RG_HEADS
    tail_rows = (CONV_WIDTH - 1) * nb

    @pl.when(t == 0)
    def _():
        conv_scr[...] = jnp.zeros_like(conv_scr)
        h_scr[...] = jnp.zeros_like(h_scr)

    nw, sh, sc, g = nw_ref[...], sh_ref[...], sc_ref[...], g_ref[...]
    cw, cb = cw_ref[...], cb_ref[...]
    decay = -RG_C * _softplus(-lam_ref[...])
    first_rows = lax.broadcasted_iota(jnp.int32, (rows, 1), 0) < nb
    nsub = ts // sub
    st = {"tail": conv_scr[...], "h": h_scr[...]}
    xs, ys, xbs, xcs, gxs, gas = {}, {}, {}, {}, {}, {}

    def project(si):
        x = x_ref[:, si * sub:(si + 1) * sub, :]
        h = _norm_mod(x, nw, sh, sc).reshape(rows, d).astype(BF16)
        hp = _dot(perm_ref[...], h).astype(BF16)
        yx = _dot(hp, win_ref[...])
        xs[si] = x
        ys[si] = _gelu_tanh(yx[:, :width])
        xbs[si] = yx[:, width:]

    def conv_gates(si):
        xb = xbs.pop(si)
        ext = jnp.concatenate([st["tail"], xb], axis=0)
        xc = cb
        for j in range(CONV_WIDTH):
            xc = xc + ext[j * nb:j * nb + rows] * cw[j:j + 1]
        st["tail"] = xb[rows - tail_rows:rows]
        xcb = xc.astype(BF16)
        gx, ga = [], []
        for hd in range(RG_HEADS):
            gts = _dot(xcb[:, hd * blk:(hd + 1) * blk], gw_ref[hd]) + gb_ref[hd]
            gx.append(gts[:, :blk])
            ga.append(gts[:, blk:])
        xcs[si] = xc
        gxs[si] = jnp.concatenate(gx, axis=1)
        gas[si] = jnp.concatenate(ga, axis=1)

    def recur_out(si):
        log_a = _sigmoid(gas.pop(si)) * decay
        a = jnp.exp(log_a)
        m2 = -jnp.tanh(log_a) * (a * a + 1.0)
        mult = jnp.where(m2 > 0.0, m2 * lax.rsqrt(m2), 0.0)
        if si == 0:
            mult = jnp.where(jnp.logical_and(first_rows, t == 0), 1.0, mult)
        u = _sigmoid(gxs.pop(si)) * xcs.pop(si) * mult
        hcar = st["h"]
        hs = []
        for ti in range(sub):
            hcar = a[ti * nb:(ti + 1) * nb] * hcar + u[ti * nb:(ti + 1) * nb]
            hs.append(hcar)
        st["h"] = hcar
        z = (jnp.concatenate(hs, axis=0) * ys.pop(si)).astype(BF16)
        zn = _dot(permt_ref[...], z).astype(BF16)
        out = _dot(zn, wout_ref[...]).reshape(nb, sub, d)
        o_ref[:, si * sub:(si + 1) * sub, :] = xs.pop(si) + g * out

    for step in range(nsub + 2):
        if step < nsub:
            project(step)
        if 0 <= step - 1 < nsub:
            conv_gates(step - 1)
        if 0 <= step - 2 < nsub:
            recur_out(step - 2)
    conv_scr[...] = st["tail"]
    h_scr[...] = st["h"]


def _rg_call(x, nw, mods, base, w_in, conv_w, conv_b, gate_w, gate_b, lam, w_out):
    b, s, d = x.shape
    width = w_out.shape[0]
    blk = width // RG_HEADS
    nb = SUBLANES
    ts = min(RG_STEPS, s)
    sub = min(RG_SUB, ts)
    rows = nb * sub
    assert b % nb == 0 and s % ts == 0 and ts % sub == 0 and sub % SUBLANES == 0
    r = jnp.arange(rows, dtype=jnp.int32)
    perm = ((r[:, None] % nb) * sub + r[:, None] // nb == r[None, :]).astype(BF16)
    tok = pl.BlockSpec((nb, ts, d), lambda i, t: (i, t, 0))
    vecs = _mod_specs(d, base, nb)
    return pl.pallas_call(
        functools.partial(_rg_kernel, nb=nb, ts=ts, sub=sub, width=width),
        grid=(b // nb, s // ts),
        in_specs=[tok, _resident((1, d)), *vecs, _resident((rows, rows)), _resident((rows, rows)),
                  _resident(w_in.shape), _resident((CONV_WIDTH, width)), _resident((1, width)),
                  _resident(gate_w.shape), _resident((RG_HEADS, 1, 2 * blk)), _resident((1, width)),
                  _resident(w_out.shape)],
        out_specs=tok,
        out_shape=jax.ShapeDtypeStruct(x.shape, F32),
        scratch_shapes=[pltpu.VMEM(((CONV_WIDTH - 1) * nb, width), F32), pltpu.VMEM((nb, width), F32)],
        compiler_params=_params(),
        name="rglru_mixer",
    )(x, nw.reshape(1, d), mods, mods, mods, perm, perm.T, w_in, conv_w, conv_b.reshape(1, width), gate_w,
      gate_b.reshape(RG_HEADS, 1, 2 * blk), lam.reshape(1, width), w_out)


def _cumsum_rows(v, rows):
    n = v.shape[1]
    groups = rows // SUBLANES
    v3 = v.reshape(groups, SUBLANES, n)
    srow = lax.broadcasted_iota(jnp.int32, v3.shape, 1)
    for k in (1, 2, 4):
        v3 = v3 + jnp.where(srow >= k, pltpu.roll(v3, k, 1), 0.0)
    out, carry = [], None
    for gi in range(groups):
        blk = v3[gi] if carry is None else v3[gi] + carry
        out.append(blk)
        carry = blk[SUBLANES - 1:SUBLANES, :]
    return jnp.concatenate(out, axis=0)


def _anchor(b, half, rows):
    n = b.shape[1]
    if 2 * half >= SUBLANES:
        v = b.reshape(rows // (2 * half), 2 * half, n)
        return jnp.broadcast_to(v[:, half - 1:half, :], v.shape).reshape(rows, n)
    v = b.reshape(rows // SUBLANES, SUBLANES, n)
    srow = lax.broadcasted_iota(jnp.int32, v.shape, 1)
    if half == 2:
        lo = jnp.broadcast_to(v[:, 1:2, :], v.shape)
        hi = jnp.broadcast_to(v[:, 5:6, :], v.shape)
        return jnp.where(srow < 4, lo, hi).reshape(rows, n)
    assert half == 1
    return jnp.where((srow & 1) == 1, pltpu.roll(v, 1, 1), v).reshape(rows, n)


def _block_diag(a, b):
    top = jnp.concatenate([a, jnp.zeros_like(b)], axis=1)
    bot = jnp.concatenate([jnp.zeros_like(a), b], axis=1)
    return jnp.concatenate([top, bot], axis=0)


def _gla_masks(rows):
    qi = lax.broadcasted_iota(jnp.int32, (rows, 2 * rows), 0)
    kj = lax.broadcasted_iota(jnp.int32, (rows, 2 * rows), 1) & (rows - 1)
    masks = []
    half, lvl = 1, 0
    while half < rows:
        same = (qi >> (lvl + 1)) == (kj >> (lvl + 1))
        masks.append(same & ((qi & half) != 0) & ((kj & half) == 0))
        half, lvl = half * 2, lvl + 1
    return qi == kj, masks


def _gla_prep(q, k, v, lf2, *, rows):
    b = _cumsum_rows(lf2, rows)
    b_last = b[rows - 1:rows, :]
    qe = (q * jnp.exp2(b)).astype(BF16)
    kd = (k * jnp.exp2(b_last - b)).astype(BF16)
    eb = jnp.exp2(b_last)
    vb = v.astype(BF16)
    rown = lax.broadcasted_iota(jnp.int32, b.shape, 0)
    mixed = []
    half = 1
    while half < rows:
        if half >= SUBLANES:
            parts = []
            for r0 in range(0, rows, half):
                if r0 & half:
                    parts.append(q[r0:r0 + half] * jnp.exp2(b[r0:r0 + half] - b[r0 - 1:r0]))
                else:
                    parts.append(k[r0:r0 + half] * jnp.exp2(b[r0 + half - 1:r0 + half] - b[r0:r0 + half]))
            mixed.append(jnp.concatenate(parts, axis=0).astype(BF16))
        else:
            e = jnp.exp2(_neg_abs(b - _anchor(b, half, rows)))
            mixed.append((jnp.where((rown & half) != 0, q, k) * e).astype(BF16))
        half *= 2
    return dict(qe=qe, kd=kd, eb=eb, vb=vb, mixed=mixed, qk=q * k)


def _gla_scores(ctx, st_ref, eye, masks, *, heads, dk, dv, rows):
    qe, kd, eb, vb, mixed, qk = (ctx[n] for n in ("qe", "kd", "eb", "vb", "mixed", "qk"))
    ks = [slice(hd * dk, (hd + 1) * dk) for hd in range(heads)]
    vs = [slice(hd * dv, (hd + 1) * dv) for hd in range(heads)]
    inter = []
    for hd in range(heads):
        st = st_ref[hd]
        inter.append(_dot_nt(qe[:, ks[hd]], st.astype(BF16)))
        st_ref[hd] = st * eb[:, ks[hd]] + _dot_tn(vb[:, vs[hd]], kd[:, ks[hd]])
    left = lax.broadcasted_iota(jnp.int32, (rows, 2 * rows), 1) < rows
    scores = []
    for h0 in range(0, heads, 2):
        kc = slice(h0 * dk, (h0 + 2) * dk)
        diag = jnp.where(left, jnp.sum(qk[:, ks[h0]], axis=-1, keepdims=True),
                         jnp.sum(qk[:, ks[h0 + 1]], axis=-1, keepdims=True))
        sc = jnp.where(eye, diag, 0.0)
        for m, mk in zip(mixed, masks):
            sc = jnp.where(mk, _dot_nt(m[:, kc], _block_diag(m[:, ks[h0]], m[:, ks[h0 + 1]])), sc)
        scores.append(sc.astype(BF16))
    ctx["inter"] = inter
    ctx["scores"] = scores


def _gla_out(ctx, *, heads, dv):
    vb, inter, scores = ctx["vb"], ctx["inter"], ctx["scores"]
    vs = [slice(hd * dv, (hd + 1) * dv) for hd in range(heads)]
    outs = []
    for h0 in range(0, heads, 2):
        o2 = _dot(scores[h0 // 2], _block_diag(vb[:, vs[h0]], vb[:, vs[h0 + 1]]))
        outs.append(o2 + jnp.concatenate([inter[h0], inter[h0 + 1]], axis=1))
    return jnp.concatenate(outs, axis=1)


def _gla_tile(n_chunks, load, store, st_ref, *, heads, dk, dv, rows):
    eye, masks = _gla_masks(rows)
    ctxs = {}
    for step in range(n_chunks + 2):
        if step < n_chunks:
            ctxs[step] = _gla_prep(*load(step), rows=rows)
        if 0 <= step - 1 < n_chunks:
            _gla_scores(ctxs[step - 1], st_ref, eye, masks, heads=heads, dk=dk, dv=dv, rows=rows)
        if 0 <= step - 2 < n_chunks:
            store(step - 2, _gla_out(ctxs.pop(step - 2), heads=heads, dv=dv))


def _hg_kernel(x_ref, nw_ref, sh_ref, sc_ref, g_ref, win_ref, lbl_ref, onw_ref, wout_ref, o_ref,
               p_scr, lf_scr, o_scr, st_scr, *, tt, d, layer, heads):
    t = pl.program_id(1)

    @pl.when(t == 0)
    def _():
        st_scr[...] = jnp.zeros_like(st_scr)

    x = x_ref[0]
    h = _norm_mod(x, nw_ref[...], sh_ref[0], sc_ref[0]).astype(BF16)
    p_scr[:, d:2 * d] = _dot(h, win_ref[:, d:2 * d])

    lg = lbl_ref[...]
    rows_l = [lg[i:i + 1, :] for i in range(lg.shape[0])]
    mx = functools.reduce(jnp.maximum, rows_l)
    ex = [jnp.exp(r - mx) for r in rows_l]
    den = functools.reduce(lambda u, w: u + w, ex)
    lb = jnp.zeros_like(den)
    for i in range(1, layer + 1):
        lb = lb + ex[i] / den
    log_lb = jnp.log(lb)
    log_1m = jnp.log1p(-lb)
    one_m = 1.0 - lb

    log_lb2 = log_lb * LOG2E
    log_1m2 = log_1m * LOG2E

    fz = p_scr[:, d:2 * d]
    fz2 = fz * LOG2E
    e = jnp.exp2(_neg_abs(fz2))
    u = 1.0 + e
    bv2 = (log_1m2 + jnp.minimum(fz2, 0.0)) - jnp.log2(u)
    lf_scr[...] = jnp.maximum(log_lb2, bv2) + jnp.log2(1.0 + jnp.exp2(_neg_abs(log_lb2 - bv2)))
    p_scr[:, d:2 * d] = one_m * (jnp.where(fz >= 0.0, e, 1.0) / u)
    p_scr[:, 0:d] = _dot(h, win_ref[:, 0:d])
    p_scr[:, 2 * d:4 * d] = _dot(h, win_ref[:, 2 * d:4 * d])

    def load(ci):
        rs = slice(ci * GLA_CHUNK, (ci + 1) * GLA_CHUNK)
        return p_scr[rs, 0:d], p_scr[rs, d:2 * d], p_scr[rs, 2 * d:3 * d], lf_scr[rs, :]

    def store(ci, o):
        o_scr[ci * GLA_CHUNK:(ci + 1) * GLA_CHUNK, :] = o

    _gla_tile(tt // GLA_CHUNK, load, store, st_scr, heads=heads, dk=d // heads, dv=d // heads, rows=GLA_CHUNK)

    o = _rmsnorm(o_scr[...], onw_ref[...]) * _sigmoid(p_scr[:, 3 * d:4 * d])
    out = _dot(o.astype(BF16), wout_ref[...])
    o_ref[0] = x + g_ref[0] * out


def _hg_call(x, nw, mods, base, w_in, lb_logits, onw, w_out, layer):
    b, s, d = x.shape
    heads = d // HG_EXPAND
    tt = min(MIX_ROWS, s)
    tok = pl.BlockSpec((1, tt, d), lambda i, t: (i, t, 0))
    vecs = _mod_specs(d, base, 1)
    return pl.pallas_call(
        functools.partial(_hg_kernel, tt=tt, d=d, layer=layer, heads=heads),
        grid=(b, s // tt),
        in_specs=[tok, _resident((1, d)), *vecs, _resident(w_in.shape), _resident(lb_logits.shape),
                  _resident((1, d)), _resident(w_out.shape)],
        out_specs=tok,
        out_shape=jax.ShapeDtypeStruct(x.shape, F32),
        scratch_shapes=[pltpu.VMEM((tt, 4 * d), F32), pltpu.VMEM((tt, d), F32), pltpu.VMEM((tt, d), F32),
                        pltpu.VMEM((heads, HG_EXPAND, HG_EXPAND), F32)],
        compiler_params=_params(),
        name="hgrn2_mixer",
    )(x, nw.reshape(1, d), mods, mods, mods, w_in, lb_logits, onw.reshape(1, d), w_out)


def _gla_kernel(x_ref, nw_ref, sh_ref, sc_ref, g_ref, win_ref, gw2_ref, gb_ref, onw_ref, wout_ref, o_ref,
                p_scr, lf_scr, o_scr, st_scr, *, tt, kd, vd, heads):
    t = pl.program_id(1)

    @pl.when(t == 0)
    def _():
        st_scr[...] = jnp.zeros_like(st_scr)

    x = x_ref[0]
    h = _norm_mod(x, nw_ref[...], sh_ref[0], sc_ref[0]).astype(BF16)
    c_v, c_low = 2 * kd, 2 * kd + 2 * vd
    a_low = _dot(h, win_ref[:, c_low:c_low + LANES]).astype(BF16)
    z2 = (_dot(a_low, gw2_ref[...]) + gb_ref[...]) * LOG2E
    lf_scr[...] = (jnp.minimum(z2, 0.0) - jnp.log2(1.0 + jnp.exp2(_neg_abs(z2)))) * (1.0 / GLA_LOGIT_NORM)
    p_scr[:, 0:c_v] = _dot(h, win_ref[:, 0:c_v])
    p_scr[:, c_v:c_low] = _dot(h, win_ref[:, c_v:c_low])
    dk = kd // heads
    dv = vd // heads
    scale = dk ** -0.5

    def load(ci):
        rs = slice(ci * GLA_CHUNK, (ci + 1) * GLA_CHUNK)
        q = p_scr[rs, 0:kd] * scale
        k = p_scr[rs, kd:2 * kd]
        v = p_scr[rs, 2 * kd:2 * kd + vd]
        return q, k, v, lf_scr[rs, :]

    def store(ci, o):
        o_scr[ci * GLA_CHUNK:(ci + 1) * GLA_CHUNK, :] = o

    _gla_tile(tt // GLA_CHUNK, load, store, st_scr, heads=heads, dk=dk, dv=dv, rows=GLA_CHUNK)

    gate = p_scr[:, 2 * kd + vd:2 * kd + 2 * vd]
    onw = onw_ref[...]
    parts = []
    for hd in range(heads):
        vs = slice(hd * dv, (hd + 1) * dv)
        gh = gate[:, vs]
        parts.append(_rmsnorm(o_scr[:, vs], onw) * (gh * _sigmoid(gh)))
    out = _dot(jnp.concatenate(parts, axis=1).astype(BF16), wout_ref[...])
    o_ref[0] = x + g_ref[0] * out


def _gla_call(x, nw, mods, base, w_in, gate_w2, gate_b, onw, w_out):
    b, s, d = x.shape
    kd = gate_w2.shape[1]
    vd = w_out.shape[0]
    heads = GLA_HEADS
    dv = vd // heads
    tt = min(MIX_ROWS, s)
    cols = 2 * kd + 2 * vd + LANES
    w_in_p = jnp.zeros((d, cols), BF16).at[:, :w_in.shape[1]].set(w_in)
    gw2_p = jnp.zeros((LANES, kd), BF16).at[:GLA_RANK, :].set(gate_w2)
    tok = pl.BlockSpec((1, tt, d), lambda i, t: (i, t, 0))
    vecs = _mod_specs(d, base, 1)
    return pl.pallas_call(
        functools.partial(_gla_kernel, tt=tt, kd=kd, vd=vd, heads=heads),
        grid=(b, s // tt),
        in_specs=[tok, _resident((1, d)), *vecs, _resident(w_in_p.shape), _resident(gw2_p.shape),
                  _resident((1, kd)), _resident((1, dv)), _resident(w_out.shape)],
        out_specs=tok,
        out_shape=jax.ShapeDtypeStruct(x.shape, F32),
        scratch_shapes=[pltpu.VMEM((tt, 2 * kd + 2 * vd), F32), pltpu.VMEM((tt, kd), F32), pltpu.VMEM((tt, vd), F32),
                        pltpu.VMEM((heads, dv, kd // heads), F32)],
        compiler_params=_params(),
        name="gla_mixer",
    )(x, nw.reshape(1, d), mods, mods, mods, w_in_p, gw2_p, gate_b.reshape(1, kd), onw.reshape(1, dv), w_out)


def kernel(x, c, ada_w, ada_b, norm_w, final_norm_w, ffn_w13, ffn_w2, rg_w_in, rg_conv_w, rg_conv_b, rg_gate_w,
           rg_gate_b, rg_lambda, rg_w_out, hg_w_in, hg_lb_logits, hg_norm_w, hg_w_out, gla_w_in, gla_gate_w2,
           gla_gate_b, gla_norm_w, gla_w_out):
    depth = ada_w.shape[0]
    b, s, d = x.shape
    assert s % min(FFN_ROWS, s) == 0 and s % min(MIX_ROWS, s) == 0 and min(MIX_ROWS, s) % GLA_CHUNK == 0
    mods = _mod_call(c, ada_w, ada_b).reshape(depth * N_MOD, b, 1, d)
    bf = lambda w: w.astype(BF16)
    w13b, w2b = bf(ffn_w13), bf(ffn_w2)
    i_rg = i_hg = i_gla = 0
    for l in range(depth):
        base = l * N_MOD
        x = _ffn_call(x, norm_w[l, 0], mods, base, w13b, w2b, l, 0)
        m = l % N_MIXERS
        if m == 0:
            x = _rg_call(x, norm_w[l, 1], mods, base + 3, bf(rg_w_in[i_rg]), rg_conv_w[i_rg], rg_conv_b[i_rg],
                         bf(rg_gate_w[i_rg]), rg_gate_b[i_rg], rg_lambda[i_rg], bf(rg_w_out[i_rg]))
            i_rg += 1
        elif m == 1:
            x = _hg_call(x, norm_w[l, 1], mods, base + 3, bf(hg_w_in[i_hg]), hg_lb_logits, hg_norm_w[i_hg],
                         bf(hg_w_out[i_hg]), l)
            i_hg += 1
        else:
            x = _gla_call(x, norm_w[l, 1], mods, base + 3, bf(gla_w_in[i_gla]), bf(gla_gate_w2[i_gla]),
                          gla_gate_b[i_gla], gla_norm_w[i_gla], bf(gla_w_out[i_gla]))
            i_gla += 1
        x = _ffn_call(x, norm_w[l, 2], mods, base + 6, w13b, w2b, l, 1,
                      final_w=final_norm_w if l == depth - 1 else None)
    return x
```

```python
import functools

import jax
import jax.numpy as jnp
from jax import lax
from jax.experimental import pallas as pl
from jax.experimental.pallas import tpu as pltpu

F32 = jnp.float32
BF16 = jnp.bfloat16

N_MIXERS = 3
N_MOD = 9
EPS = 1e-6
RG_HEADS = 8
CONV_WIDTH = 4
RG_C = 8.0
HG_EXPAND = 128
GLA_HEADS = 4
GLA_RANK = 16
GLA_LOGIT_NORM = 16.0
LOG2E = 1.4426950408889634

LANES = 128
SUBLANES = 8
MXU_COLS = 256
VMEM_LIMIT_BYTES = 56 * 1024 * 1024

MOD_PER_STEP = 3
FFN_ROWS = 1024
FFN_SUB = 512
FFN_COL_CHUNKS = 2
MIX_ROWS = 512
RG_STEPS = 128
RG_SUB = 32
GLA_CHUNK = 64


def _params():
    return pltpu.CompilerParams(dimension_semantics=("arbitrary", "arbitrary"),
                                vmem_limit_bytes=VMEM_LIMIT_BYTES)


def _resident(shape):
    zeros = (0,) * len(shape)
    return pl.BlockSpec(shape, lambda *_: zeros, pipeline_mode=pl.Buffered(1))


def _mod_specs(d, base, seqs):
    return [pl.BlockSpec((None, seqs, 1, d), lambda i, t, row=base + j: (row, i, 0, 0)) for j in range(3)]


def _dot(a, b):
    return jnp.dot(a, b, preferred_element_type=F32)


def _dot_nt(a, b):
    return lax.dot_general(a, b, (((1,), (1,)), ((), ())), preferred_element_type=F32)


def _dot_tn(a, b):
    return lax.dot_general(a, b, (((0,), (0,)), ((), ())), preferred_element_type=F32)


def _neg_abs(x):
    bits = lax.bitcast_convert_type(x, jnp.uint32) | jnp.uint32(0x80000000)
    return lax.bitcast_convert_type(bits, F32)


def _sigmoid(x):
    return 1.0 / (1.0 + jnp.exp(-x))


def _softplus(x):
    return jnp.maximum(x, 0.0) + jnp.log1p(jnp.exp(-jnp.abs(x)))


def _rmsnorm(x, w):
    ms = jnp.mean(x * x, axis=-1, keepdims=True)
    return (x * lax.rsqrt(ms + EPS)) * w


def _norm_mod(x, nw, shift, scale):
    return _rmsnorm(x, nw) * (1.0 + scale) + shift


def _mod_kernel(c_ref, w_ref, b_ref, o_ref, *, d):
    c = c_ref[...]
    ca = c * _sigmoid(c)
    w = w_ref[0]
    c_hi = ca.astype(BF16)
    c_lo = (ca - c_hi.astype(F32)).astype(BF16)
    w_hi = w.astype(BF16)
    w_lo = (w - w_hi.astype(F32)).astype(BF16)
    acc = _dot(c_hi, w_hi) + (_dot(c_hi, w_lo) + _dot(c_lo, w_hi)) + b_ref[0]
    for r in range(MOD_PER_STEP):
        o_ref[r] = acc[:, r * d:(r + 1) * d]


def _mod_call(c, ada_w, ada_b):
    depth, d, _ = ada_w.shape
    b = c.shape[0]
    steps = N_MOD // MOD_PER_STEP
    return pl.pallas_call(
        functools.partial(_mod_kernel, d=d),
        grid=(depth, steps),
        in_specs=[
            pl.BlockSpec((b, d), lambda l, j: (0, 0)),
            pl.BlockSpec((1, d, MOD_PER_STEP * d), lambda l, j: (l, 0, j)),
            pl.BlockSpec((1, 1, MOD_PER_STEP * d), lambda l, j: (l, 0, j)),
        ],
        out_specs=pl.BlockSpec((MOD_PER_STEP, b, d), lambda l, j: (l * steps + j, 0, 0)),
        out_shape=jax.ShapeDtypeStruct((depth * N_MOD, b, d), F32),
        compiler_params=_params(),
        name="adaln_table",
    )(c, ada_w, ada_b.reshape(depth, 1, N_MOD * d))


def _ffn_kernel(x_ref, nw_ref, sh_ref, sc_ref, g_ref, w13_ref, w2_ref, *rest, d_ff, tm, sub, final):
    if final:
        fw_ref, o_ref = rest
    else:
        (o_ref,) = rest
    nw, sh, sc = nw_ref[...], sh_ref[0], sc_ref[0]
    half_g = 0.5 * g_ref[0]
    tiles = -(-d_ff // MXU_COLS)
    per = -(-tiles // FFN_COL_CHUNKS)
    bounds = [min(c * per * MXU_COLS, d_ff) for c in range(FFN_COL_CHUNKS + 1)]
    for r0 in range(0, tm, sub):
        x = x_ref[0, r0:r0 + sub, :]
        h = _norm_mod(x, nw, sh, sc).astype(BF16)
        y = None
        for c0, c1 in zip(bounds[:-1], bounds[1:]):
            gate = _dot(h, w13_ref[:, c0:c1])
            up = _dot(h, w13_ref[:, d_ff + c0:d_ff + c1])
            act = (gate * _sigmoid(gate) * up).astype(BF16)
            part = _dot(act, w2_ref[c0:c1, :])
            y = part if y is None else y + part
        out = x + half_g * y
        if final:
            out = _rmsnorm(out, fw_ref[...])
        o_ref[0, r0:r0 + sub, :] = out


def _ffn_call(x, nw, mods, base, w13, w2, layer, which, final_w=None):
    b, s, d = x.shape
    d_ff = w2.shape[2]
    tm = min(FFN_ROWS, s)
    sub = min(FFN_SUB, tm)
    assert tm % sub == 0 and d_ff % LANES == 0
    tok = pl.BlockSpec((1, tm, d), lambda i, t: (i, t, 0))
    vecs = _mod_specs(d, base, 1)
    pick = lambda i, t: (layer, which, 0, 0)
    w13_spec = pl.BlockSpec((None, None, d, 2 * d_ff), pick, pipeline_mode=pl.Buffered(1))
    w2_spec = pl.BlockSpec((None, None, d_ff, d), pick, pipeline_mode=pl.Buffered(1))
    in_specs = [tok, _resident((1, d)), *vecs, w13_spec, w2_spec]
    args = [x, nw.reshape(1, d), mods, mods, mods, w13, w2]
    if final_w is not None:
        in_specs.append(_resident((1, d)))
        args.append(final_w.reshape(1, d))
    return pl.pallas_call(
        functools.partial(_ffn_kernel, d_ff=d_ff, tm=tm, sub=sub, final=final_w is not None),
        grid=(b, s // tm),
        in_specs=in_specs,
        out_specs=tok,
        out_shape=jax.ShapeDtypeStruct(x.shape, F32),
        compiler_params=_params(),
        name="swiglu_final" if final_w is not None else "swiglu",
    )(*args)


def _gelu_tanh(x):
    return x * (0.5 * (1.0 + jnp.tanh(0.7978845608028654 * (x + 0.044715 * (x * x * x)))))


def _rg_kernel(x_ref, nw_ref, sh_ref, sc_ref, g_ref, perm_ref, permt_ref, win_ref, cw_ref, cb_ref, gw_ref, gb_ref,
               lam_ref, wout_ref, o_ref, conv_scr, h_scr, *, nb, ts, sub, width):
    t = pl.program_id(1)
    rows = nb * sub
    d = x_ref.shape[2]
    blk = width // RG_HEADS
    tail_rows = (CONV_WIDTH - 1) * nb

    @pl.when(t == 0)
    def _():
        conv_scr[...] = jnp.zeros_like(conv_scr)
        h_scr[...] = jnp.zeros_like(h_scr)

    nw, sh, sc, g = nw_ref[...], sh_ref[...], sc_ref[...], g_ref[...]
    cw, cb = cw_ref[...], cb_ref[...]
    decay = -RG_C * _softplus(-lam_ref[...])
    first_rows = lax.broadcasted_iota(jnp.int32, (rows, 1), 0) < nb
    nsub = ts // sub
    st = {"tail": conv_scr[...], "h": h_scr[...]}
    xs, ys, xbs, xcs, gxs, gas = {}, {}, {}, {}, {}, {}

    def project(si):
        x = x_ref[:, si * sub:(si + 1) * sub, :]
        h = _norm_mod(x, nw, sh, sc).reshape(rows, d).astype(BF16)
        hp = _dot(perm_ref[...], h).astype(BF16)
        yx = _dot(hp, win_ref[...])
        xs[si] = x
        ys[si] = _gelu_tanh(yx[:, :width])
        xbs[si] = yx[:, width:]

    def conv_gates(si):
        xb = xbs.pop(si)
        ext = jnp.concatenate([st["tail"], xb], axis=0)
        xc = cb
        for j in range(CONV_WIDTH):
            xc = xc + ext[j * nb:j * nb + rows] * cw[j:j + 1]
        st["tail"] = xb[rows - tail_rows:rows]
        xcb = xc.astype(BF16)
        gx, ga = [], []
        for hd in range(RG_HEADS):
            gts = _dot(xcb[:, hd * blk:(hd + 1) * blk], gw_ref[hd]) + gb_ref[hd]
            gx.append(gts[:, :blk])
            ga.append(gts[:, blk:])
        xcs[si] = xc
        gxs[si] = jnp.concatenate(gx, axis=1)
        gas[si] = jnp.concatenate(ga, axis=1)

    def recur_out(si):
        log_a = _sigmoid(gas.pop(si)) * decay
        a = jnp.exp(log_a)
        m2 = -jnp.tanh(log_a) * (a * a + 1.0)
        mult = jnp.where(m2 > 0.0, m2 * lax.rsqrt(m2), 0.0)
        if si == 0:
            mult = jnp.where(jnp.logical_and(first_rows, t == 0), 1.0, mult)
        u = _sigmoid(gxs.pop(si)) * xcs.pop(si) * mult
        hcar = st["h"]
        hs = []
        for ti in range(sub):
            hcar = a[ti * nb:(ti + 1) * nb] * hcar + u[ti * nb:(ti + 1) * nb]
            hs.append(hcar)
        st["h"] = hcar
        z = (jnp.concatenate(hs, axis=0) * ys.pop(si)).astype(BF16)
        zn = _dot(permt_ref[...], z).astype(BF16)
        out = _dot(zn, wout_ref[...]).reshape(nb, sub, d)
        o_ref[:, si * sub:(si + 1) * sub, :] = xs.pop(si) + g * out

    for step in range(nsub + 2):
        if step < nsub:
            project(step)
        if 0 <= step - 1 < nsub:
            conv_gates(step - 1)
        if 0 <= step - 2 < nsub:
            recur_out(step - 2)
    conv_scr[...] = st["tail"]
    h_scr[...] = st["h"]


def _rg_call(x, nw, mods, base, w_in, conv_w, conv_b, gate_w, gate_b, lam, w_out):
    b, s, d = x.shape
    width = w_out.shape[0]
    blk = width // RG_HEADS
    nb = SUBLANES
    ts = min(RG_STEPS, s)
    sub = min(RG_SUB, ts)
    rows = nb * sub
    assert b % nb == 0 and s % ts == 0 and ts % sub == 0 and sub % SUBLANES == 0
    r = jnp.arange(rows, dtype=jnp.int32)
    perm = ((r[:, None] % nb) * sub + r[:, None] // nb == r[None, :]).astype(BF16)
    tok = pl.BlockSpec((nb, ts, d), lambda i, t: (i, t, 0))
    vecs = _mod_specs(d, base, nb)
    return pl.pallas_call(
        functools.partial(_rg_kernel, nb=nb, ts=ts, sub=sub, width=width),
        grid=(b // nb, s // ts),
        in_specs=[tok, _resident((1, d)), *vecs, _resident((rows, rows)), _resident((rows, rows)),
                  _resident(w_in.shape), _resident((CONV_WIDTH, width)), _resident((1, width)),
                  _resident(gate_w.shape), _resident((RG_HEADS, 1, 2 * blk)), _resident((1, width)),
                  _resident(w_out.shape)],
        out_specs=tok,
        out_shape=jax.ShapeDtypeStruct(x.shape, F32),
        scratch_shapes=[pltpu.VMEM(((CONV_WIDTH - 1) * nb, width), F32), pltpu.VMEM((nb, width), F32)],
        compiler_params=_params(),
        name="rglru_mixer",
    )(x, nw.reshape(1, d), mods, mods, mods, perm, perm.T, w_in, conv_w, conv_b.reshape(1, width), gate_w,
      gate_b.reshape(RG_HEADS, 1, 2 * blk), lam.reshape(1, width), w_out)


def _cumsum_rows(v, rows):
    n = v.shape[1]
    groups = rows // SUBLANES
    v3 = v.reshape(groups, SUBLANES, n)
    srow = lax.broadcasted_iota(jnp.int32, v3.shape, 1)
    for k in (1, 2, 4):
        v3 = v3 + jnp.where(srow >= k, pltpu.roll(v3, k, 1), 0.0)
    out, carry = [], None
    for gi in range(groups):
        blk = v3[gi] if carry is None else v3[gi] + carry
        out.append(blk)
        carry = blk[SUBLANES - 1:SUBLANES, :]
    return jnp.concatenate(out, axis=0)


def _anchor(b, half, rows):
    n = b.shape[1]
    if 2 * half >= SUBLANES:
        v = b.reshape(rows // (2 * half), 2 * half, n)
        return jnp.broadcast_to(v[:, half - 1:half, :], v.shape).reshape(rows, n)
    v = b.reshape(rows // SUBLANES, SUBLANES, n)
    srow = lax.broadcasted_iota(jnp.int32, v.shape, 1)
    if half == 2:
        lo = jnp.broadcast_to(v[:, 1:2, :], v.shape)
        hi = jnp.broadcast_to(v[:, 5:6, :], v.shape)
        return jnp.where(srow < 4, lo, hi).reshape(rows, n)
    assert half == 1
    return jnp.where((srow & 1) == 1, pltpu.roll(v, 1, 1), v).reshape(rows, n)


def _block_diag(a, b):
    top = jnp.concatenate([a, jnp.zeros_like(b)], axis=1)
    bot = jnp.concatenate([jnp.zeros_like(a), b], axis=1)
    return jnp.concatenate([top, bot], axis=0)


def _gla_masks(rows):
    qi = lax.broadcasted_iota(jnp.int32, (rows, 2 * rows), 0)
    kj = lax.broadcasted_iota(jnp.int32, (rows, 2 * rows), 1) & (rows - 1)
    masks = []
    half, lvl = 1, 0
    while half < rows:
        same = (qi >> (lvl + 1)) == (kj >> (lvl + 1))
        masks.append(same & ((qi & half) != 0) & ((kj & half) == 0))
        half, lvl = half * 2, lvl + 1
    return qi == kj, masks


def _gla_prep(q, k, v, lf2, *, rows):
    b = _cumsum_rows(lf2, rows)
    b_last = b[rows - 1:rows, :]
    qe = (q * jnp.exp2(b)).astype(BF16)
    kd = (k * jnp.exp2(b_last - b)).astype(BF16)
    eb = jnp.exp2(b_last)
    vb = v.astype(BF16)
    rown = lax.broadcasted_iota(jnp.int32, b.shape, 0)
    mixed = []
    half = 1
    while half < rows:
        if half >= SUBLANES:
            parts = []
            for r0 in range(0, rows, half):
                if r0 & half:
                    parts.append(q[r0:r0 + half] * jnp.exp2(b[r0:r0 + half] - b[r0 - 1:r0]))
                else:
                    parts.append(k[r0:r0 + half] * jnp.exp2(b[r0 + half - 1:r0 + half] - b[r0:r0 + half]))
            mixed.append(jnp.concatenate(parts, axis=0).astype(BF16))
        else:
            e = jnp.exp2(_neg_abs(b - _anchor(b, half, rows)))
            mixed.append((jnp.where((rown & half) != 0, q, k) * e).astype(BF16))
        half *= 2
    return dict(qe=qe, kd=kd, eb=eb, vb=vb, mixed=mixed, qk=q * k)


def _gla_scores(ctx, st_ref, eye, masks, *, heads, dk, dv, rows):
    qe, kd, eb, vb, mixed, qk = (ctx[n] for n in ("qe", "kd", "eb", "vb", "mixed", "qk"))
    ks = [slice(hd * dk, (hd + 1) * dk) for hd in range(heads)]
    vs = [slice(hd * dv, (hd + 1) * dv) for hd in range(heads)]
    inter = []
    for hd in range(heads):
        st = st_ref[hd]
        inter.append(_dot_nt(qe[:, ks[hd]], st.astype(BF16)))
        st_ref[hd] = st * eb[:, ks[hd]] + _dot_tn(vb[:, vs[hd]], kd[:, ks[hd]])
    left = lax.broadcasted_iota(jnp.int32, (rows, 2 * rows), 1) < rows
    scores = []
    for h0 in range(0, heads, 2):
        kc = slice(h0 * dk, (h0 + 2) * dk)
        diag = jnp.where(left, jnp.sum(qk[:, ks[h0]], axis=-1, keepdims=True),
                         jnp.sum(qk[:, ks[h0 + 1]], axis=-1, keepdims=True))
        sc = jnp.where(eye, diag, 0.0)
        for m, mk in zip(mixed, masks):
            sc = jnp.where(mk, _dot_nt(m[:, kc], _block_diag(m[:, ks[h0]], m[:, ks[h0 + 1]])), sc)
        scores.append(sc.astype(BF16))
    ctx["inter"] = inter
    ctx["scores"] = scores


def _gla_out(ctx, *, heads, dv):
    vb, inter, scores = ctx["vb"], ctx["inter"], ctx["scores"]
    vs = [slice(hd * dv, (hd + 1) * dv) for hd in range(heads)]
    outs = []
    for h0 in range(0, heads, 2):
        o2 = _dot(scores[h0 // 2], _block_diag(vb[:, vs[h0]], vb[:, vs[h0 + 1]]))
        outs.append(o2 + jnp.concatenate([inter[h0], inter[h0 + 1]], axis=1))
    return jnp.concatenate(outs, axis=1)


def _gla_tile(n_chunks, load, store, st_ref, *, heads, dk, dv, rows):
    eye, masks = _gla_masks(rows)
    ctxs = {}
    for step in range(n_chunks + 2):
        if step < n_chunks:
            ctxs[step] = _gla_prep(*load(step), rows=rows)
        if 0 <= step - 1 < n_chunks:
            _gla_scores(ctxs[step - 1], st_ref, eye, masks, heads=heads, dk=dk, dv=dv, rows=rows)
        if 0 <= step - 2 < n_chunks:
            store(step - 2, _gla_out(ctxs.pop(step - 2), heads=heads, dv=dv))


def _hg_kernel(x_ref, nw_ref, sh_ref, sc_ref, g_ref, win_ref, lbl_ref, onw_ref, wout_ref, o_ref,
               p_scr, pv_scr, lf_scr, o_scr, st_scr, *, tt, d, layer, heads):
    t = pl.program_id(1)

    @pl.when(t == 0)
    def _():
        st_scr[...] = jnp.zeros_like(st_scr)

    x = x_ref[0]
    h = _norm_mod(x, nw_ref[...], sh_ref[0], sc_ref[0]).astype(BF16)
    p_scr[:, d:2 * d] = _dot(h, win_ref[:, d:2 * d])

    lg = lbl_ref[...]
    rows_l = [lg[i:i + 1, :] for i in range(lg.shape[0])]
    mx = functools.reduce(jnp.maximum, rows_l)
    ex = [jnp.exp(r - mx) for r in rows_l]
    den = functools.reduce(lambda u, w: u + w, ex)
    lb = jnp.zeros_like(den)
    for i in range(1, layer + 1):
        lb = lb + ex[i] / den
    log_lb = jnp.log(lb)
    log_1m = jnp.log1p(-lb)
    one_m = 1.0 - lb

    log_lb2 = log_lb * LOG2E
    log_1m2 = log_1m * LOG2E

    fz = p_scr[:, d:2 * d]
    fz2 = fz * LOG2E
    e = jnp.exp2(_neg_abs(fz2))
    u = 1.0 + e
    bv2 = (log_1m2 + jnp.minimum(fz2, 0.0)) - jnp.log2(u)
    lf_scr[...] = jnp.maximum(log_lb2, bv2) + jnp.log2(1.0 + jnp.exp2(_neg_abs(log_lb2 - bv2)))
    p_scr[:, d:2 * d] = one_m * (jnp.where(fz >= 0.0, e, 1.0) / u)
    p_scr[:, 0:d] = _dot(h, win_ref[:, 0:d])
    pv_scr[...] = _dot(h, win_ref[:, 2 * d:3 * d]).astype(BF16)
    p_scr[:, 2 * d:3 * d] = _dot(h, win_ref[:, 3 * d:4 * d])

    def load(ci):
        rs = slice(ci * GLA_CHUNK, (ci + 1) * GLA_CHUNK)
        return p_scr[rs, 0:d], p_scr[rs, d:2 * d], pv_scr[rs, :], lf_scr[rs, :]

    def store(ci, o):
        o_scr[ci * GLA_CHUNK:(ci + 1) * GLA_CHUNK, :] = o

    _gla_tile(tt // GLA_CHUNK, load, store, st_scr, heads=heads, dk=d // heads, dv=d // heads, rows=GLA_CHUNK)

    o = _rmsnorm(o_scr[...], onw_ref[...]) * _sigmoid(p_scr[:, 2 * d:3 * d])
    out = _dot(o.astype(BF16), wout_ref[...])
    o_ref[0] = x + g_ref[0] * out


def _hg_call(x, nw, mods, base, w_in, lb_logits, onw, w_out, layer):
    b, s, d = x.shape
    heads = d // HG_EXPAND
    tt = min(MIX_ROWS, s)
    tok = pl.BlockSpec((1, tt, d), lambda i, t: (i, t, 0))
    vecs = _mod_specs(d, base, 1)
    return pl.pallas_call(
        functools.partial(_hg_kernel, tt=tt, d=d, layer=layer, heads=heads),
        grid=(b, s // tt),
        in_specs=[tok, _resident((1, d)), *vecs, _resident(w_in.shape), _resident(lb_logits.shape),
                  _resident((1, d)), _resident(w_out.shape)],
        out_specs=tok,
        out_shape=jax.ShapeDtypeStruct(x.shape, F32),
        scratch_shapes=[pltpu.VMEM((tt, 3 * d), F32), pltpu.VMEM((tt, d), BF16), pltpu.VMEM((tt, d), F32),
                        pltpu.VMEM((tt, d), F32), pltpu.VMEM((heads, HG_EXPAND, HG_EXPAND), F32)],
        compiler_params=_params(),
        name="hgrn2_mixer",
    )(x, nw.reshape(1, d), mods, mods, mods, w_in, lb_logits, onw.reshape(1, d), w_out)


def _gla_kernel(x_ref, nw_ref, sh_ref, sc_ref, g_ref, win_ref, gw2_ref, gb_ref, onw_ref, wout_ref, o_ref,
                p_scr, pv_scr, lf_scr, o_scr, st_scr, *, tt, kd, vd, heads):
    t = pl.program_id(1)

    @pl.when(t == 0)
    def _():
        st_scr[...] = jnp.zeros_like(st_scr)

    x = x_ref[0]
    h = _norm_mod(x, nw_ref[...], sh_ref[0], sc_ref[0]).astype(BF16)
    c_v, c_low = 2 * kd, 2 * kd + 2 * vd
    a_low = _dot(h, win_ref[:, c_low:c_low + LANES]).astype(BF16)
    z2 = (_dot(a_low, gw2_ref[...]) + gb_ref[...]) * LOG2E
    lf_scr[...] = (jnp.minimum(z2, 0.0) - jnp.log2(1.0 + jnp.exp2(_neg_abs(z2)))) * (1.0 / GLA_LOGIT_NORM)
    p_scr[:, 0:c_v] = _dot(h, win_ref[:, 0:c_v])
    pv_scr[...] = _dot(h, win_ref[:, c_v:c_v + vd]).astype(BF16)
    p_scr[:, c_v:c_v + vd] = _dot(h, win_ref[:, c_v + vd:c_low])
    dk = kd // heads
    dv = vd // heads
    scale = dk ** -0.5

    def load(ci):
        rs = slice(ci * GLA_CHUNK, (ci + 1) * GLA_CHUNK)
        q = p_scr[rs, 0:kd] * scale
        k = p_scr[rs, kd:2 * kd]
        v = pv_scr[rs, :]
        return q, k, v, lf_scr[rs, :]

    def store(ci, o):
        o_scr[ci * GLA_CHUNK:(ci + 1) * GLA_CHUNK, :] = o

    _gla_tile(tt // GLA_CHUNK, load, store, st_scr, heads=heads, dk=dk, dv=dv, rows=GLA_CHUNK)

    gate = p_scr[:, 2 * kd:2 * kd + vd]
    onw = onw_ref[...]
    parts = []
    for hd in range(heads):
        vs = slice(hd * dv, (hd + 1) * dv)
        gh = gate[:, vs]
        parts.append(_rmsnorm(o_scr[:, vs], onw) * (gh * _sigmoid(gh)))
    out = _dot(jnp.concatenate(parts, axis=1).astype(BF16), wout_ref[...])
    o_ref[0] = x + g_ref[0] * out


def _gla_call(x, nw, mods, base, w_in, gate_w2, gate_b, onw, w_out):
    b, s, d = x.shape
    kd = gate_w2.shape[1]
    vd = w_out.shape[0]
    heads = GLA_HEADS
    dv = vd // heads
    tt = min(MIX_ROWS, s)
    cols = 2 * kd + 2 * vd + LANES
    w_in_p = jnp.zeros((d, cols), BF16).at[:, :w_in.shape[1]].set(w_in)
    gw2_p = jnp.zeros((LANES, kd), BF16).at[:GLA_RANK, :].set(gate_w2)
    tok = pl.BlockSpec((1, tt, d), lambda i, t: (i, t, 0))
    vecs = _mod_specs(d, base, 1)
    return pl.pallas_call(
        functools.partial(_gla_kernel, tt=tt, kd=kd, vd=vd, heads=heads),
        grid=(b, s // tt),
        in_specs=[tok, _resident((1, d)), *vecs, _resident(w_in_p.shape), _resident(gw2_p.shape),
                  _resident((1, kd)), _resident((1, dv)), _resident(w_out.shape)],
        out_specs=tok,
        out_shape=jax.ShapeDtypeStruct(x.shape, F32),
        scratch_shapes=[pltpu.VMEM((tt, 2 * kd + vd), F32), pltpu.VMEM((tt, vd), BF16), pltpu.VMEM((tt, kd), F32),
                        pltpu.VMEM((tt, vd), F32),
                        pltpu.VMEM((heads, dv, kd // heads), F32)],
        compiler_params=_params(),
        name="gla_mixer",
    )(x, nw.reshape(1, d), mods, mods, mods, w_in_p, gw2_p, gate_b.reshape(1, kd), onw.reshape(1, dv), w_out)


def kernel(x, c, ada_w, ada_b, norm_w, final_norm_w, ffn_w13, ffn_w2, rg_w_in, rg_conv_w, rg_conv_b, rg_gate_w,
           rg_gate_b, rg_lambda, rg_w_out, hg_w_in, hg_lb_logits, hg_norm_w, hg_w_out, gla_w_in, gla_gate_w2,
           gla_gate_b, gla_norm_w, gla_w_out):
    depth = ada_w.shape[0]
    b, s, d = x.shape
    assert s % min(FFN_ROWS, s) == 0 and s % min(MIX_ROWS, s) == 0 and min(MIX_ROWS, s) % GLA_CHUNK == 0
    mods = _mod_call(c, ada_w, ada_b).reshape(depth * N_MOD, b, 1, d)
    bf = lambda w: w.astype(BF16)
    w13b, w2b = bf(ffn_w13), bf(ffn_w2)
    i_rg = i_hg = i_gla = 0
    for l in range(depth):
        base = l * N_MOD
        x = _ffn_call(x, norm_w[l, 0], mods, base, w13b, w2b, l, 0)
        m = l % N_MIXERS
        if m == 0:
            x = _rg_call(x, norm_w[l, 1], mods, base + 3, bf(rg_w_in[i_rg]), rg_conv_w[i_rg], rg_conv_b[i_rg],
                         bf(rg_gate_w[i_rg]), rg_gate_b[i_rg], rg_lambda[i_rg], bf(rg_w_out[i_rg]))
            i_rg += 1
        elif m == 1:
            x = _hg_call(x, norm_w[l, 1], mods, base + 3, bf(hg_w_in[i_hg]), hg_lb_logits, hg_norm_w[i_hg],
                         bf(hg_w_out[i_hg]), l)
            i_hg += 1
        else:
            x = _gla_call(x, norm_w[l, 1], mods, base + 3, bf(gla_w_in[i_gla]), bf(gla_gate_w2[i_gla]),
                          gla_gate_b[i_gla], gla_norm_w[i_gla], bf(gla_w_out[i_gla]))
            i_gla += 1
        x = _ffn_call(x, norm_w[l, 2], mods, base + 6, w13b, w2b, l, 1,
                      final_w=final_norm_w if l == depth - 1 else None)
    return x
```

```python
import functools

import jax
import jax.numpy as jnp
from jax import lax
from jax.experimental import pallas as pl
from jax.experimental.pallas import tpu as pltpu

F32 = jnp.float32
BF16 = jnp.bfloat16

N_MIXERS = 3
N_MOD = 9
EPS = 1e-6
RG_HEADS = 8
CONV_WIDTH = 4
RG_C = 8.0
HG_EXPAND = 128
GLA_HEADS = 4
GLA_RANK = 16
GLA_LOGIT_NORM = 16.0
LOG2E = 1.4426950408889634

LANES = 128
SUBLANES = 8
MXU_COLS = 256
VMEM_LIMIT_BYTES = 56 * 1024 * 1024

MOD_PER_STEP = 3
FFN_ROWS = 1024
FFN_SUB = 512
FFN_COL_CHUNKS = 2
MIX_ROWS = 512
RG_STEPS = 128
RG_SUB = 32
GLA_CHUNK = 64


def _params():
    return pltpu.CompilerParams(dimension_semantics=("arbitrary", "arbitrary"),
                                vmem_limit_bytes=VMEM_LIMIT_BYTES)


def _resident(shape):
    zeros = (0,) * len(shape)
    return pl.BlockSpec(shape, lambda *_: zeros, pipeline_mode=pl.Buffered(1))


def _mod_specs(d, base, seqs):
    return [pl.BlockSpec((None, seqs, 1, d), lambda i, t, row=base + j: (row, i, 0, 0)) for j in range(3)]


def _dot(a, b):
    return jnp.dot(a, b, preferred_element_type=F32)


def _dot_nt(a, b):
    return lax.dot_general(a, b, (((1,), (1,)), ((), ())), preferred_element_type=F32)


def _dot_tn(a, b):
    return lax.dot_general(a, b, (((0,), (0,)), ((), ())), preferred_element_type=F32)


def _neg_abs(x):
    bits = lax.bitcast_convert_type(x, jnp.uint32) | jnp.uint32(0x80000000)
    return lax.bitcast_convert_type(bits, F32)


def _sigmoid(x):
    return 1.0 / (1.0 + jnp.exp(-x))


def _softplus(x):
    return jnp.maximum(x, 0.0) + jnp.log1p(jnp.exp(-jnp.abs(x)))


def _rmsnorm(x, w):
    ms = jnp.mean(x * x, axis=-1, keepdims=True)
    return (x * lax.rsqrt(ms + EPS)) * w


def _norm_mod(x, nw, shift, scale):
    return _rmsnorm(x, nw) * (1.0 + scale) + shift


def _mod_kernel(c_ref, w_ref, b_ref, o_ref, *, d):
    c = c_ref[...]
    ca = c * _sigmoid(c)
    w = w_ref[0]
    c_hi = ca.astype(BF16)
    c_lo = (ca - c_hi.astype(F32)).astype(BF16)
    w_hi = w.astype(BF16)
    w_lo = (w - w_hi.astype(F32)).astype(BF16)
    acc = _dot(c_hi, w_hi) + (_dot(c_hi, w_lo) + _dot(c_lo, w_hi)) + b_ref[0]
    for r in range(MOD_PER_STEP):
        o_ref[r] = acc[:, r * d:(r + 1) * d]


def _mod_call(c, ada_w, ada_b):
    depth, d, _ = ada_w.shape
    b = c.shape[0]
    steps = N_MOD // MOD_PER_STEP
    return pl.pallas_call(
        functools.partial(_mod_kernel, d=d),
        grid=(depth, steps),
        in_specs=[
            pl.BlockSpec((b, d), lambda l, j: (0, 0)),
            pl.BlockSpec((1, d, MOD_PER_STEP * d), lambda l, j: (l, 0, j)),
            pl.BlockSpec((1, 1, MOD_PER_STEP * d), lambda l, j: (l, 0, j)),
        ],
        out_specs=pl.BlockSpec((MOD_PER_STEP, b, d), lambda l, j: (l * steps + j, 0, 0)),
        out_shape=jax.ShapeDtypeStruct((depth * N_MOD, b, d), F32),
        compiler_params=_params(),
        name="adaln_table",
    )(c, ada_w, ada_b.reshape(depth, 1, N_MOD * d))


def _ffn_kernel(x_ref, nw_ref, sh_ref, sc_ref, g_ref, w13_ref, w2_ref, *rest, d_ff, tm, sub, final):
    if final:
        fw_ref, o_ref = rest
    else:
        (o_ref,) = rest
    nw, sh, sc = nw_ref[...], sh_ref[0], sc_ref[0]
    half_g = 0.5 * g_ref[0]
    tiles = -(-d_ff // MXU_COLS)
    per = -(-tiles // FFN_COL_CHUNKS)
    bounds = [min(c * per * MXU_COLS, d_ff) for c in range(FFN_COL_CHUNKS + 1)]
    for r0 in range(0, tm, sub):
        x = x_ref[0, r0:r0 + sub, :]
        h = _norm_mod(x, nw, sh, sc).astype(BF16)
        y = None
        for c0, c1 in zip(bounds[:-1], bounds[1:]):
            gate = _dot(h, w13_ref[:, c0:c1])
            up = _dot(h, w13_ref[:, d_ff + c0:d_ff + c1])
            act = (gate * _sigmoid(gate) * up).astype(BF16)
            part = _dot(act, w2_ref[c0:c1, :])
            y = part if y is None else y + part
        out = x + half_g * y
        if final:
            out = _rmsnorm(out, fw_ref[...])
        o_ref[0, r0:r0 + sub, :] = out


def _ffn_call(x, nw, mods, base, w13, w2, layer, which, final_w=None):
    b, s, d = x.shape
    d_ff = w2.shape[2]
    tm = min(FFN_ROWS, s)
    sub = min(FFN_SUB, tm)
    assert tm % sub == 0 and d_ff % LANES == 0
    tok = pl.BlockSpec((1, tm, d), lambda i, t: (i, t, 0))
    vecs = _mod_specs(d, base, 1)
    pick = lambda i, t: (layer, which, 0, 0)
    w13_spec = pl.BlockSpec((None, None, d, 2 * d_ff), pick, pipeline_mode=pl.Buffered(1))
    w2_spec = pl.BlockSpec((None, None, d_ff, d), pick, pipeline_mode=pl.Buffered(1))
    in_specs = [tok, _resident((1, d)), *vecs, w13_spec, w2_spec]
    args = [x, nw.reshape(1, d), mods, mods, mods, w13, w2]
    if final_w is not None:
        in_specs.append(_resident((1, d)))
        args.append(final_w.reshape(1, d))
    return pl.pallas_call(
        functools.partial(_ffn_kernel, d_ff=d_ff, tm=tm, sub=sub, final=final_w is not None),
        grid=(b, s // tm),
        in_specs=in_specs,
        out_specs=tok,
        out_shape=jax.ShapeDtypeStruct(x.shape, F32),
        compiler_params=_params(),
        name="swiglu_final" if final_w is not None else "swiglu",
    )(*args)


def _gelu_tanh(x):
    return x * (0.5 * (1.0 + jnp.tanh(0.7978845608028654 * (x + 0.044715 * (x * x * x)))))


def _rg_kernel(x_ref, nw_ref, sh_ref, sc_ref, g_ref, perm_ref, permt_ref, win_ref, cw_ref, cb_ref, gw_ref, gb_ref,
               lam_ref, wout_ref, o_ref, conv_scr, h_scr, *, nb, ts, sub, width):
    t = pl.program_id(1)
    rows = nb * sub
    d = x_ref.shape[2]
    blk = width // RG_HEADS
    tail_rows = (CONV_WIDTH - 1) * nb

    @pl.when(t == 0)
    def _():
        conv_scr[...] = jnp.zeros_like(conv_scr)
        h_scr[...] = jnp.zeros_like(h_scr)

    nw, sh, sc, g = nw_ref[...], sh_ref[...], sc_ref[...], g_ref[...]
    cw, cb = cw_ref[...], cb_ref[...]
    decay = -RG_C * _softplus(-lam_ref[...])
    first_rows = lax.broadcasted_iota(jnp.int32, (rows, 1), 0) < nb
    nsub = ts // sub
    st = {"tail": conv_scr[...], "h": h_scr[...]}
    xs, ys, xbs, xcs, gxs, gas, zs = {}, {}, {}, {}, {}, {}, {}

    def project(si):
        x = x_ref[:, si * sub:(si + 1) * sub, :]
        h = _norm_mod(x, nw, sh, sc).reshape(rows, d).astype(BF16)
        hp = _dot(perm_ref[...], h).astype(BF16)
        yx = _dot(hp, win_ref[...])
        xs[si] = x
        ys[si] = _gelu_tanh(yx[:, :width])
        xbs[si] = yx[:, width:]

    def conv_gates(si):
        xb = xbs.pop(si)
        ext = jnp.concatenate([st["tail"], xb], axis=0)
        xc = cb
        for j in range(CONV_WIDTH):
            xc = xc + ext[j * nb:j * nb + rows] * cw[j:j + 1]
        st["tail"] = xb[rows - tail_rows:rows]
        xcb = xc.astype(BF16)
        gx, ga = [], []
        for hd in range(RG_HEADS):
            gts = _dot(xcb[:, hd * blk:(hd + 1) * blk], gw_ref[hd]) + gb_ref[hd]
            gx.append(gts[:, :blk])
            ga.append(gts[:, blk:])
        xcs[si] = xc
        gxs[si] = jnp.concatenate(gx, axis=1)
        gas[si] = jnp.concatenate(ga, axis=1)

    def recur_out(si):
        log_a = _sigmoid(gas.pop(si)) * decay
        a = jnp.exp(log_a)
        m2 = -jnp.tanh(log_a) * (a * a + 1.0)
        mult = jnp.where(m2 > 0.0, m2 * lax.rsqrt(m2), 0.0)
        if si == 0:
            mult = jnp.where(jnp.logical_and(first_rows, t == 0), 1.0, mult)
        u = _sigmoid(gxs.pop(si)) * xcs.pop(si) * mult
        hcar = st["h"]
        hs = []
        for ti in range(sub):
            hcar = a[ti * nb:(ti + 1) * nb] * hcar + u[ti * nb:(ti + 1) * nb]
            hs.append(hcar)
        st["h"] = hcar
        zs[si] = (jnp.concatenate(hs, axis=0) * ys.pop(si)).astype(BF16)

    def project_out(si):
        zn = _dot(permt_ref[...], zs.pop(si)).astype(BF16)
        out = _dot(zn, wout_ref[...]).reshape(nb, sub, d)
        o_ref[:, si * sub:(si + 1) * sub, :] = xs.pop(si) + g * out

    for step in range(nsub + 3):
        if step < nsub:
            project(step)
        if 0 <= step - 1 < nsub:
            conv_gates(step - 1)
        if 0 <= step - 2 < nsub:
            recur_out(step - 2)
        if 0 <= step - 3 < nsub:
            project_out(step - 3)
    conv_scr[...] = st["tail"]
    h_scr[...] = st["h"]


def _rg_call(x, nw, mods, base, w_in, conv_w, conv_b, gate_w, gate_b, lam, w_out):
    b, s, d = x.shape
    width = w_out.shape[0]
    blk = width // RG_HEADS
    nb = SUBLANES
    ts = min(RG_STEPS, s)
    sub = min(RG_SUB, ts)
    rows = nb * sub
    assert b % nb == 0 and s % ts == 0 and ts % sub == 0 and sub % SUBLANES == 0
    r = jnp.arange(rows, dtype=jnp.int32)
    perm = ((r[:, None] % nb) * sub + r[:, None] // nb == r[None, :]).astype(BF16)
    tok = pl.BlockSpec((nb, ts, d), lambda i, t: (i, t, 0))
    vecs = _mod_specs(d, base, nb)
    return pl.pallas_call(
        functools.partial(_rg_kernel, nb=nb, ts=ts, sub=sub, width=width),
        grid=(b // nb, s // ts),
        in_specs=[tok, _resident((1, d)), *vecs, _resident((rows, rows)), _resident((rows, rows)),
                  _resident(w_in.shape), _resident((CONV_WIDTH, width)), _resident((1, width)),
                  _resident(gate_w.shape), _resident((RG_HEADS, 1, 2 * blk)), _resident((1, width)),
                  _resident(w_out.shape)],
        out_specs=tok,
        out_shape=jax.ShapeDtypeStruct(x.shape, F32),
        scratch_shapes=[pltpu.VMEM(((CONV_WIDTH - 1) * nb, width), F32), pltpu.VMEM((nb, width), F32)],
        compiler_params=_params(),
        name="rglru_mixer",
    )(x, nw.reshape(1, d), mods, mods, mods, perm, perm.T, w_in, conv_w, conv_b.reshape(1, width), gate_w,
      gate_b.reshape(RG_HEADS, 1, 2 * blk), lam.reshape(1, width), w_out)


def _cumsum_rows(v, rows):
    n = v.shape[1]
    groups = rows // SUBLANES
    v3 = v.reshape(groups, SUBLANES, n)
    srow = lax.broadcasted_iota(jnp.int32, v3.shape, 1)
    for k in (1, 2, 4):
        v3 = v3 + jnp.where(srow >= k, pltpu.roll(v3, k, 1), 0.0)
    out, carry = [], None
    for gi in range(groups):
        blk = v3[gi] if carry is None else v3[gi] + carry
        out.append(blk)
        carry = blk[SUBLANES - 1:SUBLANES, :]
    return jnp.concatenate(out, axis=0)


def _anchor(b, half, rows):
    n = b.shape[1]
    if 2 * half >= SUBLANES:
        v = b.reshape(rows // (2 * half), 2 * half, n)
        return jnp.broadcast_to(v[:, half - 1:half, :], v.shape).reshape(rows, n)
    v = b.reshape(rows // SUBLANES, SUBLANES, n)
    srow = lax.broadcasted_iota(jnp.int32, v.shape, 1)
    if half == 2:
        lo = jnp.broadcast_to(v[:, 1:2, :], v.shape)
        hi = jnp.broadcast_to(v[:, 5:6, :], v.shape)
        return jnp.where(srow < 4, lo, hi).reshape(rows, n)
    assert half == 1
    return jnp.where((srow & 1) == 1, pltpu.roll(v, 1, 1), v).reshape(rows, n)


def _block_diag(a, b):
    top = jnp.concatenate([a, jnp.zeros_like(b)], axis=1)
    bot = jnp.concatenate([jnp.zeros_like(a), b], axis=1)
    return jnp.concatenate([top, bot], axis=0)


def _gla_masks(rows):
    qi = lax.broadcasted_iota(jnp.int32, (rows, 2 * rows), 0)
    kj = lax.broadcasted_iota(jnp.int32, (rows, 2 * rows), 1) & (rows - 1)
    masks = []
    half, lvl = 1, 0
    while half < rows:
        same = (qi >> (lvl + 1)) == (kj >> (lvl + 1))
        masks.append(same & ((qi & half) != 0) & ((kj & half) == 0))
        half, lvl = half * 2, lvl + 1
    return qi == kj, masks


def _gla_prep(q, k, v, lf2, *, rows):
    b = _cumsum_rows(lf2, rows)
    b_last = b[rows - 1:rows, :]
    qe = (q * jnp.exp2(b)).astype(BF16)
    kd = (k * jnp.exp2(b_last - b)).astype(BF16)
    eb = jnp.exp2(b_last)
    vb = v.astype(BF16)
    rown = lax.broadcasted_iota(jnp.int32, b.shape, 0)
    mixed = []
    half = 1
    while half < rows:
        if half >= SUBLANES:
            parts = []
            for r0 in range(0, rows, half):
                if r0 & half:
                    parts.append(q[r0:r0 + half] * jnp.exp2(b[r0:r0 + half] - b[r0 - 1:r0]))
                else:
                    parts.append(k[r0:r0 + half] * jnp.exp2(b[r0 + half - 1:r0 + half] - b[r0:r0 + half]))
            mixed.append(jnp.concatenate(parts, axis=0).astype(BF16))
        else:
            e = jnp.exp2(_neg_abs(b - _anchor(b, half, rows)))
            mixed.append((jnp.where((rown & half) != 0, q, k) * e).astype(BF16))
        half *= 2
    return dict(qe=qe, kd=kd, eb=eb, vb=vb, mixed=mixed, qk=q * k)


def _gla_scores(ctx, st_ref, eye, masks, *, heads, dk, dv, rows):
    qe, kd, eb, vb, mixed, qk = (ctx[n] for n in ("qe", "kd", "eb", "vb", "mixed", "qk"))
    ks = [slice(hd * dk, (hd + 1) * dk) for hd in range(heads)]
    vs = [slice(hd * dv, (hd + 1) * dv) for hd in range(heads)]
    left = lax.broadcasted_iota(jnp.int32, (rows, 2 * rows), 1) < rows
    scores = []
    for h0 in range(0, heads, 2):
        kc = slice(h0 * dk, (h0 + 2) * dk)
        diag = jnp.where(left, jnp.sum(qk[:, ks[h0]], axis=-1, keepdims=True),
                         jnp.sum(qk[:, ks[h0 + 1]], axis=-1, keepdims=True))
        sc = jnp.where(eye, diag, 0.0)
        for m, mk in zip(mixed, masks):
            sc = jnp.where(mk, _dot_nt(m[:, kc], _block_diag(m[:, ks[h0]], m[:, ks[h0 + 1]])), sc)
        scores.append(sc.astype(BF16))
    inter = []
    for hd in range(heads):
        st = st_ref[hd]
        inter.append(_dot_nt(qe[:, ks[hd]], st.astype(BF16)))
        st_ref[hd] = st * eb[:, ks[hd]] + _dot_tn(vb[:, vs[hd]], kd[:, ks[hd]])
    ctx["inter"] = inter
    ctx["scores"] = scores


def _gla_out(ctx, *, heads, dv):
    vb, inter, scores = ctx["vb"], ctx["inter"], ctx["scores"]
    vs = [slice(hd * dv, (hd + 1) * dv) for hd in range(heads)]
    outs = []
    for h0 in range(0, heads, 2):
        o2 = _dot(scores[h0 // 2], _block_diag(vb[:, vs[h0]], vb[:, vs[h0 + 1]]))
        outs.append(o2 + jnp.concatenate([inter[h0], inter[h0 + 1]], axis=1))
    return jnp.concatenate(outs, axis=1)


def _gla_tile(n_chunks, load, store, st_ref, *, heads, dk, dv, rows):
    eye, masks = _gla_masks(rows)
    ctxs = {}
    for step in range(n_chunks + 2):
        if step < n_chunks:
            ctxs[step] = _gla_prep(*load(step), rows=rows)
        if 0 <= step - 1 < n_chunks:
            _gla_scores(ctxs[step - 1], st_ref, eye, masks, heads=heads, dk=dk, dv=dv, rows=rows)
        if 0 <= step - 2 < n_chunks:
            store(step - 2, _gla_out(ctxs.pop(step - 2), heads=heads, dv=dv))


def _hg_kernel(x_ref, nw_ref, sh_ref, sc_ref, g_ref, win_ref, lbl_ref, onw_ref, wout_ref, o_ref,
               p_scr, pv_scr, lf_scr, o_scr, st_scr, *, tt, d, layer, heads):
    t = pl.program_id(1)

    @pl.when(t == 0)
    def _():
        st_scr[...] = jnp.zeros_like(st_scr)

    x = x_ref[0]
    h = _norm_mod(x, nw_ref[...], sh_ref[0], sc_ref[0]).astype(BF16)
    p_scr[:, d:2 * d] = _dot(h, win_ref[:, d:2 * d])

    lg = lbl_ref[...]
    rows_l = [lg[i:i + 1, :] for i in range(lg.shape[0])]
    mx = functools.reduce(jnp.maximum, rows_l)
    ex = [jnp.exp(r - mx) for r in rows_l]
    den = functools.reduce(lambda u, w: u + w, ex)
    lb = jnp.zeros_like(den)
    for i in range(1, layer + 1):
        lb = lb + ex[i] / den
    log_lb = jnp.log(lb)
    log_1m = jnp.log1p(-lb)
    one_m = 1.0 - lb

    log_lb2 = log_lb * LOG2E
    log_1m2 = log_1m * LOG2E

    fz = p_scr[:, d:2 * d]
    fz2 = fz * LOG2E
    e = jnp.exp2(_neg_abs(fz2))
    u = 1.0 + e
    bv2 = (log_1m2 + jnp.minimum(fz2, 0.0)) - jnp.log2(u)
    lf_scr[...] = jnp.maximum(log_lb2, bv2) + jnp.log2(1.0 + jnp.exp2(_neg_abs(log_lb2 - bv2)))
    p_scr[:, d:2 * d] = one_m * (jnp.where(fz >= 0.0, e, 1.0) / u)
    p_scr[:, 0:d] = _dot(h, win_ref[:, 0:d])
    pv_scr[...] = _dot(h, win_ref[:, 2 * d:3 * d]).astype(BF16)
    p_scr[:, 2 * d:3 * d] = _dot(h, win_ref[:, 3 * d:4 * d])

    def load(ci):
        rs = slice(ci * GLA_CHUNK, (ci + 1) * GLA_CHUNK)
        return p_scr[rs, 0:d], p_scr[rs, d:2 * d], pv_scr[rs, :], lf_scr[rs, :]

    def store(ci, o):
        o_scr[ci * GLA_CHUNK:(ci + 1) * GLA_CHUNK, :] = o

    _gla_tile(tt // GLA_CHUNK, load, store, st_scr, heads=heads, dk=d // heads, dv=d // heads, rows=GLA_CHUNK)

    o = _rmsnorm(o_scr[...], onw_ref[...]) * _sigmoid(p_scr[:, 2 * d:3 * d])
    out = _dot(o.astype(BF16), wout_ref[...])
    o_ref[0] = x + g_ref[0] * out


def _hg_call(x, nw, mods, base, w_in, lb_logits, onw, w_out, layer):
    b, s, d = x.shape
    heads = d // HG_EXPAND
    tt = min(MIX_ROWS, s)
    tok = pl.BlockSpec((1, tt, d), lambda i, t: (i, t, 0))
    vecs = _mod_specs(d, base, 1)
    return pl.pallas_call(
        functools.partial(_hg_kernel, tt=tt, d=d, layer=layer, heads=heads),
        grid=(b, s // tt),
        in_specs=[tok, _resident((1, d)), *vecs, _resident(w_in.shape), _resident(lb_logits.shape),
                  _resident((1, d)), _resident(w_out.shape)],
        out_specs=tok,
        out_shape=jax.ShapeDtypeStruct(x.shape, F32),
        scratch_shapes=[pltpu.VMEM((tt, 3 * d), F32), pltpu.VMEM((tt, d), BF16), pltpu.VMEM((tt, d), F32),
                        pltpu.VMEM((tt, d), F32), pltpu.VMEM((heads, HG_EXPAND, HG_EXPAND), F32)],
        compiler_params=_params(),
        name="hgrn2_mixer",
    )(x, nw.reshape(1, d), mods, mods, mods, w_in, lb_logits, onw.reshape(1, d), w_out)


def _gla_kernel(x_ref, nw_ref, sh_ref, sc_ref, g_ref, win_ref, gw2_ref, gb_ref, onw_ref, wout_ref, o_ref,
                p_scr, pv_scr, lf_scr, o_scr, st_scr, *, tt, kd, vd, heads):
    t = pl.program_id(1)

    @pl.when(t == 0)
    def _():
        st_scr[...] = jnp.zeros_like(st_scr)

    x = x_ref[0]
    h = _norm_mod(x, nw_ref[...], sh_ref[0], sc_ref[0]).astype(BF16)
    c_v, c_low = 2 * kd, 2 * kd + 2 * vd
    a_low = _dot(h, win_ref[:, c_low:c_low + LANES]).astype(BF16)
    z2 = (_dot(a_low, gw2_ref[...]) + gb_ref[...]) * LOG2E
    lf_scr[...] = (jnp.minimum(z2, 0.0) - jnp.log2(1.0 + jnp.exp2(_neg_abs(z2)))) * (1.0 / GLA_LOGIT_NORM)
    p_scr[:, 0:c_v] = _dot(h, win_ref[:, 0:c_v])
    pv_scr[...] = _dot(h, win_ref[:, c_v:c_v + vd]).astype(BF16)
    p_scr[:, c_v:c_v + vd] = _dot(h, win_ref[:, c_v + vd:c_low])
    dk = kd // heads
    dv = vd // heads
    scale = dk ** -0.5

    def load(ci):
        rs = slice(ci * GLA_CHUNK, (ci + 1) * GLA_CHUNK)
        q = p_scr[rs, 0:kd] * scale
        k = p_scr[rs, kd:2 * kd]
        v = pv_scr[rs, :]
        return q, k, v, lf_scr[rs, :]

    def store(ci, o):
        o_scr[ci * GLA_CHUNK:(ci + 1) * GLA_CHUNK, :] = o

    _gla_tile(tt // GLA_CHUNK, load, store, st_scr, heads=heads, dk=dk, dv=dv, rows=GLA_CHUNK)

    gate = p_scr[:, 2 * kd:2 * kd + vd]
    onw = onw_ref[...]
    parts = []
    for hd in range(heads):
        vs = slice(hd * dv, (hd + 1) * dv)
        gh = gate[:, vs]
        parts.append(_rmsnorm(o_scr[:, vs], onw) * (gh * _sigmoid(gh)))
    out = _dot(jnp.concatenate(parts, axis=1).astype(BF16), wout_ref[...])
    o_ref[0] = x + g_ref[0] * out


def _gla_call(x, nw, mods, base, w_in, gate_w2, gate_b, onw, w_out):
    b, s, d = x.shape
    kd = gate_w2.shape[1]
    vd = w_out.shape[0]
    heads = GLA_HEADS
    dv = vd // heads
    tt = min(MIX_ROWS, s)
    cols = 2 * kd + 2 * vd + LANES
    w_in_p = jnp.zeros((d, cols), BF16).at[:, :w_in.shape[1]].set(w_in)
    gw2_p = jnp.zeros((LANES, kd), BF16).at[:GLA_RANK, :].set(gate_w2)
    tok = pl.BlockSpec((1, tt, d), lambda i, t: (i, t, 0))
    vecs = _mod_specs(d, base, 1)
    return pl.pallas_call(
        functools.partial(_gla_kernel, tt=tt, kd=kd, vd=vd, heads=heads),
        grid=(b, s // tt),
        in_specs=[tok, _resident((1, d)), *vecs, _resident(w_in_p.shape), _resident(gw2_p.shape),
                  _resident((1, kd)), _resident((1, dv)), _resident(w_out.shape)],
        out_specs=tok,
        out_shape=jax.ShapeDtypeStruct(x.shape, F32),
        scratch_shapes=[pltpu.VMEM((tt, 2 * kd + vd), F32), pltpu.VMEM((tt, vd), BF16), pltpu.VMEM((tt, kd), F32),
                        pltpu.VMEM((tt, vd), F32),
                        pltpu.VMEM((heads, dv, kd // heads), F32)],
        compiler_params=_params(),
        name="gla_mixer",
    )(x, nw.reshape(1, d), mods, mods, mods, w_in_p, gw2_p, gate_b.reshape(1, kd), onw.reshape(1, dv), w_out)


def kernel(x, c, ada_w, ada_b, norm_w, final_norm_w, ffn_w13, ffn_w2, rg_w_in, rg_conv_w, rg_conv_b, rg_gate_w,
           rg_gate_b, rg_lambda, rg_w_out, hg_w_in, hg_lb_logits, hg_norm_w, hg_w_out, gla_w_in, gla_gate_w2,
           gla_gate_b, gla_norm_w, gla_w_out):
    depth = ada_w.shape[0]
    b, s, d = x.shape
    assert s % min(FFN_ROWS, s) == 0 and s % min(MIX_ROWS, s) == 0 and min(MIX_ROWS, s) % GLA_CHUNK == 0
    mods = _mod_call(c, ada_w, ada_b).reshape(depth * N_MOD, b, 1, d)
    bf = lambda w: w.astype(BF16)
    w13b, w2b = bf(ffn_w13), bf(ffn_w2)
    i_rg = i_hg = i_gla = 0
    for l in range(depth):
        base = l * N_MOD
        x = _ffn_call(x, norm_w[l, 0], mods, base, w13b, w2b, l, 0)
        m = l % N_MIXERS
        if m == 0:
            x = _rg_call(x, norm_w[l, 1], mods, base + 3, bf(rg_w_in[i_rg]), rg_conv_w[i_rg], rg_conv_b[i_rg],
                         bf(rg_gate_w[i_rg]), rg_gate_b[i_rg], rg_lambda[i_rg], bf(rg_w_out[i_rg]))
            i_rg += 1
        elif m == 1:
            x = _hg_call(x, norm_w[l, 1], mods, base + 3, bf(hg_w_in[i_hg]), hg_lb_logits, hg_norm_w[i_hg],
                         bf(hg_w_out[i_hg]), l)
            i_hg += 1
        else:
            x = _gla_call(x, norm_w[l, 1], mods, base + 3, bf(gla_w_in[i_gla]), bf(gla_gate_w2[i_gla]),
                          gla_gate_b[i_gla], gla_norm_w[i_gla], bf(gla_w_out[i_gla]))
            i_gla += 1
        x = _ffn_call(x, norm_w[l, 2], mods, base + 6, w13b, w2b, l, 1,
                      final_w=final_norm_w if l == depth - 1 else None)
    return x
```

```python
import functools

import jax
import jax.numpy as jnp
from jax import lax
from jax.experimental import pallas as pl
from jax.experimental.pallas import tpu as pltpu

F32 = jnp.float32
BF16 = jnp.bfloat16

N_MIXERS = 3
N_MOD = 9
EPS = 1e-6
RG_HEADS = 8
CONV_WIDTH = 4
RG_C = 8.0
HG_EXPAND = 128
GLA_HEADS = 4
GLA_RANK = 16
GLA_LOGIT_NORM = 16.0
LOG2E = 1.4426950408889634

LANES = 128
SUBLANES = 8
MXU_COLS = 256
VMEM_LIMIT_BYTES = 56 * 1024 * 1024

MOD_PER_STEP = 3
FFN_ROWS = 1024
FFN_SUB = 512
FFN_COL_CHUNKS = 2
MIX_ROWS = 512
RG_STEPS = 128
RG_SUB = 32
GLA_CHUNK = 64


def _params():
    return pltpu.CompilerParams(dimension_semantics=("arbitrary", "arbitrary"),
                                vmem_limit_bytes=VMEM_LIMIT_BYTES)


def _resident(shape):
    zeros = (0,) * len(shape)
    return pl.BlockSpec(shape, lambda *_: zeros, pipeline_mode=pl.Buffered(1))


def _mod_specs(d, base, seqs):
    return [pl.BlockSpec((None, seqs, 1, d), lambda i, t, row=base + j: (row, i, 0, 0)) for j in range(3)]


def _dot(a, b):
    return jnp.dot(a, b, preferred_element_type=F32)


def _dot_nt(a, b):
    return lax.dot_general(a, b, (((1,), (1,)), ((), ())), preferred_element_type=F32)


def _dot_tn(a, b):
    return lax.dot_general(a, b, (((0,), (0,)), ((), ())), preferred_element_type=F32)


def _neg_abs(x):
    bits = lax.bitcast_convert_type(x, jnp.uint32) | jnp.uint32(0x80000000)
    return lax.bitcast_convert_type(bits, F32)


def _sigmoid(x):
    return 1.0 / (1.0 + jnp.exp(-x))


def _softplus(x):
    return jnp.maximum(x, 0.0) + jnp.log1p(jnp.exp(-jnp.abs(x)))


def _rmsnorm(x, w):
    ms = jnp.mean(x * x, axis=-1, keepdims=True)
    return (x * lax.rsqrt(ms + EPS)) * w


def _norm_mod(x, nw, shift, scale):
    return _rmsnorm(x, nw) * (1.0 + scale) + shift


def _mod_kernel(c_ref, w_ref, b_ref, o_ref, *, d):
    c = c_ref[...]
    ca = c * _sigmoid(c)
    w = w_ref[0]
    c_hi = ca.astype(BF16)
    c_lo = (ca - c_hi.astype(F32)).astype(BF16)
    w_hi = w.astype(BF16)
    w_lo = (w - w_hi.astype(F32)).astype(BF16)
    acc = _dot(c_hi, w_hi) + (_dot(c_hi, w_lo) + _dot(c_lo, w_hi)) + b_ref[0]
    for r in range(MOD_PER_STEP):
        o_ref[r] = acc[:, r * d:(r + 1) * d]


def _mod_call(c, ada_w, ada_b):
    depth, d, _ = ada_w.shape
    b = c.shape[0]
    steps = N_MOD // MOD_PER_STEP
    return pl.pallas_call(
        functools.partial(_mod_kernel, d=d),
        grid=(depth, steps),
        in_specs=[
            pl.BlockSpec((b, d), lambda l, j: (0, 0)),
            pl.BlockSpec((1, d, MOD_PER_STEP * d), lambda l, j: (l, 0, j)),
            pl.BlockSpec((1, 1, MOD_PER_STEP * d), lambda l, j: (l, 0, j)),
        ],
        out_specs=pl.BlockSpec((MOD_PER_STEP, b, d), lambda l, j: (l * steps + j, 0, 0)),
        out_shape=jax.ShapeDtypeStruct((depth * N_MOD, b, d), F32),
        compiler_params=_params(),
        name="adaln_table",
    )(c, ada_w, ada_b.reshape(depth, 1, N_MOD * d))


def _ffn_kernel(x_ref, nw_ref, sh_ref, sc_ref, g_ref, w13_ref, w2_ref, *rest, d_ff, tm, sub, final):
    if final:
        fw_ref, o_ref = rest
    else:
        (o_ref,) = rest
    nw, sh, sc = nw_ref[...], sh_ref[0], sc_ref[0]
    half_g = 0.5 * g_ref[0]
    tiles = -(-d_ff // MXU_COLS)
    per = -(-tiles // FFN_COL_CHUNKS)
    bounds = [min(c * per * MXU_COLS, d_ff) for c in range(FFN_COL_CHUNKS + 1)]
    for r0 in range(0, tm, sub):
        x = x_ref[0, r0:r0 + sub, :]
        h = _norm_mod(x, nw, sh, sc).astype(BF16)
        y = None
        for c0, c1 in zip(bounds[:-1], bounds[1:]):
            gate = _dot(h, w13_ref[:, c0:c1])
            up = _dot(h, w13_ref[:, d_ff + c0:d_ff + c1])
            act = (gate * _sigmoid(gate) * up).astype(BF16)
            part = _dot(act, w2_ref[c0:c1, :])
            y = part if y is None else y + part
        out = x + half_g * y
        if final:
            out = _rmsnorm(out, fw_ref[...])
        o_ref[0, r0:r0 + sub, :] = out


def _ffn_call(x, nw, mods, base, w13, w2, layer, which, final_w=None):
    b, s, d = x.shape
    d_ff = w2.shape[2]
    tm = min(FFN_ROWS, s)
    sub = min(FFN_SUB, tm)
    assert tm % sub == 0 and d_ff % LANES == 0
    tok = pl.BlockSpec((1, tm, d), lambda i, t: (i, t, 0))
    vecs = _mod_specs(d, base, 1)
    pick = lambda i, t: (layer, which, 0, 0)
    w13_spec = pl.BlockSpec((None, None, d, 2 * d_ff), pick, pipeline_mode=pl.Buffered(1))
    w2_spec = pl.BlockSpec((None, None, d_ff, d), pick, pipeline_mode=pl.Buffered(1))
    in_specs = [tok, _resident((1, d)), *vecs, w13_spec, w2_spec]
    args = [x, nw.reshape(1, d), mods, mods, mods, w13, w2]
    if final_w is not None:
        in_specs.append(_resident((1, d)))
        args.append(final_w.reshape(1, d))
    return pl.pallas_call(
        functools.partial(_ffn_kernel, d_ff=d_ff, tm=tm, sub=sub, final=final_w is not None),
        grid=(b, s // tm),
        in_specs=in_specs,
        out_specs=tok,
        out_shape=jax.ShapeDtypeStruct(x.shape, F32),
        compiler_params=_params(),
        name="swiglu_final" if final_w is not None else "swiglu",
    )(*args)


def _gelu_tanh(x):
    return x * (0.5 * (1.0 + jnp.tanh(0.7978845608028654 * (x + 0.044715 * (x * x * x)))))


def _rg_kernel(x_ref, nw_ref, sh_ref, sc_ref, g_ref, perm_ref, permt_ref, win_ref, cw_ref, cb_ref, gw_ref, gb_ref,
               lam_ref, wout_ref, o_ref, conv_scr, h_scr, *, nb, ts, sub, width):
    t = pl.program_id(1)
    rows = nb * sub
    d = x_ref.shape[2]
    blk = width // RG_HEADS
    tail_rows = (CONV_WIDTH - 1) * nb

    @pl.when(t == 0)
    def _():
        conv_scr[...] = jnp.zeros_like(conv_scr)
        h_scr[...] = jnp.zeros_like(h_scr)

    nw, sh, sc, g = nw_ref[...], sh_ref[...], sc_ref[...], g_ref[...]
    cw, cb = cw_ref[...], cb_ref[...]
    decay = -RG_C * _softplus(-lam_ref[...])
    first_rows = lax.broadcasted_iota(jnp.int32, (rows, 1), 0) < nb
    nsub = ts // sub
    st = {"tail": conv_scr[...], "h": h_scr[...]}
    xs, ys, xbs, xcs, gxs, gas, zs = {}, {}, {}, {}, {}, {}, {}

    def project(si):
        x = x_ref[:, si * sub:(si + 1) * sub, :]
        h = _norm_mod(x, nw, sh, sc).reshape(rows, d).astype(BF16)
        hp = _dot(perm_ref[...], h).astype(BF16)
        yx = _dot(hp, win_ref[...])
        xs[si] = x
        ys[si] = _gelu_tanh(yx[:, :width])
        xbs[si] = yx[:, width:]

    def conv_gates(si):
        xb = xbs.pop(si)
        ext = jnp.concatenate([st["tail"], xb], axis=0)
        xc = cb
        for j in range(CONV_WIDTH):
            xc = xc + ext[j * nb:j * nb + rows] * cw[j:j + 1]
        st["tail"] = xb[rows - tail_rows:rows]
        xcb = xc.astype(BF16)
        gx, ga = [], []
        for hd in range(RG_HEADS):
            gts = _dot(xcb[:, hd * blk:(hd + 1) * blk], gw_ref[hd]) + gb_ref[hd]
            gx.append(gts[:, :blk])
            ga.append(gts[:, blk:])
        xcs[si] = xc
        gxs[si] = jnp.concatenate(gx, axis=1)
        gas[si] = jnp.concatenate(ga, axis=1)

    def recur_out(si):
        log_a = _sigmoid(gas.pop(si)) * decay
        a = jnp.exp(log_a)
        m2 = -jnp.tanh(log_a) * (a * a + 1.0)
        mult = jnp.where(m2 > 0.0, m2 * lax.rsqrt(m2), 0.0)
        if si == 0:
            mult = jnp.where(jnp.logical_and(first_rows, t == 0), 1.0, mult)
        u = _sigmoid(gxs.pop(si)) * xcs.pop(si) * mult
        hcar = st["h"]
        hs = []
        for ti in range(sub):
            hcar = a[ti * nb:(ti + 1) * nb] * hcar + u[ti * nb:(ti + 1) * nb]
            hs.append(hcar)
        st["h"] = hcar
        zs[si] = (jnp.concatenate(hs, axis=0) * ys.pop(si)).astype(BF16)

    def project_out(si):
        zn = _dot(permt_ref[...], zs.pop(si)).astype(BF16)
        out = _dot(zn, wout_ref[...]).reshape(nb, sub, d)
        o_ref[:, si * sub:(si + 1) * sub, :] = xs.pop(si) + g * out

    for step in range(nsub + 3):
        if step < nsub:
            project(step)
        if 0 <= step - 1 < nsub:
            conv_gates(step - 1)
        if 0 <= step - 2 < nsub:
            recur_out(step - 2)
        if 0 <= step - 3 < nsub:
            project_out(step - 3)
    conv_scr[...] = st["tail"]
    h_scr[...] = st["h"]


def _rg_call(x, nw, mods, base, w_in, conv_w, conv_b, gate_w, gate_b, lam, w_out):
    b, s, d = x.shape
    width = w_out.shape[0]
    blk = width // RG_HEADS
    nb = SUBLANES
    ts = min(RG_STEPS, s)
    sub = min(RG_SUB, ts)
    rows = nb * sub
    assert b % nb == 0 and s % ts == 0 and ts % sub == 0 and sub % SUBLANES == 0
    r = jnp.arange(rows, dtype=jnp.int32)
    perm = ((r[:, None] % nb) * sub + r[:, None] // nb == r[None, :]).astype(BF16)
    tok = pl.BlockSpec((nb, ts, d), lambda i, t: (i, t, 0))
    vecs = _mod_specs(d, base, nb)
    return pl.pallas_call(
        functools.partial(_rg_kernel, nb=nb, ts=ts, sub=sub, width=width),
        grid=(b // nb, s // ts),
        in_specs=[tok, _resident((1, d)), *vecs, _resident((rows, rows)), _resident((rows, rows)),
                  _resident(w_in.shape), _resident((CONV_WIDTH, width)), _resident((1, width)),
                  _resident(gate_w.shape), _resident((RG_HEADS, 1, 2 * blk)), _resident((1, width)),
                  _resident(w_out.shape)],
        out_specs=tok,
        out_shape=jax.ShapeDtypeStruct(x.shape, F32),
        scratch_shapes=[pltpu.VMEM(((CONV_WIDTH - 1) * nb, width), F32), pltpu.VMEM((nb, width), F32)],
        compiler_params=_params(),
        name="rglru_mixer",
    )(x, nw.reshape(1, d), mods, mods, mods, perm, perm.T, w_in, conv_w, conv_b.reshape(1, width), gate_w,
      gate_b.reshape(RG_HEADS, 1, 2 * blk), lam.reshape(1, width), w_out)


def _cumsum_rows(v, rows):
    n = v.shape[1]
    groups = rows // SUBLANES
    v3 = v.reshape(groups, SUBLANES, n)
    srow = lax.broadcasted_iota(jnp.int32, v3.shape, 1)
    for k in (1, 2, 4):
        v3 = v3 + jnp.where(srow >= k, pltpu.roll(v3, k, 1), 0.0)
    out, carry = [], None
    for gi in range(groups):
        blk = v3[gi] if carry is None else v3[gi] + carry
        out.append(blk)
        carry = blk[SUBLANES - 1:SUBLANES, :]
    return jnp.concatenate(out, axis=0)


def _anchor(b, half, rows):
    n = b.shape[1]
    if 2 * half >= SUBLANES:
        v = b.reshape(rows // (2 * half), 2 * half, n)
        return jnp.broadcast_to(v[:, half - 1:half, :], v.shape).reshape(rows, n)
    v = b.reshape(rows // SUBLANES, SUBLANES, n)
    srow = lax.broadcasted_iota(jnp.int32, v.shape, 1)
    if half == 2:
        lo = jnp.broadcast_to(v[:, 1:2, :], v.shape)
        hi = jnp.broadcast_to(v[:, 5:6, :], v.shape)
        return jnp.where(srow < 4, lo, hi).reshape(rows, n)
    assert half == 1
    return jnp.where((srow & 1) == 1, pltpu.roll(v, 1, 1), v).reshape(rows, n)


def _block_diag(a, b):
    top = jnp.concatenate([a, jnp.zeros_like(b)], axis=1)
    bot = jnp.concatenate([jnp.zeros_like(a), b], axis=1)
    return jnp.concatenate([top, bot], axis=0)


def _gla_masks(rows):
    qi = lax.broadcasted_iota(jnp.int32, (rows, 2 * rows), 0)
    kj = lax.broadcasted_iota(jnp.int32, (rows, 2 * rows), 1) & (rows - 1)
    masks = []
    half, lvl = 1, 0
    while half < rows:
        same = (qi >> (lvl + 1)) == (kj >> (lvl + 1))
        masks.append(same & ((qi & half) != 0) & ((kj & half) == 0))
        half, lvl = half * 2, lvl + 1
    return qi == kj, masks


def _gla_prep(q, k, v, lf2, *, rows):
    b = _cumsum_rows(lf2, rows)
    b_last = b[rows - 1:rows, :]
    qe = (q * jnp.exp2(b)).astype(BF16)
    kd = (k * jnp.exp2(b_last - b)).astype(BF16)
    eb = jnp.exp2(b_last)
    vb = v.astype(BF16)
    rown = lax.broadcasted_iota(jnp.int32, b.shape, 0)
    mixed = []
    half = 1
    while half < rows:
        if half >= SUBLANES:
            parts = []
            for r0 in range(0, rows, half):
                if r0 & half:
                    parts.append(q[r0:r0 + half] * jnp.exp2(b[r0:r0 + half] - b[r0 - 1:r0]))
                else:
                    parts.append(k[r0:r0 + half] * jnp.exp2(b[r0 + half - 1:r0 + half] - b[r0:r0 + half]))
            mixed.append(jnp.concatenate(parts, axis=0).astype(BF16))
        else:
            e = jnp.exp2(_neg_abs(b - _anchor(b, half, rows)))
            mixed.append((jnp.where((rown & half) != 0, q, k) * e).astype(BF16))
        half *= 2
    return dict(qe=qe, kd=kd, eb=eb, vb=vb, mixed=mixed, qk=q * k)


def _gla_scores(ctx, st_ref, eye, masks, *, heads, dk, dv, rows):
    qe, kd, eb, vb, mixed, qk = (ctx[n] for n in ("qe", "kd", "eb", "vb", "mixed", "qk"))
    ks = [slice(hd * dk, (hd + 1) * dk) for hd in range(heads)]
    vs = [slice(hd * dv, (hd + 1) * dv) for hd in range(heads)]
    inter = []
    for hd in range(heads):
        st = st_ref[hd]
        inter.append(_dot_nt(qe[:, ks[hd]], st.astype(BF16)))
        st_ref[hd] = st * eb[:, ks[hd]] + _dot_tn(vb[:, vs[hd]], kd[:, ks[hd]])
    left = lax.broadcasted_iota(jnp.int32, (rows, 2 * rows), 1) < rows
    scores = []
    for h0 in range(0, heads, 2):
        kc = slice(h0 * dk, (h0 + 2) * dk)
        diag = jnp.where(left, jnp.sum(qk[:, ks[h0]], axis=-1, keepdims=True),
                         jnp.sum(qk[:, ks[h0 + 1]], axis=-1, keepdims=True))
        sc = jnp.where(eye, diag, 0.0)
        for m, mk in zip(mixed, masks):
            sc = jnp.where(mk, _dot_nt(m[:, kc], _block_diag(m[:, ks[h0]], m[:, ks[h0 + 1]])), sc)
        scores.append(sc.astype(BF16))
    ctx["inter"] = inter
    ctx["scores"] = scores


def _gla_out(ctx, *, heads, dv):
    vb, inter, scores = ctx["vb"], ctx["inter"], ctx["scores"]
    vs = [slice(hd * dv, (hd + 1) * dv) for hd in range(heads)]
    outs = []
    for h0 in range(0, heads, 2):
        o2 = _dot(scores[h0 // 2], _block_diag(vb[:, vs[h0]], vb[:, vs[h0 + 1]]))
        outs.append(o2 + jnp.concatenate([inter[h0], inter[h0 + 1]], axis=1))
    return jnp.concatenate(outs, axis=1)


def _gla_tile(n_chunks, load, store, st_ref, *, heads, dk, dv, rows):
    eye, masks = _gla_masks(rows)
    ctxs = {}
    for step in range(n_chunks + 2):
        if step < n_chunks:
            ctxs[step] = _gla_prep(*load(step), rows=rows)
        if 0 <= step - 1 < n_chunks:
            _gla_scores(ctxs[step - 1], st_ref, eye, masks, heads=heads, dk=dk, dv=dv, rows=rows)
        if 0 <= step - 2 < n_chunks:
            store(step - 2, _gla_out(ctxs.pop(step - 2), heads=heads, dv=dv))


def _hg_kernel(x_ref, nw_ref, sh_ref, sc_ref, g_ref, win_ref, lbl_ref, onw_ref, wout_ref, o_ref,
               p_scr, pv_scr, lf_scr, o_scr, st_scr, *, tt, d, layer, heads):
    t = pl.program_id(1)

    @pl.when(t == 0)
    def _():
        st_scr[...] = jnp.zeros_like(st_scr)

    x = x_ref[0]
    h = _norm_mod(x, nw_ref[...], sh_ref[0], sc_ref[0]).astype(BF16)
    p_scr[:, d:2 * d] = _dot(h, win_ref[:, d:2 * d])

    lg = lbl_ref[...]
    rows_l = [lg[i:i + 1, :] for i in range(lg.shape[0])]
    mx = functools.reduce(jnp.maximum, rows_l)
    ex = [jnp.exp(r - mx) for r in rows_l]
    den = functools.reduce(lambda u, w: u + w, ex)
    lb = jnp.zeros_like(den)
    for i in range(1, layer + 1):
        lb = lb + ex[i] / den
    log_lb = jnp.log(lb)
    log_1m = jnp.log1p(-lb)
    one_m = 1.0 - lb

    log_lb2 = log_lb * LOG2E
    log_1m2 = log_1m * LOG2E

    fz = p_scr[:, d:2 * d]
    fz2 = fz * LOG2E
    e = jnp.exp2(_neg_abs(fz2))
    u = 1.0 + e
    bv2 = (log_1m2 + jnp.minimum(fz2, 0.0)) - jnp.log2(u)
    lf_scr[...] = jnp.maximum(log_lb2, bv2) + jnp.log2(1.0 + jnp.exp2(_neg_abs(log_lb2 - bv2)))
    p_scr[:, d:2 * d] = one_m * (jnp.where(fz >= 0.0, e, 1.0) / u)
    p_scr[:, 0:d] = _dot(h, win_ref[:, 0:d])
    pv_scr[...] = _dot(h, win_ref[:, 2 * d:3 * d]).astype(BF16)
    p_scr[:, 2 * d:3 * d] = _dot(h, win_ref[:, 3 * d:4 * d])

    def load(ci):
        rs = slice(ci * GLA_CHUNK, (ci + 1) * GLA_CHUNK)
        return p_scr[rs, 0:d], p_scr[rs, d:2 * d], pv_scr[rs, :], lf_scr[rs, :]

    def store(ci, o):
        o_scr[ci * GLA_CHUNK:(ci + 1) * GLA_CHUNK, :] = o

    _gla_tile(tt // GLA_CHUNK, load, store, st_scr, heads=heads, dk=d // heads, dv=d // heads, rows=GLA_CHUNK)

    o = _rmsnorm(o_scr[...], onw_ref[...]) * _sigmoid(p_scr[:, 2 * d:3 * d])
    out = _dot(o.astype(BF16), wout_ref[...])
    o_ref[0] = x + g_ref[0] * out


def _hg_call(x, nw, mods, base, w_in, lb_logits, onw, w_out, layer):
    b, s, d = x.shape
    heads = d // HG_EXPAND
    tt = min(MIX_ROWS, s)
    tok = pl.BlockSpec((1, tt, d), lambda i, t: (i, t, 0))
    vecs = _mod_specs(d, base, 1)
    return pl.pallas_call(
        functools.partial(_hg_kernel, tt=tt, d=d, layer=layer, heads=heads),
        grid=(b, s // tt),
        in_specs=[tok, _resident((1, d)), *vecs, _resident(w_in.shape), _resident(lb_logits.shape),
                  _resident((1, d)), _resident(w_out.shape)],
        out_specs=tok,
        out_shape=jax.ShapeDtypeStruct(x.shape, F32),
        scratch_shapes=[pltpu.VMEM((tt, 3 * d), F32), pltpu.VMEM((tt, d), BF16), pltpu.VMEM((tt, d), F32),
                        pltpu.VMEM((tt, d), F32), pltpu.VMEM((heads, HG_EXPAND, HG_EXPAND), F32)],
        compiler_params=_params(),
        name="hgrn2_mixer",
    )(x, nw.reshape(1, d), mods, mods, mods, w_in, lb_logits, onw.reshape(1, d), w_out)


def _gla_kernel(x_ref, nw_ref, sh_ref, sc_ref, g_ref, win_ref, gw2_ref, gb_ref, onw_ref, wout_ref, o_ref,
                p_scr, pv_scr, lf_scr, o_scr, st_scr, *, tt, kd, vd, heads):
    t = pl.program_id(1)

    @pl.when(t == 0)
    def _():
        st_scr[...] = jnp.zeros_like(st_scr)

    x = x_ref[0]
    h = _norm_mod(x, nw_ref[...], sh_ref[0], sc_ref[0]).astype(BF16)
    c_v, c_low = 2 * kd, 2 * kd + 2 * vd
    a_low = _dot(h, win_ref[:, c_low:c_low + LANES]).astype(BF16)
    z2 = (_dot(a_low, gw2_ref[...]) + gb_ref[...]) * LOG2E
    lf_scr[...] = (jnp.minimum(z2, 0.0) - jnp.log2(1.0 + jnp.exp2(_neg_abs(z2)))) * (1.0 / GLA_LOGIT_NORM)
    p_scr[:, 0:c_v] = _dot(h, win_ref[:, 0:c_v])
    pv_scr[...] = _dot(h, win_ref[:, c_v:c_v + vd]).astype(BF16)
    p_scr[:, c_v:c_v + vd] = _dot(h, win_ref[:, c_v + vd:c_low])
    dk = kd // heads
    dv = vd // heads
    scale = dk ** -0.5

    def load(ci):
        rs = slice(ci * GLA_CHUNK, (ci + 1) * GLA_CHUNK)
        q = p_scr[rs, 0:kd] * scale
        k = p_scr[rs, kd:2 * kd]
        v = pv_scr[rs, :]
        return q, k, v, lf_scr[rs, :]

    def store(ci, o):
        o_scr[ci * GLA_CHUNK:(ci + 1) * GLA_CHUNK, :] = o

    _gla_tile(tt // GLA_CHUNK, load, store, st_scr, heads=heads, dk=dk, dv=dv, rows=GLA_CHUNK)

    gate = p_scr[:, 2 * kd:2 * kd + vd]
    onw = onw_ref[...]
    parts = []
    for hd in range(heads):
        vs = slice(hd * dv, (hd + 1) * dv)
        gh = gate[:, vs]
        parts.append(_rmsnorm(o_scr[:, vs], onw) * (gh * _sigmoid(gh)))
    out = _dot(jnp.concatenate(parts, axis=1).astype(BF16), wout_ref[...])
    o_ref[0] = x + g_ref[0] * out


def _gla_call(x, nw, mods, base, w_in, gate_w2, gate_b, onw, w_out):
    b, s, d = x.shape
    kd = gate_w2.shape[1]
    vd = w_out.shape[0]
    heads = GLA_HEADS
    dv = vd // heads
    tt = min(MIX_ROWS, s)
    cols = 2 * kd + 2 * vd + LANES
    w_in_p = jnp.zeros((d, cols), BF16).at[:, :w_in.shape[1]].set(w_in)
    gw2_p = jnp.zeros((LANES, kd), BF16).at[:GLA_RANK, :].set(gate_w2)
    tok = pl.BlockSpec((1, tt, d), lambda i, t: (i, t, 0))
    vecs = _mod_specs(d, base, 1)
    return pl.pallas_call(
        functools.partial(_gla_kernel, tt=tt, kd=kd, vd=vd, heads=heads),
        grid=(b, s // tt),
        in_specs=[tok, _resident((1, d)), *vecs, _resident(w_in_p.shape), _resident(gw2_p.shape),
                  _resident((1, kd)), _resident((1, dv)), _resident(w_out.shape)],
        out_specs=tok,
        out_shape=jax.ShapeDtypeStruct(x.shape, F32),
        scratch_shapes=[pltpu.VMEM((tt, 2 * kd + vd), F32), pltpu.VMEM((tt, vd), BF16), pltpu.VMEM((tt, kd), F32),
                        pltpu.VMEM((tt, vd), F32),
                        pltpu.VMEM((heads, dv, kd // heads), F32)],
        compiler_params=_params(),
        name="gla_mixer",
    )(x, nw.reshape(1, d), mods, mods, mods, w_in_p, gw2_p, gate_b.reshape(1, kd), onw.reshape(1, dv), w_out)


def kernel(x, c, ada_w, ada_b, norm_w, final_norm_w, ffn_w13, ffn_w2, rg_w_in, rg_conv_w, rg_conv_b, rg_gate_w,
           rg_gate_b, rg_lambda, rg_w_out, hg_w_in, hg_lb_logits, hg_norm_w, hg_w_out, gla_w_in, gla_gate_w2,
           gla_gate_b, gla_norm_w, gla_w_out):
    depth = ada_w.shape[0]
    b, s, d = x.shape
    assert s % min(FFN_ROWS, s) == 0 and s % min(MIX_ROWS, s) == 0 and min(MIX_ROWS, s) % GLA_CHUNK == 0
    mods = _mod_call(c, ada_w, ada_b).reshape(depth * N_MOD, b, 1, d)
    bf = lambda w: w.astype(BF16)
    w13b, w2b = bf(ffn_w13), bf(ffn_w2)
    i_rg = i_hg = i_gla = 0
    for l in range(depth):
        base = l * N_MOD
        x = _ffn_call(x, norm_w[l, 0], mods, base, w13b, w2b, l, 0)
        m = l % N_MIXERS
        if m == 0:
            x = _rg_call(x, norm_w[l, 1], mods, base + 3, bf(rg_w_in[i_rg]), rg_conv_w[i_rg], rg_conv_b[i_rg],
                         bf(rg_gate_w[i_rg]), rg_gate_b[i_rg], rg_lambda[i_rg], bf(rg_w_out[i_rg]))
            i_rg += 1
        elif m == 1:
            x = _hg_call(x, norm_w[l, 1], mods, base + 3, bf(hg_w_in[i_hg]), hg_lb_logits, hg_norm_w[i_hg],
                         bf(hg_w_out[i_hg]), l)
            i_hg += 1
        else:
            x = _gla_call(x, norm_w[l, 1], mods, base + 3, bf(gla_w_in[i_gla]), bf(gla_gate_w2[i_gla]),
                          gla_gate_b[i_gla], gla_norm_w[i_gla], bf(gla_w_out[i_gla]))
            i_gla += 1
        x = _ffn_call(x, norm_w[l, 2], mods, base + 6, w13b, w2b, l, 1,
                      final_w=final_norm_w if l == depth - 1 else None)
    return x
```
